```python
import math
import jax, jax.numpy as jnp
from jax import lax
import numpy as np

D_MODEL = 4096
BATCH = 2
SEQ = 8192
DEPTH = 2

SSM_WIDTH = 1024
SSM_GROUP = 16
SSM_GROUPS = SSM_WIDTH // SSM_GROUP
SSM_STATE = 64
MLA_HEADS = 12
MLA_Q_LORA = 1024
MLA_KV_LORA = 512
MLA_NOPE = 128
MLA_ROPE = 64
MLA_V = 128
MLA_QK = MLA_NOPE + MLA_ROPE
ROPE_THETA = 10000.0
DSA_HEADS = 12
DSA_HEAD_DIM = 128
IDX_HEADS = 16
IDX_DIM = 64
DSA_TOPK_MAX = 256
Q_BLOCK = 128
N_BRANCH = 3
BR0 = SSM_WIDTH
BR1 = SSM_WIDTH + MLA_HEADS * MLA_V
MIX_WIDTH = BR1 + DSA_HEADS * DSA_HEAD_DIM
IN_SIZES = (SSM_WIDTH, MLA_Q_LORA, MLA_KV_LORA, MLA_ROPE, DSA_HEADS * DSA_HEAD_DIM, DSA_HEAD_DIM, DSA_HEAD_DIM, IDX_HEADS * IDX_DIM, IDX_DIM, IDX_HEADS, N_BRANCH * D_MODEL)
IN_WIDTH = sum(IN_SIZES)
D_FF_DENSE = 11008
N_EXPERTS = 8
TOP_K_EXPERTS = 2
D_FF_EXPERT = 4096
N_DENSE_LAYERS = (DEPTH + 1) // 2
N_MOE_LAYERS = DEPTH // 2
EPS = 1e-6

kernel_name = 'hybrid_s5_mla_dsa_moe_trunk'


def rms_norm(x, gain):
    xf = x.astype(jnp.float32)
    y = xf * lax.rsqrt(jnp.mean(xf * xf, axis=-1, keepdims=True) + EPS)
    return (y * gain.astype(jnp.float32)).astype(x.dtype)


def rope(x, positions):
    half = x.shape[-1] // 2
    inv = ROPE_THETA ** (-jnp.arange(half, dtype=jnp.float32) / half)
    ang = positions.astype(jnp.float32)[:, :, None, None] * inv
    cos, sin = jnp.cos(ang), jnp.sin(ang)
    xf = x.astype(jnp.float32)
    x1, x2 = xf[..., :half], xf[..., half:]
    return jnp.concatenate([x1 * cos - x2 * sin, x1 * sin + x2 * cos], axis=-1).astype(x.dtype)


def to_blocks(t):
    b, s = t.shape[:2]
    t = t.reshape((b, s // Q_BLOCK, Q_BLOCK) + t.shape[2:])
    return jnp.moveaxis(t, 1, 0)


def from_blocks(t):
    t = jnp.moveaxis(t, 0, 1)
    return t.reshape((t.shape[0], t.shape[1] * t.shape[2]) + t.shape[3:])


def cmul(ar, ai, br, bi):
    return ar * br - ai * bi, ar * bi + ai * br


def s5_mixer(u, lam_re, lam_im, log_dt, b_re, b_im, c_re, c_im, d_skip, w_glu):
    f32 = jnp.float32
    bsz, seq, _ = u.shape
    uf = u.astype(f32).reshape(bsz, seq, SSM_GROUPS, SSM_GROUP)
    lr, li = lam_re.astype(f32), lam_im.astype(f32)
    dt = jnp.exp(log_dt.astype(f32))[:, None]
    mag = jnp.exp(lr * dt)
    ab_re, ab_im = mag * jnp.cos(li * dt), mag * jnp.sin(li * dt)
    den = lr * lr + li * li
    nr = ab_re - 1.0
    zr = (nr * lr + ab_im * li) / den
    zi = (ab_im * lr - nr * li) / den
    bb_re, bb_im = cmul(zr[..., None], zi[..., None], b_re.astype(f32), b_im.astype(f32))
    bu_re = jnp.einsum('gph,bsgh->bsgp', bb_re, uf)
    bu_im = jnp.einsum('gph,bsgh->bsgp', bb_im, uf)
    a_re = jnp.broadcast_to(ab_re, bu_re.shape)
    a_im = jnp.broadcast_to(ab_im, bu_im.shape)

    def combine(left, right):
        alr, ali, blr, bli = left
        arr, ari, brr, bri = right
        a_r, a_i = cmul(arr, ari, alr, ali)
        m_r, m_i = cmul(arr, ari, blr, bli)
        return a_r, a_i, m_r + brr, m_i + bri

    _, _, x_re, x_im = lax.associative_scan(combine, (a_re, a_im, bu_re, bu_im), axis=1)
    y = jnp.einsum('ghp,bsgp->bsgh', c_re.astype(f32), x_re) - jnp.einsum('ghp,bsgp->bsgh', c_im.astype(f32), x_im)
    y = y + d_skip.astype(f32).reshape(SSM_GROUPS, SSM_GROUP) * uf
    y = jax.nn.gelu(y.reshape(bsz, seq, SSM_WIDTH)).astype(u.dtype)
    a, g = jnp.split(y @ w_glu, 2, axis=-1)
    return a * jax.nn.sigmoid(g)


def mla_mixer(c_q, c_kv, k_pe, positions, q_lat_gain, kv_lat_gain, w_uq, w_ukv, q_gain, k_gain):
    f32 = jnp.float32
    bsz, seq, _ = c_q.shape
    q = (rms_norm(c_q, q_lat_gain) @ w_uq).reshape(bsz, seq, MLA_HEADS, MLA_QK)
    kv = (rms_norm(c_kv, kv_lat_gain) @ w_ukv).reshape(bsz, seq, MLA_HEADS, MLA_NOPE + MLA_V)
    k_nope, v = kv[..., :MLA_NOPE], kv[..., MLA_NOPE:]
    k = jnp.concatenate([k_nope, jnp.broadcast_to(k_pe[:, :, None, :], (bsz, seq, MLA_HEADS, MLA_ROPE))], axis=-1)
    q = rms_norm(q, q_gain)
    k = rms_norm(k, k_gain)
    q = jnp.concatenate([q[..., :MLA_NOPE], rope(q[..., MLA_NOPE:], positions)], axis=-1)
    k = jnp.concatenate([k[..., :MLA_NOPE], rope(k[..., MLA_NOPE:], positions)], axis=-1)
    scale = MLA_QK ** -0.5
    key_pos = jnp.arange(seq)

    def block(args):
        qb, start = args
        s = jnp.einsum('bqhd,bkhd->bhqk', qb, k).astype(f32) * scale
        qpos = start + jnp.arange(Q_BLOCK)
        s = jnp.where(key_pos[None, :] <= qpos[:, None], s, -jnp.inf)
        p = jax.nn.softmax(s, axis=-1).astype(v.dtype)
        return jnp.einsum('bhqk,bkhd->bqhd', p, v)

    starts = jnp.arange(seq // Q_BLOCK) * Q_BLOCK
    out = from_blocks(lax.map(block, (to_blocks(q), starts)))
    return out.reshape(bsz, seq, MLA_HEADS * MLA_V)


def dsa_mixer(q, k, v, q_idx, k_idx, w_idx, q_gain, k_gain):
    f32 = jnp.float32
    bsz, seq, _ = q.shape
    n_sel = min(DSA_TOPK_MAX, seq // 4)
    q = rms_norm(q.reshape(bsz, seq, DSA_HEADS, DSA_HEAD_DIM), q_gain)
    k = rms_norm(k, k_gain)
    q_idx = q_idx.reshape(bsz, seq, IDX_HEADS, IDX_DIM)
    w_idx = w_idx * (IDX_HEADS ** -0.5)
    key_pos = jnp.arange(seq)
    gather = jax.vmap(lambda t, i: t[i])

    def block(args):
        qb, qib, wb, start = args
        qpos = start + jnp.arange(Q_BLOCK)
        causal = key_pos[None, :] <= qpos[:, None]
        logits = jnp.einsum('bqhd,bsd->bqhs', qib, k_idx).astype(f32) * (IDX_DIM ** -0.5)
        score = jnp.einsum('bqh,bqhs->bqs', wb.astype(f32), jax.nn.relu(logits))
        score = jnp.where(causal[None], score, -jnp.inf)
        _, sel = lax.top_k(score, n_sel)
        valid = sel <= qpos[None, :, None]
        ks = gather(k, sel)
        vs = gather(v, sel)
        s = jnp.einsum('bqhd,bqkd->bhqk', qb, ks).astype(f32) * (DSA_HEAD_DIM ** -0.5)
        s = jnp.where(valid[:, None], s, -jnp.inf)
        p = jax.nn.softmax(s, axis=-1).astype(vs.dtype)
        return jnp.einsum('bhqk,bqkd->bqhd', p, vs)

    starts = jnp.arange(seq // Q_BLOCK) * Q_BLOCK
    out = from_blocks(lax.map(block, (to_blocks(q), to_blocks(q_idx), to_blocks(w_idx), starts)))
    return out.reshape(bsz, seq, DSA_HEADS * DSA_HEAD_DIM)


def hybrid_mixer(x, positions, norm_gain, w_in, q_lat_gain, kv_lat_gain, w_uq, w_ukv, mla_q_gain, mla_k_gain, dsa_q_gain, dsa_k_gain, lam_re, lam_im, log_dt, b_re, b_im, c_re, c_im, d_skip, w_glu, w_branch, w_out):
    bsz, seq, _ = x.shape
    xn = rms_norm(x, norm_gain)
    h = xn @ w_in
    offsets = np.cumsum(IN_SIZES)[:-1].tolist()
    u, c_q, c_kv, k_pe, q_c, k_c, v_c, q_i, k_i, w_i, g = jnp.split(h, offsets, axis=-1)
    y_ssm = s5_mixer(u, lam_re, lam_im, log_dt, b_re, b_im, c_re, c_im, d_skip, w_glu)
    y_mla = mla_mixer(c_q, c_kv, k_pe, positions, q_lat_gain, kv_lat_gain, w_uq, w_ukv, mla_q_gain, mla_k_gain)
    y_dsa = dsa_mixer(q_c, k_c, v_c, q_i, k_i, w_i, dsa_q_gain, dsa_k_gain)
    gates = jax.nn.sigmoid(g.astype(jnp.float32)).astype(x.dtype).reshape(bsz, seq, N_BRANCH, D_MODEL)
    merged = (gates[:, :, 0] * (y_ssm @ w_branch[:BR0])
              + gates[:, :, 1] * (y_mla @ w_branch[BR0:BR1])
              + gates[:, :, 2] * (y_dsa @ w_branch[BR1:]))
    return merged @ w_out


def swiglu(xn, w_gate, w_up, w_down):
    return (jax.nn.silu(xn @ w_gate) * (xn @ w_up)) @ w_down


def moe_swiglu(xn, w_router, w1, w3, w2):
    f32 = jnp.float32
    logits = (xn @ w_router).astype(f32)
    top_val, top_idx = lax.top_k(logits, TOP_K_EXPERTS)
    top_w = jax.nn.softmax(top_val, axis=-1)
    combine = jnp.sum(jax.nn.one_hot(top_idx, N_EXPERTS, dtype=f32) * top_w[..., None], axis=-2).astype(xn.dtype)
    out = jnp.zeros_like(xn)
    for e in range(N_EXPERTS):
        out = out + combine[..., e:e + 1] * swiglu(xn, w1[e], w3[e], w2[e])
    return out


def setup_inputs(seed: int = 0) -> dict:
    key = jax.random.key(seed)
    keys = jax.random.split(key, 32)
    counter = iter(range(32))
    f32 = jnp.float32

    def nxt():
        return keys[next(counter)]

    def nrm(shape, scale):
        return jax.random.normal(nxt(), shape, f32) * scale

    def gain(shape):
        return 1.0 + 0.02 * jax.random.normal(nxt(), shape, f32)

    L, ND, NM = DEPTH, N_DENSE_LAYERS, N_MOE_LAYERS
    G, P, H = SSM_GROUPS, SSM_STATE, SSM_GROUP
    x = jax.random.normal(nxt(), (BATCH, SEQ, D_MODEL), f32)
    positions = jnp.broadcast_to(jnp.arange(SEQ, dtype=jnp.int32), (BATCH, SEQ))
    mix_norm = gain((L, D_MODEL))
    w_in = nrm((L, D_MODEL, IN_WIDTH), D_MODEL ** -0.5)
    mla_q_lat_norm = gain((L, MLA_Q_LORA))
    mla_kv_lat_norm = gain((L, MLA_KV_LORA))
    mla_w_uq = nrm((L, MLA_Q_LORA, MLA_HEADS * MLA_QK), MLA_Q_LORA ** -0.5)
    mla_w_ukv = nrm((L, MLA_KV_LORA, MLA_HEADS * (MLA_NOPE + MLA_V)), MLA_KV_LORA ** -0.5)
    mla_q_norm = gain((L, MLA_QK))
    mla_k_norm = gain((L, MLA_QK))
    dsa_q_norm = gain((L, DSA_HEAD_DIM))
    dsa_k_norm = gain((L, DSA_HEAD_DIM))
    ssm_lam_re = -0.5 + 0.01 * nrm((L, G, P), 1.0)
    ssm_lam_im = math.pi * jnp.arange(P, dtype=f32) + 0.01 * nrm((L, G, P), 1.0)
    ssm_log_dt = jax.random.uniform(nxt(), (L, G), f32, math.log(1e-3), math.log(1e-1))
    ssm_b_re = nrm((L, G, P, H), (2.0 * H) ** -0.5)
    ssm_b_im = nrm((L, G, P, H), (2.0 * H) ** -0.5)
    ssm_c_re = nrm((L, G, H, P), (2.0 * P) ** -0.5)
    ssm_c_im = nrm((L, G, H, P), (2.0 * P) ** -0.5)
    ssm_d = nrm((L, SSM_WIDTH), 1.0)
    ssm_w_glu = nrm((L, SSM_WIDTH, 2 * SSM_WIDTH), SSM_WIDTH ** -0.5)
    w_branch = nrm((L, MIX_WIDTH, D_MODEL), (MLA_HEADS * MLA_V) ** -0.5)
    w_out = nrm((L, D_MODEL, D_MODEL), D_MODEL ** -0.5)
    ffn_norm = gain((L, D_MODEL))
    dense_w_gate = nrm((ND, D_MODEL, D_FF_DENSE), D_MODEL ** -0.5)
    dense_w_up = nrm((ND, D_MODEL, D_FF_DENSE), D_MODEL ** -0.5)
    dense_w_down = nrm((ND, D_FF_DENSE, D_MODEL), D_FF_DENSE ** -0.5)
    moe_w_router = nrm((NM, D_MODEL, N_EXPERTS), D_MODEL ** -0.5)
    moe_w1 = nrm((NM, N_EXPERTS, D_MODEL, D_FF_EXPERT), D_MODEL ** -0.5)
    moe_w3 = nrm((NM, N_EXPERTS, D_MODEL, D_FF_EXPERT), D_MODEL ** -0.5)
    moe_w2 = nrm((NM, N_EXPERTS, D_FF_EXPERT, D_MODEL), D_FF_EXPERT ** -0.5)
    return {'x': x, 'positions': positions, 'mix_norm': mix_norm, 'w_in': w_in,
            'mla_q_lat_norm': mla_q_lat_norm, 'mla_kv_lat_norm': mla_kv_lat_norm,
            'mla_w_uq': mla_w_uq, 'mla_w_ukv': mla_w_ukv, 'mla_q_norm': mla_q_norm, 'mla_k_norm': mla_k_norm,
            'dsa_q_norm': dsa_q_norm, 'dsa_k_norm': dsa_k_norm,
            'ssm_lam_re': ssm_lam_re, 'ssm_lam_im': ssm_lam_im, 'ssm_log_dt': ssm_log_dt,
            'ssm_b_re': ssm_b_re, 'ssm_b_im': ssm_b_im, 'ssm_c_re': ssm_c_re, 'ssm_c_im': ssm_c_im,
            'ssm_d': ssm_d, 'ssm_w_glu': ssm_w_glu, 'w_branch': w_branch, 'w_out': w_out,
            'ffn_norm': ffn_norm, 'dense_w_gate': dense_w_gate, 'dense_w_up': dense_w_up, 'dense_w_down': dense_w_down,
            'moe_w_router': moe_w_router, 'moe_w1': moe_w1, 'moe_w3': moe_w3, 'moe_w2': moe_w2}


def reference(x, positions, mix_norm, w_in, mla_q_lat_norm, mla_kv_lat_norm, mla_w_uq, mla_w_ukv, mla_q_norm, mla_k_norm, dsa_q_norm, dsa_k_norm, ssm_lam_re, ssm_lam_im, ssm_log_dt, ssm_b_re, ssm_b_im, ssm_c_re, ssm_c_im, ssm_d, ssm_w_glu, w_branch, w_out, ffn_norm, dense_w_gate, dense_w_up, dense_w_down, moe_w_router, moe_w1, moe_w3, moe_w2):
    for layer in range(DEPTH):
        x = x + hybrid_mixer(x, positions, mix_norm[layer], w_in[layer],
                             mla_q_lat_norm[layer], mla_kv_lat_norm[layer], mla_w_uq[layer], mla_w_ukv[layer],
                             mla_q_norm[layer], mla_k_norm[layer], dsa_q_norm[layer], dsa_k_norm[layer],
                             ssm_lam_re[layer], ssm_lam_im[layer], ssm_log_dt[layer],
                             ssm_b_re[layer], ssm_b_im[layer], ssm_c_re[layer], ssm_c_im[layer],
                             ssm_d[layer], ssm_w_glu[layer], w_branch[layer], w_out[layer])
        xn = rms_norm(x, ffn_norm[layer])
        i = layer // 2
        if layer % 2 == 0:
            x = x + swiglu(xn, dense_w_gate[i], dense_w_up[i], dense_w_down[i])
        else:
            x = x + moe_swiglu(xn, moe_w_router[i], moe_w1[i], moe_w3[i], moe_w2[i])
    return x
```

```python
import functools
import math

import jax
import jax.numpy as jnp
from jax import lax
from jax.experimental import pallas as pl
from jax.experimental.pallas import tpu as pltpu

F32 = jnp.float32
BF16 = jnp.bfloat16
I32 = jnp.int32

SSM_GROUP = 16
SSM_STATE = 64
MLA_HEADS = 12
MLA_NOPE = 128
MLA_ROPE = 64
MLA_V = 128
DSA_HEADS = 12
DSA_HEAD_DIM = 128
IDX_HEADS = 16
IDX_DIM = 64
DSA_TOPK_MAX = 256
TOP_K_EXPERTS = 2
ROPE_THETA = 10000.0
EPS = 1e-6

LANE = 128
SUBLANE = 8
VMEM_LIMIT_BYTES = 56 * 1024 * 1024
NEG_BIG = -1e30
INT_MIN = -(2 ** 31)


def _params(semantics):
    return pltpu.CompilerParams(dimension_semantics=semantics, vmem_limit_bytes=VMEM_LIMIT_BYTES)


def _sigmoid(x):
    return 1.0 / (1.0 + jnp.exp(-x))


def _pick(n, candidates):
    for c in candidates:
        if n % c == 0:
            return c
    return n


def _rmsnorm_body(x_ref, g_ref, o_ref):
    x = x_ref[...].astype(F32)
    ms = jnp.mean(x * x, axis=-1, keepdims=True)
    o_ref[...] = (x * lax.rsqrt(ms + EPS) * g_ref[...]).astype(o_ref.dtype)


def _rmsnorm(x, gain, *, width=None, col_block=0, out_dtype=BF16):
    m = x.shape[0]
    width = width or x.shape[1]
    tm = _pick(m, (512, 256, 128, 64, 32, 16, 8))
    return pl.pallas_call(
        _rmsnorm_body,
        grid=(m // tm,),
        in_specs=[pl.BlockSpec((tm, width), lambda i: (i, col_block)),
                  pl.BlockSpec((1, width), lambda i: (0, 0))],
        out_specs=pl.BlockSpec((tm, width), lambda i: (i, 0)),
        out_shape=jax.ShapeDtypeStruct((m, width), out_dtype),
        compiler_params=_params(("parallel",)),
        name="rmsnorm",
    )(x, gain.reshape(1, width).astype(F32))


def _mm_body(*refs, n_a, pairs, n_extra, nk, epilogue):
    a_refs = refs[:n_a]
    w_refs = refs[n_a:n_a + len(pairs)]
    e_refs = refs[n_a + len(pairs):n_a + len(pairs) + n_extra]
    o_ref = refs[n_a + len(pairs) + n_extra]
    acc_refs = refs[n_a + len(pairs) + n_extra + 1:]

    a_vals = [a[...].astype(BF16) for a in a_refs]
    parts = [jnp.dot(a_vals[ai], w[...].astype(BF16), preferred_element_type=F32)
             for ai, w in zip(pairs, w_refs)]

    def finish(ps):
        o_ref[...] = epilogue(ps, *[e[...] for e in e_refs]).astype(o_ref.dtype)

    if nk == 1:
        finish(parts)
        return
    k = pl.program_id(2)

    @pl.when(k == 0)
    def _():
        for acc, p in zip(acc_refs, parts):
            acc[...] = p

    @pl.when(k > 0)
    def _():
        for acc, p in zip(acc_refs, parts):
            acc[...] += p

    @pl.when(k == nk - 1)
    def _():
        finish([acc[...] for acc in acc_refs])


def _matmul(a_list, w_list, *, epilogue, out_dtype, tn, tm=None, tk=None, extras=(), aliases=None, name="matmul"):
    m = a_list[0].shape[0]
    n = w_list[0][1].shape[-1]
    kdim = a_list[0].shape[1]
    tm = tm or _pick(m, (1024, 512, 256, 128, 64, 32, 16, 8))
    nk = 1 if tk is None else kdim // tk
    grid = (m // tm, n // tn, nk)
    in_specs = []
    for a in a_list:
        ka = a.shape[1] if nk == 1 else tk
        in_specs.append(pl.BlockSpec((tm, ka), lambda i, j, k: (i, k)))
    for _, w, prefix in w_list:
        kw = w.shape[-2] if nk == 1 else tk
        in_specs.append(pl.BlockSpec((None,) * len(prefix) + (kw, tn),
                                     lambda i, j, k, prefix=prefix: tuple(prefix) + (k, j)))
    for arr, kind, arg in extras:
        if kind == "mn":
            in_specs.append(pl.BlockSpec((tm, tn), lambda i, j, k, arg=arg: (i, j + arg)))
        elif kind == "row":
            in_specs.append(pl.BlockSpec((1, tn), lambda i, j, k: (0, j)))
        else:
            in_specs.append(pl.BlockSpec((tm, arr.shape[1]), lambda i, j, k: (i, 0)))
    pairs = tuple(ai for ai, _, _ in w_list)
    body = functools.partial(_mm_body, n_a=len(a_list), pairs=pairs, n_extra=len(extras), nk=nk, epilogue=epilogue)
    scratch = [pltpu.VMEM((tm, tn), F32) for _ in pairs] if nk > 1 else []
    return pl.pallas_call(
        body,
        grid=grid,
        in_specs=in_specs,
        out_specs=pl.BlockSpec((tm, tn), lambda i, j, k: (i, j)),
        out_shape=jax.ShapeDtypeStruct((m, n), out_dtype),
        scratch_shapes=scratch,
        input_output_aliases=aliases or {},
        compiler_params=_params(("parallel", "parallel", "arbitrary")),
        name=name,
    )(*a_list, *[w for _, w, _ in w_list], *[arr for arr, _, _ in extras])


def _ep_plain(ps):
    return ps[0]


def _ep_residual(ps, res):
    return res + ps[0]


def _ep_swiglu(ps):
    g = ps[0]
    return g * _sigmoid(g) * ps[1]


def _ep_glu(ps):
    return ps[0] * _sigmoid(ps[1])


def _ep_s5_out(ps, u, d):
    y = ps[0] + ps[1] + d * u
    return jax.nn.gelu(y)


def _ep_branches(ps, g0, g1, g2):
    return _sigmoid(g0) * ps[0] + _sigmoid(g1) * ps[1] + _sigmoid(g2) * ps[2]


def _ep_expert_acc(ps, prev, combine, *, expert):
    return prev + combine[:, expert:expert + 1] * ps[0]


def _s5_scan_body(bre_ref, bim_ref, a_ref, o_re_ref, o_im_ref, xr_ref, xi_ref, car_ref, *, steps, seg_len):
    t = pl.program_id(2)
    lw = bre_ref.shape[-1]

    @pl.when(t == 0)
    def _():
        car_ref[...] = jnp.zeros_like(car_ref)

    ar = jnp.broadcast_to(a_ref[0:1, :], (SUBLANE, lw))
    ai = jnp.broadcast_to(a_ref[1:2, :], (SUBLANE, lw))

    def pass1(j, carry):
        xr, xi = carry
        off = pl.multiple_of(j * SUBLANE, SUBLANE)
        nr = ar * xr - ai * xi + bre_ref[0, pl.ds(off, SUBLANE), :]
        ni = ar * xi + ai * xr + bim_ref[0, pl.ds(off, SUBLANE), :]
        xr_ref[pl.ds(off, SUBLANE), :] = nr
        xi_ref[pl.ds(off, SUBLANE), :] = ni
        return nr, ni

    zero = jnp.zeros((SUBLANE, lw), F32)
    er, ei = lax.fori_loop(0, steps, pass1, (zero, zero))

    pr = a_ref[2:3, :]
    pi = a_ref[3:4, :]
    cr = car_ref[0:1, :]
    ci = car_ref[1:2, :]
    rows_r, rows_i = [], []
    for s in range(SUBLANE):
        rows_r.append(cr)
        rows_i.append(ci)
        nr = pr * cr - pi * ci + er[s:s + 1, :]
        ni = pr * ci + pi * cr + ei[s:s + 1, :]
        cr, ci = nr, ni
    car_ref[0:1, :] = cr
    car_ref[1:2, :] = ci
    cin_r = jnp.concatenate(rows_r, axis=0)
    cin_i = jnp.concatenate(rows_i, axis=0)

    def pass2(j, carry):
        cr_, ci_ = carry
        off = pl.multiple_of(j * SUBLANE, SUBLANE)
        nr = ar * cr_ - ai * ci_
        ni = ar * ci_ + ai * cr_
        o_re_ref[0, pl.ds(off, SUBLANE), :] = (xr_ref[pl.ds(off, SUBLANE), :] + nr).astype(o_re_ref.dtype)
        o_im_ref[0, pl.ds(off, SUBLANE), :] = (xi_ref[pl.ds(off, SUBLANE), :] + ni).astype(o_im_ref.dtype)
        return nr, ni

    lax.fori_loop(0, steps, pass2, (cin_r, cin_i))


def _s5_scan(bu_re, bu_im, a_tab, *, chunk):
    b, s, w = bu_re.shape
    lw = _pick(w, (512, 256, 128))
    steps = chunk // SUBLANE
    body = functools.partial(_s5_scan_body, steps=steps, seg_len=steps)
    blk = pl.BlockSpec((1, chunk, lw), lambda bi, li, ti: (bi, ti, li))
    return pl.pallas_call(
        body,
        grid=(b, w // lw, s // chunk),
        in_specs=[blk, blk, pl.BlockSpec((4, lw), lambda bi, li, ti: (0, li))],
        out_specs=[blk, blk],
        out_shape=[jax.ShapeDtypeStruct((b, s, w), BF16)] * 2,
        scratch_shapes=[pltpu.VMEM((chunk, lw), F32), pltpu.VMEM((chunk, lw), F32), pltpu.VMEM((SUBLANE, lw), F32)],
        compiler_params=_params(("parallel", "parallel", "arbitrary")),
        name="s5_scan",
    )(bu_re, bu_im, a_tab)


def _s5_chunk(s):
    return _pick(s, (512, 256, 128, 64))


def _to_chunk_order(t, chunk):
    b, s, w = t.shape
    return t.reshape(b, s // chunk, SUBLANE, chunk // SUBLANE, w).transpose(0, 1, 3, 2, 4).reshape(b, s, w)


def _from_chunk_order(t, chunk):
    b, s, w = t.shape
    return t.reshape(b, s // chunk, chunk // SUBLANE, SUBLANE, w).transpose(0, 1, 3, 2, 4).reshape(b, s, w)


def _s5_discretise(lam_re, lam_im, log_dt, b_re, b_im, c_re, c_im, steps):
    g, p = lam_re.shape
    h = b_re.shape[-1]
    lr, li = lam_re.astype(F32), lam_im.astype(F32)
    dt = jnp.exp(log_dt.astype(F32))[:, None]
    mag = jnp.exp(lr * dt)
    ab_re, ab_im = mag * jnp.cos(li * dt), mag * jnp.sin(li * dt)
    den = lr * lr + li * li
    nr = ab_re - 1.0
    zr = (nr * lr + ab_im * li) / den
    zi = (ab_im * lr - nr * li) / den
    bb_re = zr[..., None] * b_re - zi[..., None] * b_im
    bb_im = zr[..., None] * b_im + zi[..., None] * b_re
    eye = jnp.eye(g, dtype=F32)
    bd_in_re = jnp.einsum("gph,gk->ghkp", bb_re, eye).reshape(g * h, g * p)
    bd_in_im = jnp.einsum("gph,gk->ghkp", bb_im, eye).reshape(g * h, g * p)
    bd_out_re = jnp.einsum("ghp,gk->gpkh", c_re.astype(F32), eye).reshape(g * p, g * h)
    bd_out_im = -jnp.einsum("ghp,gk->gpkh", c_im.astype(F32), eye).reshape(g * p, g * h)
    pr, pi = ab_re, ab_im
    for _ in range(int(round(math.log2(steps)))):
        pr, pi = pr * pr - pi * pi, 2.0 * pr * pi
    a_tab = jnp.stack([ab_re.reshape(-1), ab_im.reshape(-1), pr.reshape(-1), pi.reshape(-1)])
    return (bd_in_re.astype(BF16), bd_in_im.astype(BF16), bd_out_re.astype(BF16), bd_out_im.astype(BF16), a_tab)


def _s5_mixer(u, bsz, seq, lam_re, lam_im, log_dt, b_re, b_im, c_re, c_im, d_skip, w_glu):
    n, w = u.shape
    chunk = _s5_chunk(seq)
    steps = chunk // SUBLANE
    bin_re, bin_im, bout_re, bout_im, a_tab = _s5_discretise(lam_re, lam_im, log_dt, b_re, b_im, c_re, c_im, steps)
    wst = bin_re.shape[1]
    u_p = _to_chunk_order(u.reshape(bsz, seq, w), chunk).reshape(n, w)
    tn = _pick(wst, (512, 256, 128))
    bu_re = _matmul([u_p], [(0, bin_re, ())], epilogue=_ep_plain, out_dtype=F32, tn=tn, name="s5_bu_re")
    bu_im = _matmul([u_p], [(0, bin_im, ())], epilogue=_ep_plain, out_dtype=F32, tn=tn, name="s5_bu_im")
    x_re, x_im = _s5_scan(bu_re.reshape(bsz, seq, wst), bu_im.reshape(bsz, seq, wst), a_tab, chunk=chunk)
    tnw = _pick(w, (512, 256, 128))
    y = _matmul([x_re.reshape(n, wst), x_im.reshape(n, wst)], [(0, bout_re, ()), (1, bout_im, ())],
                epilogue=_ep_s5_out, out_dtype=BF16, tn=tnw,
                extras=[(u_p, "mn", 0), (d_skip.reshape(1, w).astype(F32), "row", 0)], name="s5_out")
    w_a = w_glu[:, :w].astype(BF16)
    w_g = w_glu[:, w:].astype(BF16)
    out = _matmul([y], [(0, w_a, ()), (0, w_g, ())], epilogue=_ep_glu, out_dtype=BF16, tn=tnw, name="s5_glu")
    return _from_chunk_order(out.reshape(bsz, seq, w), chunk).reshape(n, w)


def _rope_table_body(pos_ref, inv_ref, cos_ref, sin_ref):
    ang = pos_ref[...].astype(F32) * inv_ref[...]
    cos_ref[...] = jnp.cos(ang)
    sin_ref[...] = jnp.sin(ang)


def _rope_tables(positions):
    n = positions.size
    half = MLA_ROPE // 2
    inv = ROPE_THETA ** (-jnp.arange(half, dtype=F32) / half)
    inv = jnp.zeros((1, LANE), F32).at[0, :half].set(inv)
    tm = _pick(n, (512, 256, 128, 64, 32, 16, 8))
    return pl.pallas_call(
        _rope_table_body,
        grid=(n // tm,),
        in_specs=[pl.BlockSpec((tm, 1), lambda i: (i, 0)), pl.BlockSpec((1, LANE), lambda i: (0, 0))],
        out_specs=[pl.BlockSpec((tm, LANE), lambda i: (i, 0))] * 2,
        out_shape=[jax.ShapeDtypeStruct((n, LANE), F32)] * 2,
        compiler_params=_params(("parallel",)),
        name="rope_table",
    )(positions.reshape(n, 1).astype(I32), inv)


def _head_norm_rope(t0, t1, t2, g0, g1, g2, cos, sin, true_dim):
    ss = (jnp.sum(t0 * t0, axis=-1, keepdims=True) + jnp.sum(t1 * t1, axis=-1, keepdims=True)
          + jnp.sum(t2 * t2, axis=-1, keepdims=True))
    inv = lax.rsqrt(ss * (1.0 / true_dim) + EPS)
    n0 = t0 * inv * g0
    n1 = t1 * inv * g1
    n2 = t2 * inv * g2
    o1 = n1 * cos - n2 * sin
    o2 = n1 * sin + n2 * cos
    half = MLA_ROPE // 2
    lane = lax.broadcasted_iota(I32, o1.shape, 1)
    o1 = jnp.where(lane < half, o1, 0.0)
    o2 = jnp.where(lane < half, o2, 0.0)
    return jnp.concatenate([n0, o1 + pltpu.roll(o2, half, 1)], axis=-1)


def _mla_q_prep_body(q_ref, g_ref, cos_ref, sin_ref, o_ref):
    cos, sin = cos_ref[...], sin_ref[...]
    hw = 3 * LANE
    g0, g1, g2 = g_ref[:, 0:LANE], g_ref[:, LANE:2 * LANE], g_ref[:, 2 * LANE:hw]
    for h in range(MLA_HEADS):
        t0 = q_ref[:, h * hw:h * hw + LANE]
        t1 = q_ref[:, h * hw + LANE:h * hw + 2 * LANE]
        t2 = q_ref[:, h * hw + 2 * LANE:(h + 1) * hw]
        o_ref[0, h] = _head_norm_rope(t0, t1, t2, g0, g1, g2, cos, sin, MLA_NOPE + MLA_ROPE).astype(o_ref.dtype)


def _mla_kv_prep_body(kv_ref, pe_ref, g_ref, cos_ref, sin_ref, k_ref, v_ref):
    cos, sin = cos_ref[...], sin_ref[...]
    hw = 3 * LANE
    g0, g1, g2 = g_ref[:, 0:LANE], g_ref[:, LANE:2 * LANE], g_ref[:, 2 * LANE:hw]
    t1 = pe_ref[:, 0:LANE]
    t2 = pe_ref[:, LANE:2 * LANE]
    kvw = MLA_NOPE + MLA_V
    for h in range(MLA_HEADS):
        t0 = kv_ref[:, h * kvw:h * kvw + MLA_NOPE]
        k_ref[0, h] = _head_norm_rope(t0, t1, t2, g0, g1, g2, cos, sin, MLA_NOPE + MLA_ROPE).astype(k_ref.dtype)
        v_ref[0, h] = kv_ref[:, h * kvw + MLA_NOPE:(h + 1) * kvw].astype(v_ref.dtype)


def _pad_rope_gain(gain):
    half = MLA_ROPE // 2
    out = jnp.zeros((3 * LANE,), F32)
    out = out.at[:MLA_NOPE].set(gain[:MLA_NOPE].astype(F32))
    out = out.at[LANE:LANE + half].set(gain[MLA_NOPE:MLA_NOPE + half].astype(F32))
    out = out.at[2 * LANE:2 * LANE + half].set(gain[MLA_NOPE + half:].astype(F32))
    return out.reshape(1, 3 * LANE)


def _flash_body(q_ref, k_ref, v_ref, o_ref, *, tq, scale):
    qi = pl.program_id(2)
    q = q_ref[0, 0]
    dv = v_ref.shape[-1]

    def step(j, carry, masked):
        m, l, acc = carry
        off = pl.multiple_of(j * tq, tq)
        k = k_ref[0, 0, pl.ds(off, tq), :]
        v = v_ref[0, 0, pl.ds(off, tq), :]
        s = lax.dot_general(q, k, (((1,), (1,)), ((), ())), preferred_element_type=F32) * scale
        if masked:
            row = lax.broadcasted_iota(I32, (tq, tq), 0)
            col = lax.broadcasted_iota(I32, (tq, tq), 1)
            s = jnp.where(col <= row, s, NEG_BIG)
        m_new = jnp.maximum(m, jnp.max(s, axis=1, keepdims=True))
        alpha = jnp.exp(m - m_new)
        p = jnp.exp(s - m_new)
        l = alpha * l + jnp.sum(p, axis=1, keepdims=True)
        acc = alpha * acc + jnp.dot(p.astype(BF16), v, preferred_element_type=F32)
        return m_new, l, acc

    init = (jnp.full((tq, 1), NEG_BIG, F32), jnp.zeros((tq, 1), F32), jnp.zeros((tq, dv), F32))
    carry = lax.fori_loop(0, qi, lambda j, c: step(j, c, False), init)
    m, l, acc = step(qi, carry, True)
    o_ref[0] = (acc / l).astype(o_ref.dtype)


def _flash_attention(q, k, v, *, scale):
    b, h, s, dk = q.shape
    dv = v.shape[-1]
    tq = _pick(s, (512, 256, 128))
    return pl.pallas_call(
        functools.partial(_flash_body, tq=tq, scale=scale),
        grid=(b, h, s // tq),
        in_specs=[pl.BlockSpec((1, 1, tq, dk), lambda bi, hi, qi: (bi, hi, qi, 0)),
                  pl.BlockSpec((1, 1, s, dk), lambda bi, hi, qi: (bi, hi, 0, 0)),
                  pl.BlockSpec((1, 1, s, dv), lambda bi, hi, qi: (bi, hi, 0, 0))],
        out_specs=pl.BlockSpec((1, tq, dv), lambda bi, hi, qi: (bi, qi, hi)),
        out_shape=jax.ShapeDtypeStruct((b, s, h * dv), BF16),
        compiler_params=_params(("parallel", "parallel", "arbitrary")),
        name="mla_flash",
    )(q, k, v)


def _mla_mixer(hm, cos, sin, bsz, seq, q_lora, kv_lora, q_lat_gain, kv_lat_gain, w_uq, w_ukv, q_gain, k_gain):
    n = hm.shape[0]
    half = MLA_ROPE // 2
    qk = MLA_NOPE + MLA_ROPE
    cqn = _rmsnorm(hm, q_lat_gain, width=q_lora, col_block=0)
    ckvn = _rmsnorm(hm, kv_lat_gain, width=kv_lora, col_block=q_lora // kv_lora)
    wq = w_uq.reshape(q_lora, MLA_HEADS, qk)
    wq_p = jnp.zeros((q_lora, MLA_HEADS, 3 * LANE), BF16)
    wq_p = wq_p.at[:, :, :MLA_NOPE].set(wq[:, :, :MLA_NOPE].astype(BF16))
    wq_p = wq_p.at[:, :, LANE:LANE + half].set(wq[:, :, MLA_NOPE:MLA_NOPE + half].astype(BF16))
    wq_p = wq_p.at[:, :, 2 * LANE:2 * LANE + half].set(wq[:, :, MLA_NOPE + half:].astype(BF16))
    wq_p = wq_p.reshape(q_lora, MLA_HEADS * 3 * LANE)
    q_raw = _matmul([cqn], [(0, wq_p, ())], epilogue=_ep_plain, out_dtype=F32, tn=3 * LANE, name="mla_uq")
    kv_raw = _matmul([ckvn], [(0, w_ukv.astype(BF16), ())], epilogue=_ep_plain, out_dtype=F32,
                     tn=MLA_NOPE + MLA_V, name="mla_ukv")

    tm = _pick(seq, (256, 128, 64, 32, 16))
    nblk = seq // tm
    qg = _pad_rope_gain(q_gain)
    kg = _pad_rope_gain(k_gain)
    row = lambda bi, si: (bi * nblk + si, 0)
    q = pl.pallas_call(
        _mla_q_prep_body,
        grid=(bsz, nblk),
        in_specs=[pl.BlockSpec((tm, MLA_HEADS * 3 * LANE), row), pl.BlockSpec((1, 3 * LANE), lambda bi, si: (0, 0)),
                  pl.BlockSpec((tm, LANE), row), pl.BlockSpec((tm, LANE), row)],
        out_specs=pl.BlockSpec((1, MLA_HEADS, tm, 2 * LANE), lambda bi, si: (bi, 0, si, 0)),
        out_shape=jax.ShapeDtypeStruct((bsz, MLA_HEADS, seq, 2 * LANE), BF16),
        compiler_params=_params(("parallel", "parallel")),
        name="mla_q_prep",
    )(q_raw, qg, cos, sin)
    pe_block = (q_lora + kv_lora) // (2 * LANE)
    k, v = pl.pallas_call(
        _mla_kv_prep_body,
        grid=(bsz, nblk),
        in_specs=[pl.BlockSpec((tm, MLA_HEADS * (MLA_NOPE + MLA_V)), row),
                  pl.BlockSpec((tm, 2 * LANE), lambda bi, si: (bi * nblk + si, pe_block)),
                  pl.BlockSpec((1, 3 * LANE), lambda bi, si: (0, 0)),
                  pl.BlockSpec((tm, LANE), row), pl.BlockSpec((tm, LANE), row)],
        out_specs=[pl.BlockSpec((1, MLA_HEADS, tm, 2 * LANE), lambda bi, si: (bi, 0, si, 0)),
                   pl.BlockSpec((1, MLA_HEADS, tm, MLA_V), lambda bi, si: (bi, 0, si, 0))],
        out_shape=[jax.ShapeDtypeStruct((bsz, MLA_HEADS, seq, 2 * LANE), BF16),
                   jax.ShapeDtypeStruct((bsz, MLA_HEADS, seq, MLA_V), BF16)],
        compiler_params=_params(("parallel", "parallel")),
        name="mla_kv_prep",
    )(kv_raw, hm, kg, cos, sin)
    return _flash_attention(q, k, v, scale=qk ** -0.5).reshape(n, MLA_HEADS * MLA_V)


def _dsa_prep_body(hd_q_ref, hd_k_ref, hd_v_ref, hi_q_ref, hi_k_ref, hi_w_ref, qg_ref, kg_ref,
                   q_ref, k_ref, v_ref, qi_ref, klo_ref, khi_ref, w_ref):
    def norm(t, g):
        ms = jnp.mean(t * t, axis=-1, keepdims=True)
        return t * lax.rsqrt(ms + EPS) * g

    qg = qg_ref[...]
    for h in range(DSA_HEADS):
        q_ref[0, h] = norm(hd_q_ref[:, h * DSA_HEAD_DIM:(h + 1) * DSA_HEAD_DIM], qg).astype(q_ref.dtype)
    k_ref[0] = norm(hd_k_ref[...], kg_ref[...]).astype(k_ref.dtype)
    v_ref[0] = hd_v_ref[...].astype(v_ref.dtype)
    qi_ref[0] = hi_q_ref[...].astype(qi_ref.dtype)
    ki = hi_k_ref[...]
    lane = lax.broadcasted_iota(I32, ki.shape, 1)
    ki = jnp.where(lane < IDX_DIM, ki, 0.0)
    klo_ref[0] = ki.astype(klo_ref.dtype)
    khi_ref[0] = pltpu.roll(ki, IDX_DIM, 1).astype(khi_ref.dtype)
    w_ref[0] = hi_w_ref[...] * ((IDX_HEADS ** -0.5) * (IDX_DIM ** -0.5))


def _float_key(x):
    bits = pltpu.bitcast(x, I32)
    return jnp.where(bits < 0, bits ^ jnp.int32(0x7FFFFFFF), bits)


def _dsa_body(q_ref, k_ref, v_ref, qi_ref, klo_ref, khi_ref, w_ref, o_ref, keys_ref, *, tq, tk, n_sel, scale):
    qb = pl.program_id(1)
    n_kv = (qb * tq + tq + tk - 1) // tk
    row = lax.broadcasted_iota(I32, (tq, tk), 0) + qb * tq
    col0 = lax.broadcasted_iota(I32, (tq, tk), 1)

    w = w_ref[0]

    def score_tile(j, _):
        off = pl.multiple_of(j * tk, tk)
        klo = klo_ref[0, pl.ds(off, tk), :]
        khi = khi_ref[0, pl.ds(off, tk), :]
        acc = jnp.zeros((tq, tk), F32)
        for hp in range(IDX_HEADS // 2):
            q2 = qi_ref[0, :, hp * LANE:(hp + 1) * LANE]
            for half, kk in ((0, klo), (1, khi)):
                h = 2 * hp + half
                logit = lax.dot_general(q2, kk, (((1,), (1,)), ((), ())), preferred_element_type=F32)
                acc = acc + w[:, h:h + 1] * jnp.maximum(logit, 0.0)
        key = _float_key(acc + 0.0)
        keys_ref[:, pl.ds(off, tk)] = jnp.where(col0 + off <= row, key, INT_MIN)
        return 0

    lax.fori_loop(0, n_kv, score_tile, 0)

    def bit_step(b, lo):
        cand = lo + lax.shift_left(jnp.int32(1), 31 - b)

        def count_tile(j, part):
            off = pl.multiple_of(j * tk, tk)
            for c in range(tk // LANE):
                kc = keys_ref[:, pl.ds(off + c * LANE, LANE)]
                part = part + jnp.where(kc >= cand, 1.0, 0.0)
            return part

        part = lax.fori_loop(0, n_kv, count_tile, jnp.zeros((tq, LANE), F32))
        cnt = jnp.sum(part, axis=1, keepdims=True)
        return jnp.where(cnt >= n_sel, cand, lo)

    thr = lax.fori_loop(0, 32, bit_step, jnp.full((tq, LANE), INT_MIN, I32))
    thr = jnp.maximum(thr, INT_MIN + 1)

    for h in range(DSA_HEADS):
        q = q_ref[0, h]

        def attend(j, carry, q=q):
            m, l, acc = carry
            off = pl.multiple_of(j * tk, tk)
            k = k_ref[0, pl.ds(off, tk), :]
            v = v_ref[0, pl.ds(off, tk), :]
            s = lax.dot_general(q, k, (((1,), (1,)), ((), ())), preferred_element_type=F32) * scale
            sel = jnp.concatenate([keys_ref[:, pl.ds(off + c * LANE, LANE)] >= thr for c in range(tk // LANE)], axis=1)
            s = jnp.where(sel, s, NEG_BIG)
            m_new = jnp.maximum(m, jnp.max(s, axis=1, keepdims=True))
            alpha = jnp.exp(m - m_new)
            p = jnp.where(sel, jnp.exp(s - m_new), 0.0)
            l = alpha * l + jnp.sum(p, axis=1, keepdims=True)
            acc = alpha * acc + jnp.dot(p.astype(BF16), v, preferred_element_type=F32)
            return m_new, l, acc

        init = (jnp.full((tq, 1), NEG_BIG, F32), jnp.zeros((tq, 1), F32), jnp.zeros((tq, DSA_HEAD_DIM), F32))
        m, l, acc = lax.fori_loop(0, n_kv, attend, init)
        o_ref[0, :, h * DSA_HEAD_DIM:(h + 1) * DSA_HEAD_DIM] = (acc / l).astype(o_ref.dtype)


def _dsa_mixer(hd, hi, bsz, seq, q_gain, k_gain):
    n = hd.shape[0]
    qw = DSA_HEADS * DSA_HEAD_DIM
    iw = IDX_HEADS * IDX_DIM
    tm = _pick(seq, (256, 128, 64, 32, 16))
    nblk = seq // tm

    def cols(width, block):
        return pl.BlockSpec((tm, width), lambda bi, si: (bi * nblk + si, block))

    per_tok = lambda width: pl.BlockSpec((1, tm, width), lambda bi, si: (bi, si, 0))
    q, k, v, qi, klo, khi, w = pl.pallas_call(
        _dsa_prep_body,
        grid=(bsz, nblk),
        in_specs=[cols(qw, 0), cols(LANE, qw // LANE), cols(LANE, qw // LANE + 1),
                  cols(iw, 0), cols(LANE, iw // LANE), cols(LANE, iw // LANE + 1),
                  pl.BlockSpec((1, DSA_HEAD_DIM), lambda bi, si: (0, 0)),
                  pl.BlockSpec((1, DSA_HEAD_DIM), lambda bi, si: (0, 0))],
        out_specs=[pl.BlockSpec((1, DSA_HEADS, tm, DSA_HEAD_DIM), lambda bi, si: (bi, 0, si, 0)),
                   per_tok(DSA_HEAD_DIM), per_tok(DSA_HEAD_DIM), per_tok(iw), per_tok(LANE), per_tok(LANE),
                   per_tok(LANE)],
        out_shape=[jax.ShapeDtypeStruct((bsz, DSA_HEADS, seq, DSA_HEAD_DIM), BF16),
                   jax.ShapeDtypeStruct((bsz, seq, DSA_HEAD_DIM), BF16),
                   jax.ShapeDtypeStruct((bsz, seq, DSA_HEAD_DIM), BF16),
                   jax.ShapeDtypeStruct((bsz, seq, iw), BF16),
                   jax.ShapeDtypeStruct((bsz, seq, LANE), BF16),
                   jax.ShapeDtypeStruct((bsz, seq, LANE), BF16),
                   jax.ShapeDtypeStruct((bsz, seq, LANE), F32)],
        compiler_params=_params(("parallel", "parallel")),
        name="dsa_prep",
    )(hd, hd, hd, hi, hi, hi, q_gain.reshape(1, -1).astype(F32), k_gain.reshape(1, -1).astype(F32))

    tq = _pick(seq, (256, 128))
    tk = _pick(seq, (512, 256, 128))
    n_sel = min(DSA_TOPK_MAX, seq // 4)
    whole = lambda width: pl.BlockSpec((1, seq, width), lambda bi, qb: (bi, 0, 0))
    out = pl.pallas_call(
        functools.partial(_dsa_body, tq=tq, tk=tk, n_sel=n_sel, scale=DSA_HEAD_DIM ** -0.5),
        grid=(bsz, seq // tq),
        in_specs=[pl.BlockSpec((1, DSA_HEADS, tq, DSA_HEAD_DIM), lambda bi, qb: (bi, 0, qb, 0)),
                  whole(DSA_HEAD_DIM), whole(DSA_HEAD_DIM),
                  pl.BlockSpec((1, tq, iw), lambda bi, qb: (bi, qb, 0)),
                  whole(LANE), whole(LANE),
                  pl.BlockSpec((1, tq, LANE), lambda bi, qb: (bi, qb, 0))],
        out_specs=pl.BlockSpec((1, tq, qw), lambda bi, qb: (bi, qb, 0)),
        out_shape=jax.ShapeDtypeStruct((bsz, seq, qw), BF16),
        scratch_shapes=[pltpu.VMEM((tq, seq), I32)],
        compiler_params=_params(("parallel", "arbitrary")),
        name="dsa_attention",
    )(q, k, v, qi, klo, khi, w)
    return out.reshape(n, qw)


def _pad_cols(w, width):
    return jnp.pad(w, ((0, 0), (0, width - w.shape[1])))


def _hybrid_mixer(x2, cos, sin, bsz, seq, norm_gain, w_in, q_lat_gain, kv_lat_gain, w_uq, w_ukv, mla_q_gain,
                  mla_k_gain, dsa_q_gain, dsa_k_gain, lam_re, lam_im, log_dt, b_re, b_im, c_re, c_im, d_skip, w_glu,
                  w_branch, w_out, layer):
    n, d = x2.shape
    ssm_w = d_skip.shape[-1]
    q_lora = q_lat_gain.shape[-1]
    kv_lora = kv_lat_gain.shape[-1]
    half = MLA_ROPE // 2
    dsa_w = DSA_HEADS * DSA_HEAD_DIM
    idx_w = IDX_HEADS * IDX_DIM
    sizes = (ssm_w, q_lora, kv_lora, MLA_ROPE, dsa_w, DSA_HEAD_DIM, DSA_HEAD_DIM, idx_w, IDX_DIM, IDX_HEADS, 3 * d)
    offs = [0]
    for s in sizes:
        offs.append(offs[-1] + s)
    seg = lambda i: w_in[:, offs[i]:offs[i + 1]]
    w_u = seg(0).astype(BF16)
    pe = seg(3)
    w_mla = jnp.concatenate([seg(1), seg(2), _pad_cols(pe[:, :half], LANE), _pad_cols(pe[:, half:], LANE)],
                            axis=1).astype(BF16)
    w_dsa = jnp.concatenate([seg(4), seg(5), seg(6)], axis=1).astype(BF16)
    w_idx = jnp.concatenate([seg(7), _pad_cols(seg(8), LANE), _pad_cols(seg(9), LANE)], axis=1).astype(BF16)
    w_gate = seg(10).astype(BF16)

    xn = _rmsnorm(x2, norm_gain)
    proj = lambda w, name: _matmul([xn], [(0, w, ())], epilogue=_ep_plain, out_dtype=F32,
                                   tn=_pick(w.shape[1], (256, 128)), name=name)
    u = proj(w_u, "in_ssm")
    hm = proj(w_mla, "in_mla")
    hd = proj(w_dsa, "in_dsa")
    hi = proj(w_idx, "in_idx")
    g = proj(w_gate, "in_gates")

    y_ssm = _s5_mixer(u, bsz, seq, lam_re, lam_im, log_dt, b_re, b_im, c_re, c_im, d_skip, w_glu)
    y_mla = _mla_mixer(hm, cos, sin, bsz, seq, q_lora, kv_lora, q_lat_gain, kv_lat_gain, w_uq, w_ukv,
                       mla_q_gain, mla_k_gain)
    y_dsa = _dsa_mixer(hd, hi, bsz, seq, dsa_q_gain, dsa_k_gain)

    br0 = ssm_w
    br1 = ssm_w + MLA_HEADS * MLA_V
    tn = _pick(d, (256, 128))
    merged = _matmul([y_ssm, y_mla, y_dsa],
                     [(0, w_branch[:br0].astype(BF16), ()), (1, w_branch[br0:br1].astype(BF16), ()),
                      (2, w_branch[br1:].astype(BF16), ())],
                     epilogue=_ep_branches, out_dtype=BF16, tn=tn,
                     extras=[(g, "mn", 0), (g, "mn", d // tn), (g, "mn", 2 * (d // tn))], name="branch_merge")
    return _matmul([merged], [(0, w_out, (layer,))], epilogue=_ep_residual, out_dtype=F32, tn=tn,
                   extras=[(x2, "mn", 0)], name="mixer_out")


def _dense_ffn(x2, gain, w_gate, w_up, w_down, idx):
    d = x2.shape[1]
    dff = w_gate.shape[-1]
    xn = _rmsnorm(x2, gain)
    tn = _pick(dff, (256, 128))
    h = _matmul([xn], [(0, w_gate, (idx,)), (0, w_up, (idx,))], epilogue=_ep_swiglu, out_dtype=BF16, tn=tn,
                name="ffn_up")
    tk = None
    for cand in (dff // 2, dff // 4):
        if cand % LANE == 0 and cand * 2 <= dff:
            tk = cand
            break
    return _matmul([h], [(0, w_down, (idx,))], epilogue=_ep_residual, out_dtype=F32, tn=_pick(d, (256, 128)), tk=tk,
                   extras=[(x2, "mn", 0)], name="ffn_down")


def _router_body(x_ref, g_ref, w_ref, xn_ref, c_ref, *, n_experts):
    x = x_ref[...]
    ms = jnp.mean(x * x, axis=-1, keepdims=True)
    xn = x * lax.rsqrt(ms + EPS) * g_ref[...]
    xn_ref[...] = xn.astype(xn_ref.dtype)
    logits = jnp.dot(xn, w_ref[...], preferred_element_type=F32, precision=lax.Precision.HIGHEST)
    lane = lax.broadcasted_iota(I32, logits.shape, 1).astype(F32)
    lg = jnp.where(lane < n_experts, logits, -jnp.inf)
    m1 = jnp.max(lg, axis=1, keepdims=True)
    i1 = jnp.min(jnp.where(lg == m1, lane, float(LANE)), axis=1, keepdims=True)
    lg2 = jnp.where(lane == i1, -jnp.inf, lg)
    m2 = jnp.max(lg2, axis=1, keepdims=True)
    i2 = jnp.min(jnp.where(lg2 == m2, lane, float(LANE)), axis=1, keepdims=True)
    e2 = jnp.exp(m2 - m1)
    den = 1.0 + e2
    c_ref[...] = jnp.where(lane == i1, 1.0 / den, 0.0) + jnp.where(lane == i2, e2 / den, 0.0)


def _moe_ffn(x2, gain, w_router, w1, w3, w2, idx):
    n, d = x2.shape
    n_experts = w_router.shape[-1]
    tm = _pick(n, (256, 128, 64, 32, 16, 8))
    wr = _pad_cols(w_router[idx].astype(F32), LANE)
    xn, combine = pl.pallas_call(
        functools.partial(_router_body, n_experts=n_experts),
        grid=(n // tm,),
        in_specs=[pl.BlockSpec((tm, d), lambda i: (i, 0)), pl.BlockSpec((1, d), lambda i: (0, 0)),
                  pl.BlockSpec((d, LANE), lambda i: (0, 0))],
        out_specs=[pl.BlockSpec((tm, d), lambda i: (i, 0)), pl.BlockSpec((tm, LANE), lambda i: (i, 0))],
        out_shape=[jax.ShapeDtypeStruct((n, d), BF16), jax.ShapeDtypeStruct((n, LANE), F32)],
        compiler_params=_params(("parallel",)),
        name="moe_router",
    )(x2, gain.reshape(1, d).astype(F32), wr)
    dff = w1.shape[-1]
    out = x2
    for e in range(n_experts):
        h = _matmul([xn], [(0, w1, (idx, e)), (0, w3, (idx, e))], epilogue=_ep_swiglu, out_dtype=BF16,
                    tn=_pick(dff, (256, 128)), name="moe_up")
        out = _matmul([h], [(0, w2, (idx, e))], epilogue=functools.partial(_ep_expert_acc, expert=e), out_dtype=F32,
                      tn=_pick(d, (256, 128)), extras=[(out, "mn", 0), (combine, "col", 0)], aliases={2: 0},
                      name="moe_down")
    return out


def kernel(x, positions, mix_norm, w_in, mla_q_lat_norm, mla_kv_lat_norm, mla_w_uq, mla_w_ukv, mla_q_norm, mla_k_norm, dsa_q_norm, dsa_k_norm, ssm_lam_re, ssm_lam_im, ssm_log_dt, ssm_b_re, ssm_b_im, ssm_c_re, ssm_c_im, ssm_d, ssm_w_glu, w_branch, w_out, ffn_norm, dense_w_gate, dense_w_up, dense_w_down, moe_w_router, moe_w1, moe_w3, moe_w2):
    bsz, seq, d = x.shape
    depth = mix_norm.shape[0]
    x2 = x.reshape(bsz * seq, d)
    cos, sin = _rope_tables(positions)
    for layer in range(depth):
        x2 = _hybrid_mixer(x2, cos, sin, bsz, seq, mix_norm[layer], w_in[layer], mla_q_lat_norm[layer],
                           mla_kv_lat_norm[layer], mla_w_uq[layer], mla_w_ukv[layer], mla_q_norm[layer],
                           mla_k_norm[layer], dsa_q_norm[layer], dsa_k_norm[layer], ssm_lam_re[layer],
                           ssm_lam_im[layer], ssm_log_dt[layer], ssm_b_re[layer], ssm_b_im[layer], ssm_c_re[layer],
                           ssm_c_im[layer], ssm_d[layer], ssm_w_glu[layer], w_branch[layer], w_out, layer)
        i = layer // 2
        if layer % 2 == 0:
            x2 = _dense_ffn(x2, ffn_norm[layer], dense_w_gate, dense_w_up, dense_w_down, i)
        else:
            x2 = _moe_ffn(x2, ffn_norm[layer], moe_w_router, moe_w1, moe_w3, moe_w2, i)
    return x2.reshape(bsz, seq, d)
```

```python
import functools
import math

import jax
import jax.numpy as jnp
from jax import lax
from jax.experimental import pallas as pl
from jax.experimental.pallas import tpu as pltpu

F32 = jnp.float32
BF16 = jnp.bfloat16
I32 = jnp.int32

SSM_GROUP = 16
SSM_STATE = 64
MLA_HEADS = 12
MLA_NOPE = 128
MLA_ROPE = 64
MLA_V = 128
DSA_HEADS = 12
DSA_HEAD_DIM = 128
IDX_HEADS = 16
IDX_DIM = 64
DSA_TOPK_MAX = 256
TOP_K_EXPERTS = 2
ROPE_THETA = 10000.0
EPS = 1e-6

LANE = 128
SUBLANE = 8
VMEM_LIMIT_BYTES = 56 * 1024 * 1024
NEG_BIG = -1e30
INT_MIN = -(2 ** 31)
LOG2E = 1.4426950408889634


def _params(semantics):
    return pltpu.CompilerParams(dimension_semantics=semantics, vmem_limit_bytes=VMEM_LIMIT_BYTES)


def _sigmoid(x):
    return 1.0 / (1.0 + jnp.exp(-x))


def _pick(n, candidates):
    for c in candidates:
        if n % c == 0:
            return c
    return n


def _rmsnorm_body(x_ref, g_ref, o_ref):
    x = x_ref[...].astype(F32)
    ms = jnp.mean(x * x, axis=-1, keepdims=True)
    o_ref[...] = (x * lax.rsqrt(ms + EPS) * g_ref[...]).astype(o_ref.dtype)


def _rmsnorm(x, gain, *, width=None, col_block=0, out_dtype=BF16):
    m = x.shape[0]
    width = width or x.shape[1]
    tm = _pick(m, (512, 256, 128, 64, 32, 16, 8))
    return pl.pallas_call(
        _rmsnorm_body,
        grid=(m // tm,),
        in_specs=[pl.BlockSpec((tm, width), lambda i: (i, col_block)),
                  pl.BlockSpec((1, width), lambda i: (0, 0))],
        out_specs=pl.BlockSpec((tm, width), lambda i: (i, 0)),
        out_shape=jax.ShapeDtypeStruct((m, width), out_dtype),
        compiler_params=_params(("parallel",)),
        name="rmsnorm",
    )(x, gain.reshape(1, width).astype(F32))


def _mm_body(*refs, n_a, pairs, n_extra, nk, epilogue):
    a_refs = refs[:n_a]
    w_refs = refs[n_a:n_a + len(pairs)]
    e_refs = refs[n_a + len(pairs):n_a + len(pairs) + n_extra]
    o_ref = refs[n_a + len(pairs) + n_extra]
    acc_refs = refs[n_a + len(pairs) + n_extra + 1:]

    a_vals = [a[...].astype(BF16) for a in a_refs]
    parts = [jnp.dot(a_vals[ai], w[...].astype(BF16), preferred_element_type=F32)
             for ai, w in zip(pairs, w_refs)]

    def finish(ps):
        o_ref[...] = epilogue(ps, *[e[...] for e in e_refs]).astype(o_ref.dtype)

    if nk == 1:
        finish(parts)
        return
    k = pl.program_id(2)

    @pl.when(k == 0)
    def _():
        for acc, p in zip(acc_refs, parts):
            acc[...] = p

    @pl.when(k > 0)
    def _():
        for acc, p in zip(acc_refs, parts):
            acc[...] += p

    @pl.when(k == nk - 1)
    def _():
        finish([acc[...] for acc in acc_refs])


def _matmul(a_list, w_list, *, epilogue, out_dtype, tn, tm=None, tk=None, extras=(), aliases=None, name="matmul"):
    m = a_list[0].shape[0]
    n = w_list[0][1].shape[-1]
    kdim = a_list[0].shape[1]
    tm = tm or _pick(m, (1024, 512, 256, 128, 64, 32, 16, 8))
    nk = 1 if tk is None else kdim // tk
    grid = (m // tm, n // tn, nk)
    in_specs = []
    for a in a_list:
        ka = a.shape[1] if nk == 1 else tk
        in_specs.append(pl.BlockSpec((tm, ka), lambda i, j, k: (i, k)))
    for _, w, prefix in w_list:
        kw = w.shape[-2] if nk == 1 else tk
        in_specs.append(pl.BlockSpec((None,) * len(prefix) + (kw, tn),
                                     lambda i, j, k, prefix=prefix: tuple(prefix) + (k, j)))
    for arr, kind, arg in extras:
        if kind == "mn":
            in_specs.append(pl.BlockSpec((tm, tn), lambda i, j, k, arg=arg: (i, j + arg)))
        elif kind == "row":
            in_specs.append(pl.BlockSpec((1, tn), lambda i, j, k: (0, j)))
        else:
            in_specs.append(pl.BlockSpec((tm, arr.shape[1]), lambda i, j, k: (i, 0)))
    pairs = tuple(ai for ai, _, _ in w_list)
    body = functools.partial(_mm_body, n_a=len(a_list), pairs=pairs, n_extra=len(extras), nk=nk, epilogue=epilogue)
    scratch = [pltpu.VMEM((tm, tn), F32) for _ in pairs] if nk > 1 else []
    return pl.pallas_call(
        body,
        grid=grid,
        in_specs=in_specs,
        out_specs=pl.BlockSpec((tm, tn), lambda i, j, k: (i, j)),
        out_shape=jax.ShapeDtypeStruct((m, n), out_dtype),
        scratch_shapes=scratch,
        input_output_aliases=aliases or {},
        compiler_params=_params(("parallel", "parallel", "arbitrary")),
        name=name,
    )(*a_list, *[w for _, w, _ in w_list], *[arr for arr, _, _ in extras])


def _ep_plain(ps):
    return ps[0]


def _ep_residual(ps, res):
    return res + ps[0]


def _ep_swiglu(ps):
    g = ps[0]
    return g * _sigmoid(g) * ps[1]


def _ep_glu(ps):
    return ps[0] * _sigmoid(ps[1])


def _ep_s5_out(ps, u, d):
    y = ps[0] + ps[1] + d * u
    return jax.nn.gelu(y)


def _ep_branches(ps, g0, g1, g2):
    return _sigmoid(g0) * ps[0] + _sigmoid(g1) * ps[1] + _sigmoid(g2) * ps[2]


def _s5_scan_body(bre_ref, bim_ref, a_ref, o_re_ref, o_im_ref, xr_ref, xi_ref, car_ref, *, steps, seg_len):
    t = pl.program_id(2)
    lw = bre_ref.shape[-1]

    @pl.when(t == 0)
    def _():
        car_ref[...] = jnp.zeros_like(car_ref)

    ar = jnp.broadcast_to(a_ref[0:1, :], (SUBLANE, lw))
    ai = jnp.broadcast_to(a_ref[1:2, :], (SUBLANE, lw))

    def pass1(j, carry):
        xr, xi = carry
        off = pl.multiple_of(j * SUBLANE, SUBLANE)
        nr = ar * xr - ai * xi + bre_ref[0, pl.ds(off, SUBLANE), :]
        ni = ar * xi + ai * xr + bim_ref[0, pl.ds(off, SUBLANE), :]
        xr_ref[pl.ds(off, SUBLANE), :] = nr
        xi_ref[pl.ds(off, SUBLANE), :] = ni
        return nr, ni

    zero = jnp.zeros((SUBLANE, lw), F32)
    er, ei = lax.fori_loop(0, steps, pass1, (zero, zero))

    pr = a_ref[2:3, :]
    pi = a_ref[3:4, :]
    cr = car_ref[0:1, :]
    ci = car_ref[1:2, :]
    rows_r, rows_i = [], []
    for s in range(SUBLANE):
        rows_r.append(cr)
        rows_i.append(ci)
        nr = pr * cr - pi * ci + er[s:s + 1, :]
        ni = pr * ci + pi * cr + ei[s:s + 1, :]
        cr, ci = nr, ni
    car_ref[0:1, :] = cr
    car_ref[1:2, :] = ci
    cin_r = jnp.concatenate(rows_r, axis=0)
    cin_i = jnp.concatenate(rows_i, axis=0)

    def pass2(j, carry):
        cr_, ci_ = carry
        off = pl.multiple_of(j * SUBLANE, SUBLANE)
        nr = ar * cr_ - ai * ci_
        ni = ar * ci_ + ai * cr_
        o_re_ref[0, pl.ds(off, SUBLANE), :] = (xr_ref[pl.ds(off, SUBLANE), :] + nr).astype(o_re_ref.dtype)
        o_im_ref[0, pl.ds(off, SUBLANE), :] = (xi_ref[pl.ds(off, SUBLANE), :] + ni).astype(o_im_ref.dtype)
        return nr, ni

    lax.fori_loop(0, steps, pass2, (cin_r, cin_i))


def _s5_scan(bu_re, bu_im, a_tab, *, chunk):
    b, s, w = bu_re.shape
    lw = _pick(w, (512, 256, 128))
    steps = chunk // SUBLANE
    body = functools.partial(_s5_scan_body, steps=steps, seg_len=steps)
    blk = pl.BlockSpec((1, chunk, lw), lambda bi, li, ti: (bi, ti, li))
    return pl.pallas_call(
        body,
        grid=(b, w // lw, s // chunk),
        in_specs=[blk, blk, pl.BlockSpec((4, lw), lambda bi, li, ti: (0, li))],
        out_specs=[blk, blk],
        out_shape=[jax.ShapeDtypeStruct((b, s, w), BF16)] * 2,
        scratch_shapes=[pltpu.VMEM((chunk, lw), F32), pltpu.VMEM((chunk, lw), F32), pltpu.VMEM((SUBLANE, lw), F32)],
        compiler_params=_params(("parallel", "parallel", "arbitrary")),
        name="s5_scan",
    )(bu_re, bu_im, a_tab)


def _s5_chunk(s):
    return _pick(s, (512, 256, 128, 64))


def _to_chunk_order(t, chunk):
    b, s, w = t.shape
    return t.reshape(b, s // chunk, SUBLANE, chunk // SUBLANE, w).transpose(0, 1, 3, 2, 4).reshape(b, s, w)


def _from_chunk_order(t, chunk):
    b, s, w = t.shape
    return t.reshape(b, s // chunk, chunk // SUBLANE, SUBLANE, w).transpose(0, 1, 3, 2, 4).reshape(b, s, w)


def _s5_discretise(lam_re, lam_im, log_dt, b_re, b_im, c_re, c_im, steps):
    g, p = lam_re.shape
    h = b_re.shape[-1]
    lr, li = lam_re.astype(F32), lam_im.astype(F32)
    dt = jnp.exp(log_dt.astype(F32))[:, None]
    mag = jnp.exp(lr * dt)
    ab_re, ab_im = mag * jnp.cos(li * dt), mag * jnp.sin(li * dt)
    den = lr * lr + li * li
    nr = ab_re - 1.0
    zr = (nr * lr + ab_im * li) / den
    zi = (ab_im * lr - nr * li) / den
    bb_re = zr[..., None] * b_re - zi[..., None] * b_im
    bb_im = zr[..., None] * b_im + zi[..., None] * b_re
    eye = jnp.eye(g, dtype=F32)
    bd_in_re = jnp.einsum("gph,gk->ghkp", bb_re, eye).reshape(g * h, g * p)
    bd_in_im = jnp.einsum("gph,gk->ghkp", bb_im, eye).reshape(g * h, g * p)
    bd_out_re = jnp.einsum("ghp,gk->gpkh", c_re.astype(F32), eye).reshape(g * p, g * h)
    bd_out_im = -jnp.einsum("ghp,gk->gpkh", c_im.astype(F32), eye).reshape(g * p, g * h)
    pr, pi = ab_re, ab_im
    for _ in range(int(round(math.log2(steps)))):
        pr, pi = pr * pr - pi * pi, 2.0 * pr * pi
    a_tab = jnp.stack([ab_re.reshape(-1), ab_im.reshape(-1), pr.reshape(-1), pi.reshape(-1)])
    return (bd_in_re.astype(BF16), bd_in_im.astype(BF16), bd_out_re.astype(BF16), bd_out_im.astype(BF16), a_tab)


def _s5_mixer(u, bsz, seq, lam_re, lam_im, log_dt, b_re, b_im, c_re, c_im, d_skip, w_glu):
    n, w = u.shape
    chunk = _s5_chunk(seq)
    steps = chunk // SUBLANE
    bin_re, bin_im, bout_re, bout_im, a_tab = _s5_discretise(lam_re, lam_im, log_dt, b_re, b_im, c_re, c_im, steps)
    wst = bin_re.shape[1]
    u_p = _to_chunk_order(u.reshape(bsz, seq, w), chunk).reshape(n, w)
    tn = _pick(wst, (512, 256, 128))
    bu_re = _matmul([u_p], [(0, bin_re, ())], epilogue=_ep_plain, out_dtype=F32, tn=tn, name="s5_bu_re")
    bu_im = _matmul([u_p], [(0, bin_im, ())], epilogue=_ep_plain, out_dtype=F32, tn=tn, name="s5_bu_im")
    x_re, x_im = _s5_scan(bu_re.reshape(bsz, seq, wst), bu_im.reshape(bsz, seq, wst), a_tab, chunk=chunk)
    tnw = _pick(w, (512, 256, 128))
    y = _matmul([x_re.reshape(n, wst), x_im.reshape(n, wst)], [(0, bout_re, ()), (1, bout_im, ())],
                epilogue=_ep_s5_out, out_dtype=BF16, tn=tnw,
                extras=[(u_p, "mn", 0), (d_skip.reshape(1, w).astype(F32), "row", 0)], name="s5_out")
    w_a = w_glu[:, :w].astype(BF16)
    w_g = w_glu[:, w:].astype(BF16)
    out = _matmul([y], [(0, w_a, ()), (0, w_g, ())], epilogue=_ep_glu, out_dtype=BF16, tn=tnw, name="s5_glu")
    return _from_chunk_order(out.reshape(bsz, seq, w), chunk).reshape(n, w)


def _rope_table_body(pos_ref, inv_ref, cos_ref, sin_ref):
    ang = pos_ref[...].astype(F32) * inv_ref[...]
    cos_ref[...] = jnp.cos(ang)
    sin_ref[...] = jnp.sin(ang)


def _rope_tables(positions):
    n = positions.size
    half = MLA_ROPE // 2
    inv = ROPE_THETA ** (-jnp.arange(half, dtype=F32) / half)
    inv = jnp.zeros((1, LANE), F32).at[0, :half].set(inv)
    tm = _pick(n, (512, 256, 128, 64, 32, 16, 8))
    return pl.pallas_call(
        _rope_table_body,
        grid=(n // tm,),
        in_specs=[pl.BlockSpec((tm, 1), lambda i: (i, 0)), pl.BlockSpec((1, LANE), lambda i: (0, 0))],
        out_specs=[pl.BlockSpec((tm, LANE), lambda i: (i, 0))] * 2,
        out_shape=[jax.ShapeDtypeStruct((n, LANE), F32)] * 2,
        compiler_params=_params(("parallel",)),
        name="rope_table",
    )(positions.reshape(n, 1).astype(I32), inv)


def _head_norm_rope(t0, t1, t2, g0, g1, g2, cos, sin, true_dim):
    ss = (jnp.sum(t0 * t0, axis=-1, keepdims=True) + jnp.sum(t1 * t1, axis=-1, keepdims=True)
          + jnp.sum(t2 * t2, axis=-1, keepdims=True))
    inv = lax.rsqrt(ss * (1.0 / true_dim) + EPS)
    n0 = t0 * inv * g0
    n1 = t1 * inv * g1
    n2 = t2 * inv * g2
    o1 = n1 * cos - n2 * sin
    o2 = n1 * sin + n2 * cos
    half = MLA_ROPE // 2
    lane = lax.broadcasted_iota(I32, o1.shape, 1)
    o1 = jnp.where(lane < half, o1, 0.0)
    o2 = jnp.where(lane < half, o2, 0.0)
    return jnp.concatenate([n0, o1 + pltpu.roll(o2, half, 1)], axis=-1)


def _mla_q_prep_body(q_ref, g_ref, cos_ref, sin_ref, o_ref):
    cos, sin = cos_ref[...], sin_ref[...]
    hw = 3 * LANE
    g0, g1, g2 = g_ref[:, 0:LANE], g_ref[:, LANE:2 * LANE], g_ref[:, 2 * LANE:hw]
    for h in range(MLA_HEADS):
        t0 = q_ref[:, h * hw:h * hw + LANE]
        t1 = q_ref[:, h * hw + LANE:h * hw + 2 * LANE]
        t2 = q_ref[:, h * hw + 2 * LANE:(h + 1) * hw]
        qh = _head_norm_rope(t0, t1, t2, g0, g1, g2, cos, sin, MLA_NOPE + MLA_ROPE)
        o_ref[0, h] = (qh * ((MLA_NOPE + MLA_ROPE) ** -0.5 * LOG2E)).astype(o_ref.dtype)


def _mla_kv_prep_body(kv_ref, pe_ref, g_ref, cos_ref, sin_ref, k_ref, v_ref):
    cos, sin = cos_ref[...], sin_ref[...]
    hw = 3 * LANE
    g0, g1, g2 = g_ref[:, 0:LANE], g_ref[:, LANE:2 * LANE], g_ref[:, 2 * LANE:hw]
    t1 = pe_ref[:, 0:LANE]
    t2 = pe_ref[:, LANE:2 * LANE]
    kvw = MLA_NOPE + MLA_V
    for h in range(MLA_HEADS):
        t0 = kv_ref[:, h * kvw:h * kvw + MLA_NOPE]
        k_ref[0, h] = _head_norm_rope(t0, t1, t2, g0, g1, g2, cos, sin, MLA_NOPE + MLA_ROPE).astype(k_ref.dtype)
        v_ref[0, h] = kv_ref[:, h * kvw + MLA_NOPE:(h + 1) * kvw].astype(v_ref.dtype)


def _pad_rope_gain(gain):
    half = MLA_ROPE // 2
    out = jnp.zeros((3 * LANE,), F32)
    out = out.at[:MLA_NOPE].set(gain[:MLA_NOPE].astype(F32))
    out = out.at[LANE:LANE + half].set(gain[MLA_NOPE:MLA_NOPE + half].astype(F32))
    out = out.at[2 * LANE:2 * LANE + half].set(gain[MLA_NOPE + half:].astype(F32))
    return out.reshape(1, 3 * LANE)


def _flash_body(q_ref, k_ref, v_ref, o_ref, *, tq):
    qi = pl.program_id(2)
    dv = v_ref.shape[-1]
    half = tq // 2
    qs = (q_ref[0, 0, 0:half, :], q_ref[0, 0, half:tq, :])

    def step(j, carry, masked):
        off = pl.multiple_of(j * tq, tq)
        k = k_ref[0, 0, pl.ds(off, tq), :]
        v = v_ref[0, 0, pl.ds(off, tq), :]
        out = []
        for r, (m, l, acc) in enumerate(carry):
            s = lax.dot_general(qs[r], k, (((1,), (1,)), ((), ())), preferred_element_type=F32)
            if masked:
                row = lax.broadcasted_iota(I32, (half, tq), 0) + r * half
                col = lax.broadcasted_iota(I32, (half, tq), 1)
                s = jnp.where(col <= row, s, NEG_BIG)
            m_new = jnp.maximum(m, jnp.max(s, axis=1, keepdims=True))
            alpha = jnp.exp2(m - m_new)
            p = jnp.exp2(s - m_new)
            l = alpha * l + jnp.sum(p, axis=1, keepdims=True)
            acc = alpha * acc + jnp.dot(p.astype(BF16), v, preferred_element_type=F32)
            out.append((m_new, l, acc))
        return tuple(out)

    one = (jnp.full((half, 1), NEG_BIG, F32), jnp.zeros((half, 1), F32), jnp.zeros((half, dv), F32))
    carry = lax.fori_loop(0, qi, lambda j, c: step(j, c, False), (one, one))
    carry = step(qi, carry, True)
    for r, (m, l, acc) in enumerate(carry):
        o_ref[0, r * half:(r + 1) * half, :] = (acc / l).astype(o_ref.dtype)


def _flash_attention(q, k, v):
    b, h, s, dk = q.shape
    dv = v.shape[-1]
    tq = _pick(s, (512, 256, 128))
    return pl.pallas_call(
        functools.partial(_flash_body, tq=tq),
        grid=(b, h, s // tq),
        in_specs=[pl.BlockSpec((1, 1, tq, dk), lambda bi, hi, qi: (bi, hi, qi, 0)),
                  pl.BlockSpec((1, 1, s, dk), lambda bi, hi, qi: (bi, hi, 0, 0)),
                  pl.BlockSpec((1, 1, s, dv), lambda bi, hi, qi: (bi, hi, 0, 0))],
        out_specs=pl.BlockSpec((1, tq, dv), lambda bi, hi, qi: (bi, qi, hi)),
        out_shape=jax.ShapeDtypeStruct((b, s, h * dv), BF16),
        compiler_params=_params(("parallel", "parallel", "arbitrary")),
        name="mla_flash",
    )(q, k, v)


def _mla_mixer(hm, cos, sin, bsz, seq, q_lora, kv_lora, q_lat_gain, kv_lat_gain, w_uq, w_ukv, q_gain, k_gain):
    n = hm.shape[0]
    half = MLA_ROPE // 2
    qk = MLA_NOPE + MLA_ROPE
    cqn = _rmsnorm(hm, q_lat_gain, width=q_lora, col_block=0)
    ckvn = _rmsnorm(hm, kv_lat_gain, width=kv_lora, col_block=q_lora // kv_lora)
    wq = w_uq.reshape(q_lora, MLA_HEADS, qk)
    wq_p = jnp.zeros((q_lora, MLA_HEADS, 3 * LANE), BF16)
    wq_p = wq_p.at[:, :, :MLA_NOPE].set(wq[:, :, :MLA_NOPE].astype(BF16))
    wq_p = wq_p.at[:, :, LANE:LANE + half].set(wq[:, :, MLA_NOPE:MLA_NOPE + half].astype(BF16))
    wq_p = wq_p.at[:, :, 2 * LANE:2 * LANE + half].set(wq[:, :, MLA_NOPE + half:].astype(BF16))
    wq_p = wq_p.reshape(q_lora, MLA_HEADS * 3 * LANE)
    q_raw = _matmul([cqn], [(0, wq_p, ())], epilogue=_ep_plain, out_dtype=F32, tn=3 * LANE, name="mla_uq")
    kv_raw = _matmul([ckvn], [(0, w_ukv.astype(BF16), ())], epilogue=_ep_plain, out_dtype=F32,
                     tn=MLA_NOPE + MLA_V, name="mla_ukv")

    tm = _pick(seq, (256, 128, 64, 32, 16))
    nblk = seq // tm
    qg = _pad_rope_gain(q_gain)
    kg = _pad_rope_gain(k_gain)
    row = lambda bi, si: (bi * nblk + si, 0)
    q = pl.pallas_call(
        _mla_q_prep_body,
        grid=(bsz, nblk),
        in_specs=[pl.BlockSpec((tm, MLA_HEADS * 3 * LANE), row), pl.BlockSpec((1, 3 * LANE), lambda bi, si: (0, 0)),
                  pl.BlockSpec((tm, LANE), row), pl.BlockSpec((tm, LANE), row)],
        out_specs=pl.BlockSpec((1, MLA_HEADS, tm, 2 * LANE), lambda bi, si: (bi, 0, si, 0)),
        out_shape=jax.ShapeDtypeStruct((bsz, MLA_HEADS, seq, 2 * LANE), BF16),
        compiler_params=_params(("parallel", "parallel")),
        name="mla_q_prep",
    )(q_raw, qg, cos, sin)
    pe_block = (q_lora + kv_lora) // (2 * LANE)
    k, v = pl.pallas_call(
        _mla_kv_prep_body,
        grid=(bsz, nblk),
        in_specs=[pl.BlockSpec((tm, MLA_HEADS * (MLA_NOPE + MLA_V)), row),
                  pl.BlockSpec((tm, 2 * LANE), lambda bi, si: (bi * nblk + si, pe_block)),
                  pl.BlockSpec((1, 3 * LANE), lambda bi, si: (0, 0)),
                  pl.BlockSpec((tm, LANE), row), pl.BlockSpec((tm, LANE), row)],
        out_specs=[pl.BlockSpec((1, MLA_HEADS, tm, 2 * LANE), lambda bi, si: (bi, 0, si, 0)),
                   pl.BlockSpec((1, MLA_HEADS, tm, MLA_V), lambda bi, si: (bi, 0, si, 0))],
        out_shape=[jax.ShapeDtypeStruct((bsz, MLA_HEADS, seq, 2 * LANE), BF16),
                   jax.ShapeDtypeStruct((bsz, MLA_HEADS, seq, MLA_V), BF16)],
        compiler_params=_params(("parallel", "parallel")),
        name="mla_kv_prep",
    )(kv_raw, hm, kg, cos, sin)
    return _flash_attention(q, k, v).reshape(n, MLA_HEADS * MLA_V)


def _dsa_prep_body(hd_q_ref, hd_k_ref, hd_v_ref, hi_q_ref, hi_k_ref, hi_w_ref, qg_ref, kg_ref,
                   q_ref, k_ref, v_ref, qi_ref, klo_ref, khi_ref, w_ref):
    def norm(t, g):
        ms = jnp.mean(t * t, axis=-1, keepdims=True)
        return t * lax.rsqrt(ms + EPS) * g

    qg = qg_ref[...]
    for h in range(DSA_HEADS):
        qh = norm(hd_q_ref[:, h * DSA_HEAD_DIM:(h + 1) * DSA_HEAD_DIM], qg)
        q_ref[0, h] = (qh * (DSA_HEAD_DIM ** -0.5 * LOG2E)).astype(q_ref.dtype)
    k_ref[0] = norm(hd_k_ref[...], kg_ref[...]).astype(k_ref.dtype)
    v_ref[0] = hd_v_ref[...].astype(v_ref.dtype)
    qi_ref[0] = hi_q_ref[...].astype(qi_ref.dtype)
    ki = hi_k_ref[...]
    lane = lax.broadcasted_iota(I32, ki.shape, 1)
    ki = jnp.where(lane < IDX_DIM, ki, 0.0)
    klo_ref[0] = ki.astype(klo_ref.dtype)
    khi_ref[0] = pltpu.roll(ki, IDX_DIM, 1).astype(khi_ref.dtype)
    w_ref[0] = hi_w_ref[...] * ((IDX_HEADS ** -0.5) * (IDX_DIM ** -0.5))


def _float_key(x):
    bits = pltpu.bitcast(x, I32)
    return jnp.where(bits < 0, bits ^ jnp.int32(0x7FFFFFFF), bits)


def _dsa_body(q_ref, k_ref, v_ref, qi_ref, klo_ref, khi_ref, w_ref, o_ref, keys_ref, m_ref, l_ref, acc_ref, *,
              tq, tk, n_sel):
    qb = pl.program_id(1)
    n_kv = (qb * tq + tq + tk - 1) // tk
    row = lax.broadcasted_iota(I32, (tq, tk), 0) + qb * tq
    col0 = lax.broadcasted_iota(I32, (tq, tk), 1)

    w = w_ref[0]

    def score_tile(j, _):
        off = pl.multiple_of(j * tk, tk)
        klo = klo_ref[0, pl.ds(off, tk), :]
        khi = khi_ref[0, pl.ds(off, tk), :]
        acc = jnp.zeros((tq, tk), F32)
        for hp in range(IDX_HEADS // 2):
            q2 = qi_ref[0, :, hp * LANE:(hp + 1) * LANE]
            for half, kk in ((0, klo), (1, khi)):
                h = 2 * hp + half
                logit = lax.dot_general(q2, kk, (((1,), (1,)), ((), ())), preferred_element_type=F32)
                acc = acc + w[:, h:h + 1] * jnp.maximum(logit, 0.0)
        key = _float_key(acc + 0.0)
        keys_ref[:, pl.ds(off, tk)] = jnp.where(col0 + off <= row, key, INT_MIN)
        return 0

    lax.fori_loop(0, n_kv, score_tile, 0)

    rc = min(tq, 64)

    def row_group(g, carry):
        r0 = pl.multiple_of(g * rc, rc)

        def bit_step(b, lo):
            cand = lo + lax.shift_left(jnp.int32(1), 31 - b)

            def count_tile(j, part):
                off = pl.multiple_of(j * tk, tk)
                for c in range(tk // LANE):
                    kc = keys_ref[pl.ds(r0, rc), pl.ds(off + c * LANE, LANE)]
                    part = part + jnp.where(kc >= cand, 1.0, 0.0)
                return part

            part = lax.fori_loop(0, n_kv, count_tile, jnp.zeros((rc, LANE), F32))
            cnt = jnp.sum(part, axis=1, keepdims=True)
            return jnp.where(cnt >= n_sel, cand, lo)

        thr = lax.fori_loop(0, 32, bit_step, jnp.full((rc, LANE), INT_MIN, I32))
        thr = jnp.maximum(thr, INT_MIN + 1)

        def bias_tile(j, c2):
            off = pl.multiple_of(j * tk, tk)
            for c in range(tk // LANE):
                kc = keys_ref[pl.ds(r0, rc), pl.ds(off + c * LANE, LANE)]
                bias = jnp.where(kc >= thr, 0.0, NEG_BIG)
                keys_ref[pl.ds(r0, rc), pl.ds(off + c * LANE, LANE)] = pltpu.bitcast(bias, I32)
            return c2

        lax.fori_loop(0, n_kv, bias_tile, 0)
        return carry

    lax.fori_loop(0, tq // rc, row_group, 0)

    m_ref[...] = jnp.full(m_ref.shape, NEG_BIG, F32)
    l_ref[...] = jnp.zeros(l_ref.shape, F32)
    acc_ref[...] = jnp.zeros(acc_ref.shape, F32)

    def attend(j, carry):
        off = pl.multiple_of(j * tk, tk)
        k = k_ref[0, pl.ds(off, tk), :]
        v = v_ref[0, pl.ds(off, tk), :]
        bias = pltpu.bitcast(keys_ref[:, pl.ds(off, tk)], F32)
        for h in range(DSA_HEADS):
            s = lax.dot_general(q_ref[0, h], k, (((1,), (1,)), ((), ())), preferred_element_type=F32) + bias
            m_old = m_ref[h]
            m_new = jnp.maximum(m_old, jnp.max(s, axis=1, keepdims=True))
            alpha = jnp.exp2(m_old - m_new)
            p = jnp.exp2(s - m_new)
            l_ref[h] = alpha * l_ref[h] + jnp.sum(p, axis=1, keepdims=True)
            acc_ref[h] = alpha * acc_ref[h] + jnp.dot(p.astype(BF16), v, preferred_element_type=F32)
            m_ref[h] = m_new
        return carry

    lax.fori_loop(0, n_kv, attend, 0)
    for h in range(DSA_HEADS):
        o_ref[0, :, h * DSA_HEAD_DIM:(h + 1) * DSA_HEAD_DIM] = (acc_ref[h] / l_ref[h]).astype(o_ref.dtype)


def _dsa_mixer(hd, hi, bsz, seq, q_gain, k_gain):
    n = hd.shape[0]
    qw = DSA_HEADS * DSA_HEAD_DIM
    iw = IDX_HEADS * IDX_DIM
    tm = _pick(seq, (256, 128, 64, 32, 16))
    nblk = seq // tm

    def cols(width, block):
        return pl.BlockSpec((tm, width), lambda bi, si: (bi * nblk + si, block))

    per_tok = lambda width: pl.BlockSpec((1, tm, width), lambda bi, si: (bi, si, 0))
    q, k, v, qi, klo, khi, w = pl.pallas_call(
        _dsa_prep_body,
        grid=(bsz, nblk),
        in_specs=[cols(qw, 0), cols(LANE, qw // LANE), cols(LANE, qw // LANE + 1),
                  cols(iw, 0), cols(LANE, iw // LANE), cols(LANE, iw // LANE + 1),
                  pl.BlockSpec((1, DSA_HEAD_DIM), lambda bi, si: (0, 0)),
                  pl.BlockSpec((1, DSA_HEAD_DIM), lambda bi, si: (0, 0))],
        out_specs=[pl.BlockSpec((1, DSA_HEADS, tm, DSA_HEAD_DIM), lambda bi, si: (bi, 0, si, 0)),
                   per_tok(DSA_HEAD_DIM), per_tok(DSA_HEAD_DIM), per_tok(iw), per_tok(LANE), per_tok(LANE),
                   per_tok(LANE)],
        out_shape=[jax.ShapeDtypeStruct((bsz, DSA_HEADS, seq, DSA_HEAD_DIM), BF16),
                   jax.ShapeDtypeStruct((bsz, seq, DSA_HEAD_DIM), BF16),
                   jax.ShapeDtypeStruct((bsz, seq, DSA_HEAD_DIM), BF16),
                   jax.ShapeDtypeStruct((bsz, seq, iw), BF16),
                   jax.ShapeDtypeStruct((bsz, seq, LANE), BF16),
                   jax.ShapeDtypeStruct((bsz, seq, LANE), BF16),
                   jax.ShapeDtypeStruct((bsz, seq, LANE), F32)],
        compiler_params=_params(("parallel", "parallel")),
        name="dsa_prep",
    )(hd, hd, hd, hi, hi, hi, q_gain.reshape(1, -1).astype(F32), k_gain.reshape(1, -1).astype(F32))

    tq = _pick(seq, (256, 128))
    tk = _pick(seq, (512, 256, 128))
    n_sel = min(DSA_TOPK_MAX, seq // 4)
    whole = lambda width: pl.BlockSpec((1, seq, width), lambda bi, qb: (bi, 0, 0))
    out = pl.pallas_call(
        functools.partial(_dsa_body, tq=tq, tk=tk, n_sel=n_sel),
        grid=(bsz, seq // tq),
        in_specs=[pl.BlockSpec((1, DSA_HEADS, tq, DSA_HEAD_DIM), lambda bi, qb: (bi, 0, qb, 0)),
                  whole(DSA_HEAD_DIM), whole(DSA_HEAD_DIM),
                  pl.BlockSpec((1, tq, iw), lambda bi, qb: (bi, qb, 0)),
                  whole(LANE), whole(LANE),
                  pl.BlockSpec((1, tq, LANE), lambda bi, qb: (bi, qb, 0))],
        out_specs=pl.BlockSpec((1, tq, qw), lambda bi, qb: (bi, qb, 0)),
        out_shape=jax.ShapeDtypeStruct((bsz, seq, qw), BF16),
        scratch_shapes=[pltpu.VMEM((tq, seq), I32), pltpu.VMEM((DSA_HEADS, tq, 1), F32),
                        pltpu.VMEM((DSA_HEADS, tq, 1), F32), pltpu.VMEM((DSA_HEADS, tq, DSA_HEAD_DIM), F32)],
        compiler_params=_params(("parallel", "arbitrary")),
        name="dsa_attention",
    )(q, k, v, qi, klo, khi, w)
    return out.reshape(n, qw)


def _pad_cols(w, width):
    return jnp.pad(w, ((0, 0), (0, width - w.shape[1])))


def _hybrid_mixer(x2, cos, sin, bsz, seq, norm_gain, w_in, q_lat_gain, kv_lat_gain, w_uq, w_ukv, mla_q_gain,
                  mla_k_gain, dsa_q_gain, dsa_k_gain, lam_re, lam_im, log_dt, b_re, b_im, c_re, c_im, d_skip, w_glu,
                  w_branch, w_out, layer):
    n, d = x2.shape
    ssm_w = d_skip.shape[-1]
    q_lora = q_lat_gain.shape[-1]
    kv_lora = kv_lat_gain.shape[-1]
    half = MLA_ROPE // 2
    dsa_w = DSA_HEADS * DSA_HEAD_DIM
    idx_w = IDX_HEADS * IDX_DIM
    sizes = (ssm_w, q_lora, kv_lora, MLA_ROPE, dsa_w, DSA_HEAD_DIM, DSA_HEAD_DIM, idx_w, IDX_DIM, IDX_HEADS, 3 * d)
    offs = [0]
    for s in sizes:
        offs.append(offs[-1] + s)
    seg = lambda i: w_in[:, offs[i]:offs[i + 1]]
    w_u = seg(0).astype(BF16)
    pe = seg(3)
    w_mla = jnp.concatenate([seg(1), seg(2), _pad_cols(pe[:, :half], LANE), _pad_cols(pe[:, half:], LANE)],
                            axis=1).astype(BF16)
    w_dsa = jnp.concatenate([seg(4), seg(5), seg(6)], axis=1).astype(BF16)
    w_idx = jnp.concatenate([seg(7), _pad_cols(seg(8), LANE), _pad_cols(seg(9), LANE)], axis=1).astype(BF16)
    w_gate = seg(10).astype(BF16)

    xn = _rmsnorm(x2, norm_gain)
    proj = lambda w, name: _matmul([xn], [(0, w, ())], epilogue=_ep_plain, out_dtype=F32,
                                   tn=_pick(w.shape[1], (256, 128)), name=name)
    u = proj(w_u, "in_ssm")
    hm = proj(w_mla, "in_mla")
    hd = proj(w_dsa, "in_dsa")
    hi = proj(w_idx, "in_idx")
    g = proj(w_gate, "in_gates")

    y_ssm = _s5_mixer(u, bsz, seq, lam_re, lam_im, log_dt, b_re, b_im, c_re, c_im, d_skip, w_glu)
    y_mla = _mla_mixer(hm, cos, sin, bsz, seq, q_lora, kv_lora, q_lat_gain, kv_lat_gain, w_uq, w_ukv,
                       mla_q_gain, mla_k_gain)
    y_dsa = _dsa_mixer(hd, hi, bsz, seq, dsa_q_gain, dsa_k_gain)

    br0 = ssm_w
    br1 = ssm_w + MLA_HEADS * MLA_V
    tn = _pick(d, (256, 128))
    merged = _matmul([y_ssm, y_mla, y_dsa],
                     [(0, w_branch[:br0].astype(BF16), ()), (1, w_branch[br0:br1].astype(BF16), ()),
                      (2, w_branch[br1:].astype(BF16), ())],
                     epilogue=_ep_branches, out_dtype=BF16, tn=tn,
                     extras=[(g, "mn", 0), (g, "mn", d // tn), (g, "mn", 2 * (d // tn))], name="branch_merge")
    return _matmul([merged], [(0, w_out, (layer,))], epilogue=_ep_residual, out_dtype=F32, tn=tn,
                   extras=[(x2, "mn", 0)], name="mixer_out")


def _dense_ffn(x2, gain, w_gate, w_up, w_down, idx):
    d = x2.shape[1]
    dff = w_gate.shape[-1]
    xn = _rmsnorm(x2, gain)
    tn = _pick(dff, (256, 128))
    h = _matmul([xn], [(0, w_gate, (idx,)), (0, w_up, (idx,))], epilogue=_ep_swiglu, out_dtype=BF16, tn=tn,
                name="ffn_up")
    tk = None
    for cand in (dff // 2, dff // 4):
        if cand % LANE == 0 and cand * 2 <= dff:
            tk = cand
            break
    return _matmul([h], [(0, w_down, (idx,))], epilogue=_ep_residual, out_dtype=F32, tn=_pick(d, (256, 128)), tk=tk,
                   extras=[(x2, "mn", 0)], name="ffn_down")


def _router_body(x_ref, g_ref, w_ref, xn_ref, meta_ref, cnt_ref, run_ref, *, n_experts):
    i = pl.program_id(0)

    @pl.when(i == 0)
    def _():
        run_ref[...] = jnp.zeros_like(run_ref)

    x = x_ref[...]
    tm = x.shape[0]
    ms = jnp.mean(x * x, axis=-1, keepdims=True)
    xn = x * lax.rsqrt(ms + EPS) * g_ref[...]
    xn_ref[...] = xn.astype(xn_ref.dtype)
    logits = jnp.dot(xn, w_ref[...], preferred_element_type=F32, precision=lax.Precision.HIGHEST)
    lane = lax.broadcasted_iota(I32, logits.shape, 1).astype(F32)
    lg = jnp.where(lane < n_experts, logits, -jnp.inf)
    m1 = jnp.max(lg, axis=1, keepdims=True)
    i1 = jnp.min(jnp.where(lg == m1, lane, float(LANE)), axis=1, keepdims=True)
    lg2 = jnp.where(lane == i1, -jnp.inf, lg)
    m2 = jnp.max(lg2, axis=1, keepdims=True)
    i2 = jnp.min(jnp.where(lg2 == m2, lane, float(LANE)), axis=1, keepdims=True)
    e2 = jnp.exp(m2 - m1)
    den = 1.0 + e2
    oh1 = jnp.where(lane == i1, 1.0, 0.0)
    oh2 = jnp.where(lane == i2, 1.0, 0.0)
    oh = oh1 + oh2
    r = lax.broadcasted_iota(I32, (tm, tm), 0)
    c = lax.broadcasted_iota(I32, (tm, tm), 1)
    tri = jnp.where(c < r, 1.0, 0.0).astype(BF16)
    base = run_ref[0:1, :]
    rank = jnp.dot(tri, oh.astype(BF16), preferred_element_type=F32) + base
    r1 = jnp.sum(oh1 * rank, axis=1, keepdims=True)
    r2 = jnp.sum(oh2 * rank, axis=1, keepdims=True)
    total = base + jnp.sum(oh, axis=0, keepdims=True)
    run_ref[0:1, :] = total
    cnt_ref[...] = jnp.broadcast_to(total, cnt_ref.shape)
    meta = jnp.zeros_like(logits)
    for k, val in enumerate((i1, i2, r1, r2, 1.0 / den, e2 / den)):
        meta = jnp.where(lane == float(k), val, meta)
    meta_ref[...] = meta


def _gather_rows_body(src_ref, x_hbm, o_ref, sem, *, rows):
    base = pl.program_id(0) * rows

    def issue(r, carry):
        pltpu.make_async_copy(x_hbm.at[src_ref[base + r]], o_ref.at[r], sem).start()
        return carry

    lax.fori_loop(0, rows, issue, 0)
    pltpu.make_async_copy(x_hbm.at[pl.ds(0, rows)], o_ref, sem).wait()


def _gather_rows(x, src, *, rows):
    n_out = src.shape[0]
    n, d = x.shape
    x3 = x.reshape(n, d // LANE, LANE)
    out = pl.pallas_call(
        functools.partial(_gather_rows_body, rows=rows),
        grid_spec=pltpu.PrefetchScalarGridSpec(
            num_scalar_prefetch=1,
            grid=(n_out // rows,),
            in_specs=[pl.BlockSpec(memory_space=pl.ANY)],
            out_specs=pl.BlockSpec((rows, d // LANE, LANE), lambda i, src_ref: (i, 0, 0)),
            scratch_shapes=[pltpu.SemaphoreType.DMA(())],
        ),
        out_shape=jax.ShapeDtypeStruct((n_out, d // LANE, LANE), x.dtype),
        compiler_params=_params(("arbitrary",)),
        name="moe_gather",
    )(src, x3)
    return out.reshape(n_out, d)


def _experts_body(tile_ref, wexp_ref, lo_ref, hi_ref, first_ref, xs_ref, w1_ref, w3_ref, w2_ref, o_ref, *, tm):
    v = pl.program_id(0)
    f = pl.program_id(1)

    @pl.when((f == 0) & (first_ref[v] == 1))
    def _():
        o_ref[...] = jnp.zeros_like(o_ref)

    lo = lo_ref[v]
    hi = hi_ref[v]

    @pl.when(hi > lo)
    def _():
        xs = xs_ref[...]
        g = jnp.dot(xs, w1_ref[...], preferred_element_type=F32)
        u = jnp.dot(xs, w3_ref[...], preferred_element_type=F32)
        row = tile_ref[v] * tm + lax.broadcasted_iota(I32, (tm, 1), 0)
        h = jnp.where((row >= lo) & (row < hi), g * _sigmoid(g) * u, 0.0)
        o_ref[...] += jnp.dot(h.astype(BF16), w2_ref[...], preferred_element_type=F32)


def _combine_body(d1_ref, d2_ref, y_hbm, x_ref, w1_ref, w2_ref, o_ref, b1_ref, b2_ref, sem, *, rows):
    base = pl.program_id(0) * rows

    def issue(r, carry):
        pltpu.make_async_copy(y_hbm.at[d1_ref[base + r]], b1_ref.at[r], sem.at[0]).start()
        pltpu.make_async_copy(y_hbm.at[d2_ref[base + r]], b2_ref.at[r], sem.at[1]).start()
        return carry

    lax.fori_loop(0, rows, issue, 0)
    pltpu.make_async_copy(y_hbm.at[pl.ds(0, rows)], b1_ref, sem.at[0]).wait()
    pltpu.make_async_copy(y_hbm.at[pl.ds(0, rows)], b2_ref, sem.at[1]).wait()
    o_ref[...] = x_ref[...] + w1_ref[...] * b1_ref[...] + w2_ref[...] * b2_ref[...]


def _moe_ffn(x2, gain, w_router, w1, w3, w2, idx):
    n, d = x2.shape
    n_experts = w_router.shape[-1]
    dff = w1.shape[-1]
    tr = _pick(n, (256, 128, 64, 32, 16, 8))
    wr = _pad_cols(w_router[idx].astype(F32), LANE)
    xn, meta, cnt = pl.pallas_call(
        functools.partial(_router_body, n_experts=n_experts),
        grid=(n // tr,),
        in_specs=[pl.BlockSpec((tr, d), lambda i: (i, 0)), pl.BlockSpec((1, d), lambda i: (0, 0)),
                  pl.BlockSpec((d, LANE), lambda i: (0, 0))],
        out_specs=[pl.BlockSpec((tr, d), lambda i: (i, 0)), pl.BlockSpec((tr, LANE), lambda i: (i, 0)),
                   pl.BlockSpec((SUBLANE, LANE), lambda i: (0, 0))],
        out_shape=[jax.ShapeDtypeStruct((n, d), BF16), jax.ShapeDtypeStruct((n, LANE), F32),
                   jax.ShapeDtypeStruct((SUBLANE, LANE), F32)],
        scratch_shapes=[pltpu.VMEM((SUBLANE, LANE), F32)],
        compiler_params=_params(("arbitrary",)),
        name="moe_router",
    )(x2, gain.reshape(1, d).astype(F32), wr)

    e1, e2 = meta[:, 0].astype(I32), meta[:, 1].astype(I32)
    counts = cnt[0, :n_experts].astype(I32)
    ends = jnp.cumsum(counts)
    off = ends - counts
    dest1 = off[e1] + meta[:, 2].astype(I32)
    dest2 = off[e2] + meta[:, 3].astype(I32)
    na = TOP_K_EXPERTS * n
    tok = jnp.arange(n, dtype=I32)
    src = jnp.zeros((na,), I32).at[dest1].set(tok).at[dest2].set(tok)

    xs = _gather_rows(xn, src, rows=_pick(na, (256, 128, 64, 32, 16, 8)))

    tm = _pick(na, (512, 256, 128, 64, 32, 16, 8))
    n_tiles = na // tm
    nv = n_tiles + n_experts - 1
    lo = jnp.sort(jnp.concatenate([jnp.arange(n_tiles, dtype=I32) * tm, ends[:-1]]))
    hi = jnp.concatenate([lo[1:], jnp.full((1,), na, I32)])
    tile = jnp.minimum(lo // tm, n_tiles - 1)
    expert = jnp.minimum(jnp.sum((lo[:, None] >= ends[None, :]).astype(I32), axis=1), n_experts - 1)
    vid = jnp.arange(nv, dtype=I32)
    last = lax.cummax(jnp.where(hi > lo, vid, -1))
    wexp = expert[jnp.maximum(last, 0)]
    first = jnp.concatenate([jnp.ones((1,), I32), (tile[1:] != tile[:-1]).astype(I32)])

    fc = _pick(dff, (256, 128))
    nf = dff // fc

    def up_map(v, f, tile_ref, wexp_ref, lo_ref, hi_ref, first_ref):
        return wexp_ref[v], 0, jnp.where(hi_ref[v] > lo_ref[v], f, nf - 1)

    def down_map(v, f, tile_ref, wexp_ref, lo_ref, hi_ref, first_ref):
        return wexp_ref[v], jnp.where(hi_ref[v] > lo_ref[v], f, nf - 1), 0

    def row_map(v, f, tile_ref, wexp_ref, lo_ref, hi_ref, first_ref):
        return tile_ref[v], 0

    ys = pl.pallas_call(
        functools.partial(_experts_body, tm=tm),
        grid_spec=pltpu.PrefetchScalarGridSpec(
            num_scalar_prefetch=5,
            grid=(nv, nf),
            in_specs=[pl.BlockSpec((tm, d), row_map), pl.BlockSpec((None, d, fc), up_map),
                      pl.BlockSpec((None, d, fc), up_map), pl.BlockSpec((None, fc, d), down_map)],
            out_specs=pl.BlockSpec((tm, d), row_map),
        ),
        out_shape=jax.ShapeDtypeStruct((na, d), F32),
        compiler_params=_params(("arbitrary", "arbitrary")),
        name="moe_experts",
    )(tile, wexp, lo, hi, first, xs, w1[idx].astype(BF16), w3[idx].astype(BF16), w2[idx].astype(BF16))

    rows = _pick(n, (256, 128, 64, 32, 16, 8))
    slab = (rows, d // LANE, LANE)
    slab_spec = pl.BlockSpec(slab, lambda i, d1, d2: (i, 0, 0))
    weight_spec = pl.BlockSpec((rows, 1, LANE), lambda i, d1, d2: (i, 0, 0))
    wt1 = jnp.broadcast_to(meta[:, 4][:, None, None], (n, 1, LANE))
    wt2 = jnp.broadcast_to(meta[:, 5][:, None, None], (n, 1, LANE))
    out = pl.pallas_call(
        functools.partial(_combine_body, rows=rows),
        grid_spec=pltpu.PrefetchScalarGridSpec(
            num_scalar_prefetch=2,
            grid=(n // rows,),
            in_specs=[pl.BlockSpec(memory_space=pl.ANY), slab_spec, weight_spec, weight_spec],
            out_specs=slab_spec,
            scratch_shapes=[pltpu.VMEM(slab, F32), pltpu.VMEM(slab, F32), pltpu.SemaphoreType.DMA((2,))],
        ),
        out_shape=jax.ShapeDtypeStruct((n, d // LANE, LANE), F32),
        compiler_params=_params(("arbitrary",)),
        name="moe_combine",
    )(dest1, dest2, ys.reshape(na, d // LANE, LANE), x2.reshape(n, d // LANE, LANE), wt1, wt2)
    return out.reshape(n, d)


def kernel(x, positions, mix_norm, w_in, mla_q_lat_norm, mla_kv_lat_norm, mla_w_uq, mla_w_ukv, mla_q_norm, mla_k_norm, dsa_q_norm, dsa_k_norm, ssm_lam_re, ssm_lam_im, ssm_log_dt, ssm_b_re, ssm_b_im, ssm_c_re, ssm_c_im, ssm_d, ssm_w_glu, w_branch, w_out, ffn_norm, dense_w_gate, dense_w_up, dense_w_down, moe_w_router, moe_w1, moe_w3, moe_w2):
    bsz, seq, d = x.shape
    depth = mix_norm.shape[0]
    x2 = x.reshape(bsz * seq, d)
    cos, sin = _rope_tables(positions)
    for layer in range(depth):
        x2 = _hybrid_mixer(x2, cos, sin, bsz, seq, mix_norm[layer], w_in[layer], mla_q_lat_norm[layer],
                           mla_kv_lat_norm[layer], mla_w_uq[layer], mla_w_ukv[layer], mla_q_norm[layer],
                           mla_k_norm[layer], dsa_q_norm[layer], dsa_k_norm[layer], ssm_lam_re[layer],
                           ssm_lam_im[layer], ssm_log_dt[layer], ssm_b_re[layer], ssm_b_im[layer], ssm_c_re[layer],
                           ssm_c_im[layer], ssm_d[layer], ssm_w_glu[layer], w_branch[layer], w_out, layer)
        i = layer // 2
        if layer % 2 == 0:
            x2 = _dense_ffn(x2, ffn_norm[layer], dense_w_gate, dense_w_up, dense_w_down, i)
        else:
            x2 = _moe_ffn(x2, ffn_norm[layer], moe_w_router, moe_w1, moe_w3, moe_w2, i)
    return x2.reshape(bsz, seq, d)
```

```python
import functools
import math

import jax
import jax.numpy as jnp
from jax import lax
from jax.experimental import pallas as pl
from jax.experimental.pallas import tpu as pltpu

F32 = jnp.float32
BF16 = jnp.bfloat16
I32 = jnp.int32

SSM_GROUP = 16
SSM_STATE = 64
MLA_HEADS = 12
MLA_NOPE = 128
MLA_ROPE = 64
MLA_V = 128
DSA_HEADS = 12
DSA_HEAD_DIM = 128
IDX_HEADS = 16
IDX_DIM = 64
DSA_TOPK_MAX = 256
TOP_K_EXPERTS = 2
ROPE_THETA = 10000.0
EPS = 1e-6

LANE = 128
SUBLANE = 8
VMEM_LIMIT_BYTES = 56 * 1024 * 1024
NEG_BIG = -1e30
INT_MIN = -(2 ** 31)
LOG2E = 1.4426950408889634


def _params(semantics):
    return pltpu.CompilerParams(dimension_semantics=semantics, vmem_limit_bytes=VMEM_LIMIT_BYTES)


def _sigmoid(x):
    return 1.0 / (1.0 + jnp.exp(-x))


def _pick(n, candidates):
    for c in candidates:
        if n % c == 0:
            return c
    return n


def _rmsnorm_body(x_ref, g_ref, o_ref):
    x = x_ref[...].astype(F32)
    ms = jnp.mean(x * x, axis=-1, keepdims=True)
    o_ref[...] = (x * lax.rsqrt(ms + EPS) * g_ref[...]).astype(o_ref.dtype)


def _rmsnorm(x, gain, *, width=None, col_block=0, out_dtype=BF16):
    m = x.shape[0]
    width = width or x.shape[1]
    tm = _pick(m, (512, 256, 128, 64, 32, 16, 8))
    return pl.pallas_call(
        _rmsnorm_body,
        grid=(m // tm,),
        in_specs=[pl.BlockSpec((tm, width), lambda i: (i, col_block)),
                  pl.BlockSpec((1, width), lambda i: (0, 0))],
        out_specs=pl.BlockSpec((tm, width), lambda i: (i, 0)),
        out_shape=jax.ShapeDtypeStruct((m, width), out_dtype),
        compiler_params=_params(("parallel",)),
        name="rmsnorm",
    )(x, gain.reshape(1, width).astype(F32))


def _mm_body(*refs, n_a, pairs, n_extra, nk, epilogue):
    a_refs = refs[:n_a]
    w_refs = refs[n_a:n_a + len(pairs)]
    e_refs = refs[n_a + len(pairs):n_a + len(pairs) + n_extra]
    o_ref = refs[n_a + len(pairs) + n_extra]
    acc_refs = refs[n_a + len(pairs) + n_extra + 1:]

    a_vals = [a[...].astype(BF16) for a in a_refs]
    parts = [jnp.dot(a_vals[ai], w[...].astype(BF16), preferred_element_type=F32)
             for ai, w in zip(pairs, w_refs)]

    def finish(ps):
        o_ref[...] = epilogue(ps, *[e[...] for e in e_refs]).astype(o_ref.dtype)

    if nk == 1:
        finish(parts)
        return
    k = pl.program_id(2)

    @pl.when(k == 0)
    def _():
        for acc, p in zip(acc_refs, parts):
            acc[...] = p

    @pl.when(k > 0)
    def _():
        for acc, p in zip(acc_refs, parts):
            acc[...] += p

    @pl.when(k == nk - 1)
    def _():
        finish([acc[...] for acc in acc_refs])


def _matmul(a_list, w_list, *, epilogue, out_dtype, tn, tm=None, tk=None, extras=(), aliases=None, name="matmul"):
    m = a_list[0].shape[0]
    n = w_list[0][1].shape[-1]
    kdim = a_list[0].shape[1]
    tm = tm or _pick(m, (1024, 512, 256, 128, 64, 32, 16, 8))
    nk = 1 if tk is None else kdim // tk
    grid = (m // tm, n // tn, nk)
    in_specs = []
    for a in a_list:
        ka = a.shape[1] if nk == 1 else tk
        in_specs.append(pl.BlockSpec((tm, ka), lambda i, j, k: (i, k)))
    for _, w, prefix in w_list:
        kw = w.shape[-2] if nk == 1 else tk
        in_specs.append(pl.BlockSpec((None,) * len(prefix) + (kw, tn),
                                     lambda i, j, k, prefix=prefix: tuple(prefix) + (k, j)))
    for arr, kind, arg in extras:
        if kind == "mn":
            in_specs.append(pl.BlockSpec((tm, tn), lambda i, j, k, arg=arg: (i, j + arg)))
        elif kind == "row":
            in_specs.append(pl.BlockSpec((1, tn), lambda i, j, k: (0, j)))
        else:
            in_specs.append(pl.BlockSpec((tm, arr.shape[1]), lambda i, j, k: (i, 0)))
    pairs = tuple(ai for ai, _, _ in w_list)
    body = functools.partial(_mm_body, n_a=len(a_list), pairs=pairs, n_extra=len(extras), nk=nk, epilogue=epilogue)
    scratch = [pltpu.VMEM((tm, tn), F32) for _ in pairs] if nk > 1 else []
    return pl.pallas_call(
        body,
        grid=grid,
        in_specs=in_specs,
        out_specs=pl.BlockSpec((tm, tn), lambda i, j, k: (i, j)),
        out_shape=jax.ShapeDtypeStruct((m, n), out_dtype),
        scratch_shapes=scratch,
        input_output_aliases=aliases or {},
        compiler_params=_params(("parallel", "parallel", "arbitrary")),
        name=name,
    )(*a_list, *[w for _, w, _ in w_list], *[arr for arr, _, _ in extras])


def _ep_plain(ps):
    return ps[0]


def _ep_residual(ps, res):
    return res + ps[0]


def _ep_swiglu(ps):
    g = ps[0]
    return g * _sigmoid(g) * ps[1]


def _ep_glu(ps):
    return ps[0] * _sigmoid(ps[1])


def _ep_branches(ps, g0, g1, g2):
    return _sigmoid(g0) * ps[0] + _sigmoid(g1) * ps[1] + _sigmoid(g2) * ps[2]


def _s5_scan_body(bre_ref, bim_ref, a_ref, o_re_ref, o_im_ref, xr_ref, xi_ref, car_ref, *, steps, seg_len):
    t = pl.program_id(2)
    lw = bre_ref.shape[-1]

    @pl.when(t == 0)
    def _():
        car_ref[...] = jnp.zeros_like(car_ref)

    ar = jnp.broadcast_to(a_ref[0:1, :], (SUBLANE, lw))
    ai = jnp.broadcast_to(a_ref[1:2, :], (SUBLANE, lw))

    def pass1(j, carry):
        xr, xi = carry
        off = pl.multiple_of(j * SUBLANE, SUBLANE)
        nr = ar * xr - ai * xi + bre_ref[0, pl.ds(off, SUBLANE), :]
        ni = ar * xi + ai * xr + bim_ref[0, pl.ds(off, SUBLANE), :]
        xr_ref[pl.ds(off, SUBLANE), :] = nr
        xi_ref[pl.ds(off, SUBLANE), :] = ni
        return nr, ni

    zero = jnp.zeros((SUBLANE, lw), F32)
    er, ei = lax.fori_loop(0, steps, pass1, (zero, zero))

    pr = a_ref[2:3, :]
    pi = a_ref[3:4, :]
    cr = car_ref[0:1, :]
    ci = car_ref[1:2, :]
    rows_r, rows_i = [], []
    for s in range(SUBLANE):
        rows_r.append(cr)
        rows_i.append(ci)
        nr = pr * cr - pi * ci + er[s:s + 1, :]
        ni = pr * ci + pi * cr + ei[s:s + 1, :]
        cr, ci = nr, ni
    car_ref[0:1, :] = cr
    car_ref[1:2, :] = ci
    cin_r = jnp.concatenate(rows_r, axis=0)
    cin_i = jnp.concatenate(rows_i, axis=0)

    def pass2(j, carry):
        cr_, ci_ = carry
        off = pl.multiple_of(j * SUBLANE, SUBLANE)
        nr = ar * cr_ - ai * ci_
        ni = ar * ci_ + ai * cr_
        o_re_ref[0, pl.ds(off, SUBLANE), :] = (xr_ref[pl.ds(off, SUBLANE), :] + nr).astype(o_re_ref.dtype)
        o_im_ref[0, pl.ds(off, SUBLANE), :] = (xi_ref[pl.ds(off, SUBLANE), :] + ni).astype(o_im_ref.dtype)
        return nr, ni

    lax.fori_loop(0, steps, pass2, (cin_r, cin_i))


def _s5_scan(bu_re, bu_im, a_tab, *, chunk):
    b, s, w = bu_re.shape
    lw = _pick(w, (512, 256, 128))
    steps = chunk // SUBLANE
    body = functools.partial(_s5_scan_body, steps=steps, seg_len=steps)
    blk = pl.BlockSpec((1, chunk, lw), lambda bi, li, ti: (bi, ti, li))
    return pl.pallas_call(
        body,
        grid=(b, w // lw, s // chunk),
        in_specs=[blk, blk, pl.BlockSpec((4, lw), lambda bi, li, ti: (0, li))],
        out_specs=[blk, blk],
        out_shape=[jax.ShapeDtypeStruct((b, s, w), BF16)] * 2,
        scratch_shapes=[pltpu.VMEM((chunk, lw), F32), pltpu.VMEM((chunk, lw), F32), pltpu.VMEM((SUBLANE, lw), F32)],
        compiler_params=_params(("parallel", "parallel", "arbitrary")),
        name="s5_scan",
    )(bu_re, bu_im, a_tab)


def _s5_chunk(s):
    return _pick(s, (512, 256, 128, 64))


def _to_chunk_order(t, chunk):
    b, s, w = t.shape
    return t.reshape(b, s // chunk, SUBLANE, chunk // SUBLANE, w).transpose(0, 1, 3, 2, 4).reshape(b, s, w)


def _from_chunk_order(t, chunk):
    b, s, w = t.shape
    return t.reshape(b, s // chunk, chunk // SUBLANE, SUBLANE, w).transpose(0, 1, 3, 2, 4).reshape(b, s, w)


def _s5_discretise(lam_re, lam_im, log_dt, b_re, b_im, c_re, c_im, steps):
    g, p = lam_re.shape
    h = b_re.shape[-1]
    lr, li = lam_re.astype(F32), lam_im.astype(F32)
    dt = jnp.exp(log_dt.astype(F32))[:, None]
    mag = jnp.exp(lr * dt)
    ab_re, ab_im = mag * jnp.cos(li * dt), mag * jnp.sin(li * dt)
    den = lr * lr + li * li
    nr = ab_re - 1.0
    zr = (nr * lr + ab_im * li) / den
    zi = (ab_im * lr - nr * li) / den
    bb_re = zr[..., None] * b_re - zi[..., None] * b_im
    bb_im = zr[..., None] * b_im + zi[..., None] * b_re
    gs = LANE // h
    ns = g // gs
    eye = jnp.eye(gs, dtype=F32)
    slab_in = lambda t: jnp.einsum("sgph,gk->sghkp", t.reshape(ns, gs, p, h), eye).reshape(ns, gs * h, gs * p)
    slab_out = lambda t: jnp.einsum("sghp,gk->sgpkh", t.reshape(ns, gs, h, p), eye).reshape(ns, gs * p, gs * h)
    pr, pi = ab_re, ab_im
    for _ in range(int(round(math.log2(steps)))):
        pr, pi = pr * pr - pi * pi, 2.0 * pr * pi
    a_tab = jnp.stack([ab_re.reshape(-1), ab_im.reshape(-1), pr.reshape(-1), pi.reshape(-1)])
    return (slab_in(bb_re).astype(BF16), slab_in(bb_im).astype(BF16), slab_out(c_re.astype(F32)).astype(BF16),
            slab_out(-c_im.astype(F32)).astype(BF16), a_tab)


def _s5_in_body(u_ref, wre_ref, wim_ref, re_ref, im_ref):
    u = u_ref[...].astype(BF16)
    re_ref[...] = jnp.dot(u, wre_ref[...], preferred_element_type=F32)
    im_ref[...] = jnp.dot(u, wim_ref[...], preferred_element_type=F32)


def _s5_out_body(xre_ref, xim_ref, cre_ref, cim_ref, u_ref, d_ref, y_ref):
    y = (jnp.dot(xre_ref[...], cre_ref[...], preferred_element_type=F32)
         + jnp.dot(xim_ref[...], cim_ref[...], preferred_element_type=F32) + d_ref[...] * u_ref[...])
    y_ref[...] = jax.nn.gelu(y).astype(y_ref.dtype)


def _s5_mixer(u, bsz, seq, lam_re, lam_im, log_dt, b_re, b_im, c_re, c_im, d_skip, w_glu):
    n, w = u.shape
    chunk = _s5_chunk(seq)
    steps = chunk // SUBLANE
    bin_re, bin_im, bout_re, bout_im, a_tab = _s5_discretise(lam_re, lam_im, log_dt, b_re, b_im, c_re, c_im, steps)
    ns, _, sw = bin_re.shape
    wst = ns * sw
    u_p = _to_chunk_order(u.reshape(bsz, seq, w), chunk).reshape(n, w)
    tm = _pick(n, (1024, 512, 256, 128, 64, 32, 16, 8))
    lanes = pl.BlockSpec((tm, LANE), lambda i, s: (i, s))
    states = pl.BlockSpec((tm, sw), lambda i, s: (i, s))
    bu_re, bu_im = pl.pallas_call(
        _s5_in_body,
        grid=(n // tm, ns),
        in_specs=[lanes, pl.BlockSpec((None, LANE, sw), lambda i, s: (s, 0, 0)),
                  pl.BlockSpec((None, LANE, sw), lambda i, s: (s, 0, 0))],
        out_specs=[states, states],
        out_shape=[jax.ShapeDtypeStruct((n, wst), F32)] * 2,
        compiler_params=_params(("parallel", "parallel")),
        name="s5_in",
    )(u_p, bin_re, bin_im)
    x_re, x_im = _s5_scan(bu_re.reshape(bsz, seq, wst), bu_im.reshape(bsz, seq, wst), a_tab, chunk=chunk)
    y = pl.pallas_call(
        _s5_out_body,
        grid=(n // tm, ns),
        in_specs=[states, states, pl.BlockSpec((None, sw, LANE), lambda i, s: (s, 0, 0)),
                  pl.BlockSpec((None, sw, LANE), lambda i, s: (s, 0, 0)), lanes,
                  pl.BlockSpec((1, LANE), lambda i, s: (0, s))],
        out_specs=lanes,
        out_shape=jax.ShapeDtypeStruct((n, w), BF16),
        compiler_params=_params(("parallel", "parallel")),
        name="s5_out",
    )(x_re.reshape(n, wst), x_im.reshape(n, wst), bout_re, bout_im, u_p, d_skip.reshape(1, w).astype(F32))
    tnw = _pick(w, (512, 256, 128))
    w_a = w_glu[:, :w].astype(BF16)
    w_g = w_glu[:, w:].astype(BF16)
    out = _matmul([y], [(0, w_a, ()), (0, w_g, ())], epilogue=_ep_glu, out_dtype=BF16, tn=tnw, name="s5_glu")
    return _from_chunk_order(out.reshape(bsz, seq, w), chunk).reshape(n, w)


def _rope_table_body(pos_ref, inv_ref, cos_ref, sin_ref):
    ang = pos_ref[...].astype(F32) * inv_ref[...]
    cos_ref[...] = jnp.cos(ang)
    sin_ref[...] = jnp.sin(ang)


def _rope_tables(positions):
    n = positions.size
    half = MLA_ROPE // 2
    inv = ROPE_THETA ** (-jnp.arange(half, dtype=F32) / half)
    inv = jnp.zeros((1, LANE), F32).at[0, :half].set(inv)
    tm = _pick(n, (512, 256, 128, 64, 32, 16, 8))
    return pl.pallas_call(
        _rope_table_body,
        grid=(n // tm,),
        in_specs=[pl.BlockSpec((tm, 1), lambda i: (i, 0)), pl.BlockSpec((1, LANE), lambda i: (0, 0))],
        out_specs=[pl.BlockSpec((tm, LANE), lambda i: (i, 0))] * 2,
        out_shape=[jax.ShapeDtypeStruct((n, LANE), F32)] * 2,
        compiler_params=_params(("parallel",)),
        name="rope_table",
    )(positions.reshape(n, 1).astype(I32), inv)


def _head_norm_rope(t0, t1, t2, g0, g1, g2, cos, sin, true_dim):
    ss = (jnp.sum(t0 * t0, axis=-1, keepdims=True) + jnp.sum(t1 * t1, axis=-1, keepdims=True)
          + jnp.sum(t2 * t2, axis=-1, keepdims=True))
    inv = lax.rsqrt(ss * (1.0 / true_dim) + EPS)
    n0 = t0 * inv * g0
    n1 = t1 * inv * g1
    n2 = t2 * inv * g2
    o1 = n1 * cos - n2 * sin
    o2 = n1 * sin + n2 * cos
    half = MLA_ROPE // 2
    lane = lax.broadcasted_iota(I32, o1.shape, 1)
    o1 = jnp.where(lane < half, o1, 0.0)
    o2 = jnp.where(lane < half, o2, 0.0)
    return jnp.concatenate([n0, o1 + pltpu.roll(o2, half, 1)], axis=-1)


def _mla_q_prep_body(q_ref, g_ref, cos_ref, sin_ref, o_ref):
    cos, sin = cos_ref[...], sin_ref[...]
    hw = 3 * LANE
    g0, g1, g2 = g_ref[:, 0:LANE], g_ref[:, LANE:2 * LANE], g_ref[:, 2 * LANE:hw]
    for h in range(MLA_HEADS):
        t0 = q_ref[:, h * hw:h * hw + LANE]
        t1 = q_ref[:, h * hw + LANE:h * hw + 2 * LANE]
        t2 = q_ref[:, h * hw + 2 * LANE:(h + 1) * hw]
        qh = _head_norm_rope(t0, t1, t2, g0, g1, g2, cos, sin, MLA_NOPE + MLA_ROPE)
        o_ref[0, h] = (qh * ((MLA_NOPE + MLA_ROPE) ** -0.5 * LOG2E)).astype(o_ref.dtype)


def _mla_kv_prep_body(kv_ref, pe_ref, g_ref, cos_ref, sin_ref, k_ref, v_ref):
    cos, sin = cos_ref[...], sin_ref[...]
    hw = 3 * LANE
    g0, g1, g2 = g_ref[:, 0:LANE], g_ref[:, LANE:2 * LANE], g_ref[:, 2 * LANE:hw]
    t1 = pe_ref[:, 0:LANE]
    t2 = pe_ref[:, LANE:2 * LANE]
    kvw = MLA_NOPE + MLA_V
    for h in range(MLA_HEADS):
        t0 = kv_ref[:, h * kvw:h * kvw + MLA_NOPE]
        k_ref[0, h] = _head_norm_rope(t0, t1, t2, g0, g1, g2, cos, sin, MLA_NOPE + MLA_ROPE).astype(k_ref.dtype)
        v_ref[0, h] = kv_ref[:, h * kvw + MLA_NOPE:(h + 1) * kvw].astype(v_ref.dtype)


def _pad_rope_gain(gain):
    half = MLA_ROPE // 2
    out = jnp.zeros((3 * LANE,), F32)
    out = out.at[:MLA_NOPE].set(gain[:MLA_NOPE].astype(F32))
    out = out.at[LANE:LANE + half].set(gain[MLA_NOPE:MLA_NOPE + half].astype(F32))
    out = out.at[2 * LANE:2 * LANE + half].set(gain[MLA_NOPE + half:].astype(F32))
    return out.reshape(1, 3 * LANE)


def _flash_body(q_ref, k_ref, v_ref, o_ref, *, tq, wide):
    qi = pl.program_id(2)
    dv = v_ref.shape[-1]
    q = q_ref[0, 0]

    def step(off, width, carry, masked):
        m, l, acc = carry
        k = k_ref[0, 0, pl.ds(off, width), :]
        v = v_ref[0, 0, pl.ds(off, width), :]
        s = lax.dot_general(q, k, (((1,), (1,)), ((), ())), preferred_element_type=F32)
        if masked:
            row = lax.broadcasted_iota(I32, (tq, width), 0)
            col = lax.broadcasted_iota(I32, (tq, width), 1)
            s = jnp.where(col <= row, s, NEG_BIG)
        m_new = jnp.maximum(m, jnp.max(s, axis=1, keepdims=True))
        alpha = jnp.exp2(m - m_new)
        p = jnp.exp2(s - m_new)
        l = alpha * l + jnp.sum(p, axis=1, keepdims=True)
        acc = alpha * acc + jnp.dot(p.astype(BF16), v, preferred_element_type=F32)
        return m_new, l, acc

    per = wide // tq
    n_wide = qi // per
    carry = (jnp.full((tq, 1), NEG_BIG, F32), jnp.zeros((tq, 1), F32), jnp.zeros((tq, dv), F32))
    carry = lax.fori_loop(0, n_wide, lambda j, c: step(pl.multiple_of(j * wide, wide), wide, c, False), carry)
    carry = lax.fori_loop(n_wide * per, qi, lambda j, c: step(pl.multiple_of(j * tq, tq), tq, c, False), carry)
    m, l, acc = step(pl.multiple_of(qi * tq, tq), tq, carry, True)
    o_ref[0] = (acc / l).astype(o_ref.dtype)


def _flash_attention(q, k, v):
    b, h, s, dk = q.shape
    dv = v.shape[-1]
    tq = _pick(s, (512, 256, 128))
    wide = _pick(s, (4 * tq, 2 * tq, tq))
    return pl.pallas_call(
        functools.partial(_flash_body, tq=tq, wide=wide),
        grid=(b, h, s // tq),
        in_specs=[pl.BlockSpec((1, 1, tq, dk), lambda bi, hi, qi: (bi, hi, qi, 0)),
                  pl.BlockSpec((1, 1, s, dk), lambda bi, hi, qi: (bi, hi, 0, 0)),
                  pl.BlockSpec((1, 1, s, dv), lambda bi, hi, qi: (bi, hi, 0, 0))],
        out_specs=pl.BlockSpec((1, tq, dv), lambda bi, hi, qi: (bi, qi, hi)),
        out_shape=jax.ShapeDtypeStruct((b, s, h * dv), BF16),
        compiler_params=_params(("parallel", "parallel", "arbitrary")),
        name="mla_flash",
    )(q, k, v)


def _mla_mixer(hm, cos, sin, bsz, seq, q_lora, kv_lora, q_lat_gain, kv_lat_gain, w_uq, w_ukv, q_gain, k_gain):
    n = hm.shape[0]
    half = MLA_ROPE // 2
    qk = MLA_NOPE + MLA_ROPE
    cqn = _rmsnorm(hm, q_lat_gain, width=q_lora, col_block=0)
    ckvn = _rmsnorm(hm, kv_lat_gain, width=kv_lora, col_block=q_lora // kv_lora)
    wq = w_uq.reshape(q_lora, MLA_HEADS, qk)
    wq_p = jnp.zeros((q_lora, MLA_HEADS, 3 * LANE), BF16)
    wq_p = wq_p.at[:, :, :MLA_NOPE].set(wq[:, :, :MLA_NOPE].astype(BF16))
    wq_p = wq_p.at[:, :, LANE:LANE + half].set(wq[:, :, MLA_NOPE:MLA_NOPE + half].astype(BF16))
    wq_p = wq_p.at[:, :, 2 * LANE:2 * LANE + half].set(wq[:, :, MLA_NOPE + half:].astype(BF16))
    wq_p = wq_p.reshape(q_lora, MLA_HEADS * 3 * LANE)
    q_raw = _matmul([cqn], [(0, wq_p, ())], epilogue=_ep_plain, out_dtype=F32, tn=3 * LANE, name="mla_uq")
    kv_raw = _matmul([ckvn], [(0, w_ukv.astype(BF16), ())], epilogue=_ep_plain, out_dtype=F32,
                     tn=MLA_NOPE + MLA_V, name="mla_ukv")

    tm = _pick(seq, (256, 128, 64, 32, 16))
    nblk = seq // tm
    qg = _pad_rope_gain(q_gain)
    kg = _pad_rope_gain(k_gain)
    row = lambda bi, si: (bi * nblk + si, 0)
    q = pl.pallas_call(
        _mla_q_prep_body,
        grid=(bsz, nblk),
        in_specs=[pl.BlockSpec((tm, MLA_HEADS * 3 * LANE), row), pl.BlockSpec((1, 3 * LANE), lambda bi, si: (0, 0)),
                  pl.BlockSpec((tm, LANE), row), pl.BlockSpec((tm, LANE), row)],
        out_specs=pl.BlockSpec((1, MLA_HEADS, tm, 2 * LANE), lambda bi, si: (bi, 0, si, 0)),
        out_shape=jax.ShapeDtypeStruct((bsz, MLA_HEADS, seq, 2 * LANE), BF16),
        compiler_params=_params(("parallel", "parallel")),
        name="mla_q_prep",
    )(q_raw, qg, cos, sin)
    pe_block = (q_lora + kv_lora) // (2 * LANE)
    k, v = pl.pallas_call(
        _mla_kv_prep_body,
        grid=(bsz, nblk),
        in_specs=[pl.BlockSpec((tm, MLA_HEADS * (MLA_NOPE + MLA_V)), row),
                  pl.BlockSpec((tm, 2 * LANE), lambda bi, si: (bi * nblk + si, pe_block)),
                  pl.BlockSpec((1, 3 * LANE), lambda bi, si: (0, 0)),
                  pl.BlockSpec((tm, LANE), row), pl.BlockSpec((tm, LANE), row)],
        out_specs=[pl.BlockSpec((1, MLA_HEADS, tm, 2 * LANE), lambda bi, si: (bi, 0, si, 0)),
                   pl.BlockSpec((1, MLA_HEADS, tm, MLA_V), lambda bi, si: (bi, 0, si, 0))],
        out_shape=[jax.ShapeDtypeStruct((bsz, MLA_HEADS, seq, 2 * LANE), BF16),
                   jax.ShapeDtypeStruct((bsz, MLA_HEADS, seq, MLA_V), BF16)],
        compiler_params=_params(("parallel", "parallel")),
        name="mla_kv_prep",
    )(kv_raw, hm, kg, cos, sin)
    return _flash_attention(q, k, v).reshape(n, MLA_HEADS * MLA_V)


def _dsa_prep_body(hd_q_ref, hd_k_ref, hd_v_ref, hi_q_ref, hi_k_ref, hi_w_ref, qg_ref, kg_ref,
                   q_ref, k_ref, v_ref, qi_ref, klo_ref, khi_ref, w_ref):
    def norm(t, g):
        ms = jnp.mean(t * t, axis=-1, keepdims=True)
        return t * lax.rsqrt(ms + EPS) * g

    qg = qg_ref[...]
    for h in range(DSA_HEADS):
        qh = norm(hd_q_ref[:, h * DSA_HEAD_DIM:(h + 1) * DSA_HEAD_DIM], qg)
        q_ref[0, h] = (qh * (DSA_HEAD_DIM ** -0.5 * LOG2E)).astype(q_ref.dtype)
    k_ref[0] = norm(hd_k_ref[...], kg_ref[...]).astype(k_ref.dtype)
    v_ref[0] = hd_v_ref[...].astype(v_ref.dtype)
    qi_ref[0] = hi_q_ref[...].astype(qi_ref.dtype)
    ki = hi_k_ref[...]
    lane = lax.broadcasted_iota(I32, ki.shape, 1)
    ki = jnp.where(lane < IDX_DIM, ki, 0.0)
    klo_ref[0] = ki.astype(klo_ref.dtype)
    khi_ref[0] = pltpu.roll(ki, IDX_DIM, 1).astype(khi_ref.dtype)
    w_ref[0] = hi_w_ref[...] * ((IDX_HEADS ** -0.5) * (IDX_DIM ** -0.5))


def _float_key(x):
    bits = pltpu.bitcast(x, I32)
    return jnp.where(bits < 0, bits ^ jnp.int32(0x7FFFFFFF), bits)


def _dsa_body(q_ref, k_ref, v_ref, qi_ref, klo_ref, khi_ref, w_ref, o_ref, keys_ref, gmax_ref, acc_ref,
              *, tq, tk, wide, n_sel):
    qb = pl.program_id(1)
    n_kv = (qb * tq + tq + tk - 1) // tk
    row = lax.broadcasted_iota(I32, (tq, tk), 0) + qb * tq
    col0 = lax.broadcasted_iota(I32, (tq, tk), 1)

    w = w_ref[0]

    def score_tile(j, _):
        off = pl.multiple_of(j * tk, tk)
        klo = klo_ref[0, pl.ds(off, tk), :]
        khi = khi_ref[0, pl.ds(off, tk), :]
        acc = jnp.zeros((tq, tk), F32)
        for hp in range(IDX_HEADS // 2):
            q2 = qi_ref[0, :, hp * LANE:(hp + 1) * LANE]
            for half, kk in ((0, klo), (1, khi)):
                h = 2 * hp + half
                logit = lax.dot_general(q2, kk, (((1,), (1,)), ((), ())), preferred_element_type=F32)
                acc = acc + w[:, h:h + 1] * jnp.maximum(logit, 0.0)
        causal = col0 + off <= row
        keys_ref[:, pl.ds(off, tk)] = jnp.where(causal, _float_key(acc + 0.0), INT_MIN)
        sc = jnp.where(causal, acc, -jnp.inf)
        for c in range(tk // LANE):
            gmax_ref[c % 2] = jnp.maximum(gmax_ref[c % 2], sc[:, c * LANE:(c + 1) * LANE])
        return 0

    gmax_ref[...] = jnp.full(gmax_ref.shape, -jnp.inf, F32)
    lax.fori_loop(0, n_kv, score_tile, 0)

    rc = min(tq, 128)

    def row_group(g, carry):
        r0 = pl.multiple_of(g * rc, rc)
        ge = gmax_ref[0, pl.ds(r0, rc), :]
        go = gmax_ref[1, pl.ds(r0, rc), :]
        key_l = _float_key(jnp.min(jnp.minimum(ge, go), axis=1, keepdims=True) + 0.0)
        key_u = _float_key(jnp.max(jnp.maximum(ge, go), axis=1, keepdims=True) + 0.0)
        shared = lax.clz(key_l ^ key_u)
        start = jnp.min(shared)
        mask = jnp.where(start > 0, lax.shift_left(jnp.int32(-1), 32 - jnp.maximum(start, 1)), 0)
        lo_init = ((key_l ^ INT_MIN) & mask) ^ INT_MIN
        lo_init = jnp.broadcast_to(lo_init, (rc, LANE))

        def bit_step(b, lo):
            cand = lo + lax.shift_left(jnp.int32(1), 31 - b)

            def count_tile(j, part):
                off = pl.multiple_of(j * tk, tk)
                for c in range(tk // LANE):
                    kc = keys_ref[pl.ds(r0, rc), pl.ds(off + c * LANE, LANE)]
                    part = part + jnp.where(kc >= cand, 1.0, 0.0)
                return part

            part = lax.fori_loop(0, n_kv, count_tile, jnp.zeros((rc, LANE), F32))
            cnt = jnp.sum(part, axis=1, keepdims=True)
            return jnp.where(cnt >= n_sel, cand, lo)

        thr = lax.fori_loop(start, 32, bit_step, lo_init)
        thr = jnp.maximum(thr, INT_MIN + 1)

        def bias_tile(j, c2):
            off = pl.multiple_of(j * tk, tk)
            for c in range(tk // LANE):
                kc = keys_ref[pl.ds(r0, rc), pl.ds(off + c * LANE, LANE)]
                bias = jnp.where(kc >= thr, 0.0, NEG_BIG)
                keys_ref[pl.ds(r0, rc), pl.ds(off + c * LANE, LANE)] = pltpu.bitcast(bias, I32)
            return c2

        lax.fori_loop(0, n_kv, bias_tile, 0)
        return carry

    lax.fori_loop(0, tq // rc, row_group, 0)

    per = wide // tk
    n_wide = n_kv // per

    def head(h, carry):
        q = q_ref[0, h]

        def step(off, width, st):
            m, l, acc = st
            k = k_ref[0, pl.ds(off, width), :]
            v = v_ref[0, pl.ds(off, width), :]
            bias = pltpu.bitcast(keys_ref[:, pl.ds(off, width)], F32)
            s = lax.dot_general(q, k, (((1,), (1,)), ((), ())), preferred_element_type=F32) + bias
            m_new = jnp.maximum(m, jnp.max(s, axis=1, keepdims=True))
            alpha = jnp.exp2(m - m_new)
            p = jnp.exp2(s - m_new)
            l = alpha * l + jnp.sum(p, axis=1, keepdims=True)
            acc = alpha * acc + jnp.dot(p.astype(BF16), v, preferred_element_type=F32)
            return m_new, l, acc

        st = (jnp.full((tq, 1), NEG_BIG, F32), jnp.zeros((tq, 1), F32), jnp.zeros((tq, DSA_HEAD_DIM), F32))
        st = lax.fori_loop(0, n_wide, lambda j, c: step(pl.multiple_of(j * wide, wide), wide, c), st)
        st = lax.fori_loop(n_wide * per, n_kv, lambda j, c: step(pl.multiple_of(j * tk, tk), tk, c), st)
        m, l, acc = st
        acc_ref[h] = acc / l
        return carry

    lax.fori_loop(0, DSA_HEADS, head, 0)
    for h in range(DSA_HEADS):
        o_ref[0, :, h * DSA_HEAD_DIM:(h + 1) * DSA_HEAD_DIM] = acc_ref[h].astype(o_ref.dtype)


def _dsa_mixer(hd, hi, bsz, seq, q_gain, k_gain):
    n = hd.shape[0]
    qw = DSA_HEADS * DSA_HEAD_DIM
    iw = IDX_HEADS * IDX_DIM
    tm = _pick(seq, (256, 128, 64, 32, 16))
    nblk = seq // tm

    def cols(width, block):
        return pl.BlockSpec((tm, width), lambda bi, si: (bi * nblk + si, block))

    per_tok = lambda width: pl.BlockSpec((1, tm, width), lambda bi, si: (bi, si, 0))
    q, k, v, qi, klo, khi, w = pl.pallas_call(
        _dsa_prep_body,
        grid=(bsz, nblk),
        in_specs=[cols(qw, 0), cols(LANE, qw // LANE), cols(LANE, qw // LANE + 1),
                  cols(iw, 0), cols(LANE, iw // LANE), cols(LANE, iw // LANE + 1),
                  pl.BlockSpec((1, DSA_HEAD_DIM), lambda bi, si: (0, 0)),
                  pl.BlockSpec((1, DSA_HEAD_DIM), lambda bi, si: (0, 0))],
        out_specs=[pl.BlockSpec((1, DSA_HEADS, tm, DSA_HEAD_DIM), lambda bi, si: (bi, 0, si, 0)),
                   per_tok(DSA_HEAD_DIM), per_tok(DSA_HEAD_DIM), per_tok(iw), per_tok(LANE), per_tok(LANE),
                   per_tok(LANE)],
        out_shape=[jax.ShapeDtypeStruct((bsz, DSA_HEADS, seq, DSA_HEAD_DIM), BF16),
                   jax.ShapeDtypeStruct((bsz, seq, DSA_HEAD_DIM), BF16),
                   jax.ShapeDtypeStruct((bsz, seq, DSA_HEAD_DIM), BF16),
                   jax.ShapeDtypeStruct((bsz, seq, iw), BF16),
                   jax.ShapeDtypeStruct((bsz, seq, LANE), BF16),
                   jax.ShapeDtypeStruct((bsz, seq, LANE), BF16),
                   jax.ShapeDtypeStruct((bsz, seq, LANE), F32)],
        compiler_params=_params(("parallel", "parallel")),
        name="dsa_prep",
    )(hd, hd, hd, hi, hi, hi, q_gain.reshape(1, -1).astype(F32), k_gain.reshape(1, -1).astype(F32))

    tq = _pick(seq, (256, 128))
    tk = _pick(seq, (512, 256, 128))
    n_sel = min(DSA_TOPK_MAX, seq // 4)
    assert n_sel <= 2 * LANE
    whole =lambda width: pl.BlockSpec((1, seq, width), lambda bi, qb: (bi, 0, 0))
    out = pl.pallas_call(
        functools.partial(_dsa_body, tq=tq, tk=tk, wide=_pick(seq, (4 * tk, 2 * tk, tk)), n_sel=n_sel),
        grid=(bsz, seq // tq),
        in_specs=[pl.BlockSpec((1, DSA_HEADS, tq, DSA_HEAD_DIM), lambda bi, qb: (bi, 0, qb, 0)),
                  whole(DSA_HEAD_DIM), whole(DSA_HEAD_DIM),
                  pl.BlockSpec((1, tq, iw), lambda bi, qb: (bi, qb, 0)),
                  whole(LANE), whole(LANE),
                  pl.BlockSpec((1, tq, LANE), lambda bi, qb: (bi, qb, 0))],
        out_specs=pl.BlockSpec((1, tq, qw), lambda bi, qb: (bi, qb, 0)),
        out_shape=jax.ShapeDtypeStruct((bsz, seq, qw), BF16),
        scratch_shapes=[pltpu.VMEM((tq, seq), I32), pltpu.VMEM((2, tq, LANE), F32),
                        pltpu.VMEM((DSA_HEADS, tq, DSA_HEAD_DIM), F32)],
        compiler_params=_params(("parallel", "arbitrary")),
        name="dsa_attention",
    )(q, k, v, qi, klo, khi, w)
    return out.reshape(n, qw)


def _pad_cols(w, width):
    return jnp.pad(w, ((0, 0), (0, width - w.shape[1])))


def _hybrid_mixer(x2, cos, sin, bsz, seq, norm_gain, w_in, q_lat_gain, kv_lat_gain, w_uq, w_ukv, mla_q_gain,
                  mla_k_gain, dsa_q_gain, dsa_k_gain, lam_re, lam_im, log_dt, b_re, b_im, c_re, c_im, d_skip, w_glu,
                  w_branch, w_out, layer):
    n, d = x2.shape
    ssm_w = d_skip.shape[-1]
    q_lora = q_lat_gain.shape[-1]
    kv_lora = kv_lat_gain.shape[-1]
    half = MLA_ROPE // 2
    dsa_w = DSA_HEADS * DSA_HEAD_DIM
    idx_w = IDX_HEADS * IDX_DIM
    sizes = (ssm_w, q_lora, kv_lora, MLA_ROPE, dsa_w, DSA_HEAD_DIM, DSA_HEAD_DIM, idx_w, IDX_DIM, IDX_HEADS, 3 * d)
    offs = [0]
    for s in sizes:
        offs.append(offs[-1] + s)
    seg = lambda i: w_in[:, offs[i]:offs[i + 1]]
    w_u = seg(0).astype(BF16)
    pe = seg(3)
    w_mla = jnp.concatenate([seg(1), seg(2), _pad_cols(pe[:, :half], LANE), _pad_cols(pe[:, half:], LANE)],
                            axis=1).astype(BF16)
    w_dsa = jnp.concatenate([seg(4), seg(5), seg(6)], axis=1).astype(BF16)
    w_idx = jnp.concatenate([seg(7), _pad_cols(seg(8), LANE), _pad_cols(seg(9), LANE)], axis=1).astype(BF16)
    w_gate = seg(10).astype(BF16)

    xn = _rmsnorm(x2, norm_gain)
    proj = lambda w, name: _matmul([xn], [(0, w, ())], epilogue=_ep_plain, out_dtype=F32,
                                   tn=_pick(w.shape[1], (256, 128)), name=name)
    u = proj(w_u, "in_ssm")
    hm = proj(w_mla, "in_mla")
    hd = proj(w_dsa, "in_dsa")
    hi = proj(w_idx, "in_idx")
    g = proj(w_gate, "in_gates")

    y_ssm = _s5_mixer(u, bsz, seq, lam_re, lam_im, log_dt, b_re, b_im, c_re, c_im, d_skip, w_glu)
    y_mla = _mla_mixer(hm, cos, sin, bsz, seq, q_lora, kv_lora, q_lat_gain, kv_lat_gain, w_uq, w_ukv,
                       mla_q_gain, mla_k_gain)
    y_dsa = _dsa_mixer(hd, hi, bsz, seq, dsa_q_gain, dsa_k_gain)

    br0 = ssm_w
    br1 = ssm_w + MLA_HEADS * MLA_V
    tn = _pick(d, (256, 128))
    merged = _matmul([y_ssm, y_mla, y_dsa],
                     [(0, w_branch[:br0].astype(BF16), ()), (1, w_branch[br0:br1].astype(BF16), ()),
                      (2, w_branch[br1:].astype(BF16), ())],
                     epilogue=_ep_branches, out_dtype=BF16, tn=tn,
                     extras=[(g, "mn", 0), (g, "mn", d // tn), (g, "mn", 2 * (d // tn))], name="branch_merge")
    return _matmul([merged], [(0, w_out, (layer,))], epilogue=_ep_residual, out_dtype=F32, tn=tn,
                   extras=[(x2, "mn", 0)], name="mixer_out")


def _dense_ffn(x2, gain, w_gate, w_up, w_down, idx):
    d = x2.shape[1]
    dff = w_gate.shape[-1]
    xn = _rmsnorm(x2, gain)
    tn = _pick(dff, (256, 128))
    h = _matmul([xn], [(0, w_gate, (idx,)), (0, w_up, (idx,))], epilogue=_ep_swiglu, out_dtype=BF16, tn=tn,
                name="ffn_up")
    tk = None
    for cand in (dff // 2, dff // 4):
        if cand % LANE == 0 and cand * 2 <= dff:
            tk = cand
            break
    return _matmul([h], [(0, w_down, (idx,))], epilogue=_ep_residual, out_dtype=F32, tn=_pick(d, (256, 128)), tk=tk,
                   extras=[(x2, "mn", 0)], name="ffn_down")


def _router_body(x_ref, g_ref, w_ref, xn_ref, meta_ref, cnt_ref, run_ref, *, n_experts):
    i = pl.program_id(0)

    @pl.when(i == 0)
    def _():
        run_ref[...] = jnp.zeros_like(run_ref)

    x = x_ref[...]
    tm = x.shape[0]
    ms = jnp.mean(x * x, axis=-1, keepdims=True)
    xn = x * lax.rsqrt(ms + EPS) * g_ref[...]
    xn_ref[...] = xn.astype(xn_ref.dtype)
    logits = jnp.dot(xn, w_ref[...], preferred_element_type=F32, precision=lax.Precision.HIGHEST)
    lane = lax.broadcasted_iota(I32, logits.shape, 1).astype(F32)
    lg = jnp.where(lane < n_experts, logits, -jnp.inf)
    m1 = jnp.max(lg, axis=1, keepdims=True)
    i1 = jnp.min(jnp.where(lg == m1, lane, float(LANE)), axis=1, keepdims=True)
    lg2 = jnp.where(lane == i1, -jnp.inf, lg)
    m2 = jnp.max(lg2, axis=1, keepdims=True)
    i2 = jnp.min(jnp.where(lg2 == m2, lane, float(LANE)), axis=1, keepdims=True)
    e2 = jnp.exp(m2 - m1)
    den = 1.0 + e2
    oh1 = jnp.where(lane == i1, 1.0, 0.0)
    oh2 = jnp.where(lane == i2, 1.0, 0.0)
    oh = oh1 + oh2
    r = lax.broadcasted_iota(I32, (tm, tm), 0)
    c = lax.broadcasted_iota(I32, (tm, tm), 1)
    tri = jnp.where(c < r, 1.0, 0.0).astype(BF16)
    base = run_ref[0:1, :]
    rank = jnp.dot(tri, oh.astype(BF16), preferred_element_type=F32) + base
    r1 = jnp.sum(oh1 * rank, axis=1, keepdims=True)
    r2 = jnp.sum(oh2 * rank, axis=1, keepdims=True)
    total = base + jnp.sum(oh, axis=0, keepdims=True)
    run_ref[0:1, :] = total
    cnt_ref[...] = jnp.broadcast_to(total, cnt_ref.shape)
    meta = jnp.zeros_like(logits)
    for k, val in enumerate((i1, i2, r1, r2, 1.0 / den, e2 / den)):
        meta = jnp.where(lane == float(k), val, meta)
    meta_ref[...] = meta


def _gather_rows_body(src_ref, x_hbm, o_ref, sem, *, rows):
    base = pl.program_id(0) * rows

    def issue(r, carry):
        pltpu.make_async_copy(x_hbm.at[src_ref[base + r]], o_ref.at[r], sem).start()
        return carry

    lax.fori_loop(0, rows, issue, 0)
    pltpu.make_async_copy(x_hbm.at[pl.ds(0, rows)], o_ref, sem).wait()


def _gather_rows(x, src, *, rows):
    n_out = src.shape[0]
    n, d = x.shape
    x3 = x.reshape(n, d // LANE, LANE)
    out = pl.pallas_call(
        functools.partial(_gather_rows_body, rows=rows),
        grid_spec=pltpu.PrefetchScalarGridSpec(
            num_scalar_prefetch=1,
            grid=(n_out // rows,),
            in_specs=[pl.BlockSpec(memory_space=pl.ANY)],
            out_specs=pl.BlockSpec((rows, d // LANE, LANE), lambda i, src_ref: (i, 0, 0)),
            scratch_shapes=[pltpu.SemaphoreType.DMA(())],
        ),
        out_shape=jax.ShapeDtypeStruct((n_out, d // LANE, LANE), x.dtype),
        compiler_params=_params(("arbitrary",)),
        name="moe_gather",
    )(src, x3)
    return out.reshape(n_out, d)


def _experts_body(tile_ref, wexp_ref, lo_ref, hi_ref, first_ref, xs_ref, w1_ref, w3_ref, w2_ref, o_ref, *, tm):
    v = pl.program_id(0)
    f = pl.program_id(1)

    @pl.when((f == 0) & (first_ref[v] == 1))
    def _():
        o_ref[...] = jnp.zeros_like(o_ref)

    lo = lo_ref[v]
    hi = hi_ref[v]

    @pl.when(hi > lo)
    def _():
        xs = xs_ref[...]
        g = jnp.dot(xs, w1_ref[...], preferred_element_type=F32)
        u = jnp.dot(xs, w3_ref[...], preferred_element_type=F32)
        row = tile_ref[v] * tm + lax.broadcasted_iota(I32, (tm, 1), 0)
        h = jnp.where((row >= lo) & (row < hi), g * _sigmoid(g) * u, 0.0)
        o_ref[...] += jnp.dot(h.astype(BF16), w2_ref[...], preferred_element_type=F32)


def _combine_body(d1_ref, d2_ref, y_hbm, x_ref, w1_ref, w2_ref, o_ref, b1_ref, b2_ref, sem, *, rows):
    base = pl.program_id(0) * rows

    def issue(r, carry):
        pltpu.make_async_copy(y_hbm.at[d1_ref[base + r]], b1_ref.at[r], sem.at[0]).start()
        pltpu.make_async_copy(y_hbm.at[d2_ref[base + r]], b2_ref.at[r], sem.at[1]).start()
        return carry

    lax.fori_loop(0, rows, issue, 0)
    pltpu.make_async_copy(y_hbm.at[pl.ds(0, rows)], b1_ref, sem.at[0]).wait()
    pltpu.make_async_copy(y_hbm.at[pl.ds(0, rows)], b2_ref, sem.at[1]).wait()
    o_ref[...] = x_ref[...] + w1_ref[...] * b1_ref[...] + w2_ref[...] * b2_ref[...]


def _moe_ffn(x2, gain, w_router, w1, w3, w2, idx):
    n, d = x2.shape
    n_experts = w_router.shape[-1]
    dff = w1.shape[-1]
    tr = _pick(n, (256, 128, 64, 32, 16, 8))
    wr = _pad_cols(w_router[idx].astype(F32), LANE)
    xn, meta, cnt = pl.pallas_call(
        functools.partial(_router_body, n_experts=n_experts),
        grid=(n // tr,),
        in_specs=[pl.BlockSpec((tr, d), lambda i: (i, 0)), pl.BlockSpec((1, d), lambda i: (0, 0)),
                  pl.BlockSpec((d, LANE), lambda i: (0, 0))],
        out_specs=[pl.BlockSpec((tr, d), lambda i: (i, 0)), pl.BlockSpec((tr, LANE), lambda i: (i, 0)),
                   pl.BlockSpec((SUBLANE, LANE), lambda i: (0, 0))],
        out_shape=[jax.ShapeDtypeStruct((n, d), BF16), jax.ShapeDtypeStruct((n, LANE), F32),
                   jax.ShapeDtypeStruct((SUBLANE, LANE), F32)],
        scratch_shapes=[pltpu.VMEM((SUBLANE, LANE), F32)],
        compiler_params=_params(("arbitrary",)),
        name="moe_router",
    )(x2, gain.reshape(1, d).astype(F32), wr)

    e1, e2 = meta[:, 0].astype(I32), meta[:, 1].astype(I32)
    counts = cnt[0, :n_experts].astype(I32)
    ends = jnp.cumsum(counts)
    off = ends - counts
    dest1 = off[e1] + meta[:, 2].astype(I32)
    dest2 = off[e2] + meta[:, 3].astype(I32)
    na = TOP_K_EXPERTS * n
    tok = jnp.arange(n, dtype=I32)
    src = jnp.zeros((na,), I32).at[dest1].set(tok).at[dest2].set(tok)

    xs = _gather_rows(xn, src, rows=_pick(na, (256, 128, 64, 32, 16, 8)))

    tm = _pick(na, (512, 256, 128, 64, 32, 16, 8))
    n_tiles = na // tm
    nv = n_tiles + n_experts - 1
    lo = jnp.sort(jnp.concatenate([jnp.arange(n_tiles, dtype=I32) * tm, ends[:-1]]))
    hi = jnp.concatenate([lo[1:], jnp.full((1,), na, I32)])
    tile = jnp.minimum(lo // tm, n_tiles - 1)
    expert = jnp.minimum(jnp.sum((lo[:, None] >= ends[None, :]).astype(I32), axis=1), n_experts - 1)
    vid = jnp.arange(nv, dtype=I32)
    last = lax.cummax(jnp.where(hi > lo, vid, -1))
    wexp = expert[jnp.maximum(last, 0)]
    first = jnp.concatenate([jnp.ones((1,), I32), (tile[1:] != tile[:-1]).astype(I32)])

    fc = _pick(dff, (256, 128))
    nf = dff // fc

    def up_map(v, f, tile_ref, wexp_ref, lo_ref, hi_ref, first_ref):
        return wexp_ref[v], 0, jnp.where(hi_ref[v] > lo_ref[v], f, nf - 1)

    def down_map(v, f, tile_ref, wexp_ref, lo_ref, hi_ref, first_ref):
        return wexp_ref[v], jnp.where(hi_ref[v] > lo_ref[v], f, nf - 1), 0

    def row_map(v, f, tile_ref, wexp_ref, lo_ref, hi_ref, first_ref):
        return tile_ref[v], 0

    ys = pl.pallas_call(
        functools.partial(_experts_body, tm=tm),
        grid_spec=pltpu.PrefetchScalarGridSpec(
            num_scalar_prefetch=5,
            grid=(nv, nf),
            in_specs=[pl.BlockSpec((tm, d), row_map), pl.BlockSpec((None, d, fc), up_map),
                      pl.BlockSpec((None, d, fc), up_map), pl.BlockSpec((None, fc, d), down_map)],
            out_specs=pl.BlockSpec((tm, d), row_map),
        ),
        out_shape=jax.ShapeDtypeStruct((na, d), F32),
        compiler_params=_params(("arbitrary", "arbitrary")),
        name="moe_experts",
    )(tile, wexp, lo, hi, first, xs, w1[idx].astype(BF16), w3[idx].astype(BF16), w2[idx].astype(BF16))

    rows = _pick(n, (256, 128, 64, 32, 16, 8))
    slab = (rows, d // LANE, LANE)
    slab_spec = pl.BlockSpec(slab, lambda i, d1, d2: (i, 0, 0))
    weight_spec = pl.BlockSpec((rows, 1, LANE), lambda i, d1, d2: (i, 0, 0))
    wt1 = jnp.broadcast_to(meta[:, 4][:, None, None], (n, 1, LANE))
    wt2 = jnp.broadcast_to(meta[:, 5][:, None, None], (n, 1, LANE))
    out = pl.pallas_call(
        functools.partial(_combine_body, rows=rows),
        grid_spec=pltpu.PrefetchScalarGridSpec(
            num_scalar_prefetch=2,
            grid=(n // rows,),
            in_specs=[pl.BlockSpec(memory_space=pl.ANY), slab_spec, weight_spec, weight_spec],
            out_specs=slab_spec,
            scratch_shapes=[pltpu.VMEM(slab, F32), pltpu.VMEM(slab, F32), pltpu.SemaphoreType.DMA((2,))],
        ),
        out_shape=jax.ShapeDtypeStruct((n, d // LANE, LANE), F32),
        compiler_params=_params(("arbitrary",)),
        name="moe_combine",
    )(dest1, dest2, ys.reshape(na, d // LANE, LANE), x2.reshape(n, d // LANE, LANE), wt1, wt2)
    return out.reshape(n, d)


def kernel(x, positions, mix_norm, w_in, mla_q_lat_norm, mla_kv_lat_norm, mla_w_uq, mla_w_ukv, mla_q_norm, mla_k_norm, dsa_q_norm, dsa_k_norm, ssm_lam_re, ssm_lam_im, ssm_log_dt, ssm_b_re, ssm_b_im, ssm_c_re, ssm_c_im, ssm_d, ssm_w_glu, w_branch, w_out, ffn_norm, dense_w_gate, dense_w_up, dense_w_down, moe_w_router, moe_w1, moe_w3, moe_w2):
    bsz, seq, d = x.shape
    depth = mix_norm.shape[0]
    x2 = x.reshape(bsz * seq, d)
    cos, sin = _rope_tables(positions)
    for layer in range(depth):
        x2 = _hybrid_mixer(x2, cos, sin, bsz, seq, mix_norm[layer], w_in[layer], mla_q_lat_norm[layer],
                           mla_kv_lat_norm[layer], mla_w_uq[layer], mla_w_ukv[layer], mla_q_norm[layer],
                           mla_k_norm[layer], dsa_q_norm[layer], dsa_k_norm[layer], ssm_lam_re[layer],
                           ssm_lam_im[layer], ssm_log_dt[layer], ssm_b_re[layer], ssm_b_im[layer], ssm_c_re[layer],
                           ssm_c_im[layer], ssm_d[layer], ssm_w_glu[layer], w_branch[layer], w_out, layer)
        i = layer // 2
        if layer % 2 == 0:
            x2 = _dense_ffn(x2, ffn_norm[layer], dense_w_gate, dense_w_up, dense_w_down, i)
        else:
            x2 = _moe_ffn(x2, ffn_norm[layer], moe_w_router, moe_w1, moe_w3, moe_w2, i)
    return x2.reshape(bsz, seq, d)
```

```python
import functools
import math

import jax
import jax.numpy as jnp
from jax import lax
from jax.experimental import pallas as pl
from jax.experimental.pallas import tpu as pltpu

F32 = jnp.float32
BF16 = jnp.bfloat16
I32 = jnp.int32

SSM_GROUP = 16
SSM_STATE = 64
MLA_HEADS = 12
MLA_NOPE = 128
MLA_ROPE = 64
MLA_V = 128
DSA_HEADS = 12
DSA_HEAD_DIM = 128
IDX_HEADS = 16
IDX_DIM = 64
DSA_TOPK_MAX = 256
TOP_K_EXPERTS = 2
ROPE_THETA = 10000.0
EPS = 1e-6

LANE = 128
SUBLANE = 8
VMEM_LIMIT_BYTES = 56 * 1024 * 1024
NEG_BIG = -1e30
INT_MIN = -(2 ** 31)
LOG2E = 1.4426950408889634


def _params(semantics):
    return pltpu.CompilerParams(dimension_semantics=semantics, vmem_limit_bytes=VMEM_LIMIT_BYTES)


def _sigmoid(x):
    return 1.0 / (1.0 + jnp.exp(-x))


def _pick(n, candidates):
    for c in candidates:
        if n % c == 0:
            return c
    return n


def _rmsnorm_body(x_ref, g_ref, o_ref):
    x = x_ref[...].astype(F32)
    ms = jnp.mean(x * x, axis=-1, keepdims=True)
    o_ref[...] = (x * lax.rsqrt(ms + EPS) * g_ref[...]).astype(o_ref.dtype)


def _rmsnorm(x, gain, *, width=None, col_block=0, out_dtype=BF16):
    m = x.shape[0]
    width = width or x.shape[1]
    tm = _pick(m, (512, 256, 128, 64, 32, 16, 8))
    return pl.pallas_call(
        _rmsnorm_body,
        grid=(m // tm,),
        in_specs=[pl.BlockSpec((tm, width), lambda i: (i, col_block)),
                  pl.BlockSpec((1, width), lambda i: (0, 0))],
        out_specs=pl.BlockSpec((tm, width), lambda i: (i, 0)),
        out_shape=jax.ShapeDtypeStruct((m, width), out_dtype),
        compiler_params=_params(("parallel",)),
        name="rmsnorm",
    )(x, gain.reshape(1, width).astype(F32))


def _mm_body(*refs, n_a, pairs, n_extra, epilogue):
    a_refs = refs[:n_a]
    w_refs = refs[n_a:n_a + len(pairs)]
    e_refs = refs[n_a + len(pairs):n_a + len(pairs) + n_extra]
    o_ref = refs[n_a + len(pairs) + n_extra]
    a_vals = [a[...].astype(BF16) for a in a_refs]
    parts = [jnp.dot(a_vals[ai], w[...].astype(BF16), preferred_element_type=F32)
             for ai, w in zip(pairs, w_refs)]
    o_ref[...] = epilogue(parts, *[e[...] for e in e_refs]).astype(o_ref.dtype)


def _matmul(a_list, w_list, *, epilogue, out_dtype, tn, tm=None, k_slice=None, extras=(), name="matmul"):
    m = a_list[0].shape[0]
    n = w_list[0][1].shape[-1]
    tm = tm or _pick(m, (1024, 512, 256, 128, 64, 32, 16, 8))
    ksize, kidx = k_slice if k_slice else (None, 0)
    in_specs = []
    for a in a_list:
        in_specs.append(pl.BlockSpec((tm, ksize or a.shape[1]), lambda i, j: (i, kidx)))
    for _, w, prefix in w_list:
        in_specs.append(pl.BlockSpec((None,) * len(prefix) + (ksize or w.shape[-2], tn),
                                     lambda i, j, prefix=prefix: tuple(prefix) + (kidx, j)))
    for arr, kind, arg in extras:
        if kind == "mn":
            in_specs.append(pl.BlockSpec((tm, tn), lambda i, j, arg=arg: (i, j + arg)))
        else:
            in_specs.append(pl.BlockSpec((1, tn), lambda i, j: (0, j)))
    pairs = tuple(ai for ai, _, _ in w_list)
    body = functools.partial(_mm_body, n_a=len(a_list), pairs=pairs, n_extra=len(extras), epilogue=epilogue)
    return pl.pallas_call(
        body,
        grid=(m // tm, n // tn),
        in_specs=in_specs,
        out_specs=pl.BlockSpec((tm, tn), lambda i, j: (i, j)),
        out_shape=jax.ShapeDtypeStruct((m, n), out_dtype),
        compiler_params=_params(("parallel", "arbitrary")),
        name=name,
    )(*a_list, *[w for _, w, _ in w_list], *[arr for arr, _, _ in extras])


def _ep_plain(ps):
    return ps[0]


def _ep_residual(ps, res):
    return res + ps[0]


def _ep_swiglu(ps):
    g = ps[0]
    return g * _sigmoid(g) * ps[1]


def _ep_glu(ps):
    return ps[0] * _sigmoid(ps[1])


def _ep_branches(ps, g0, g1, g2):
    return _sigmoid(g0) * ps[0] + _sigmoid(g1) * ps[1] + _sigmoid(g2) * ps[2]


def _s5_scan_body(bre_ref, bim_ref, a_ref, o_re_ref, o_im_ref, xr_ref, xi_ref, car_ref, *, steps, seg_len):
    t = pl.program_id(2)
    lw = bre_ref.shape[-1]

    @pl.when(t == 0)
    def _():
        car_ref[...] = jnp.zeros_like(car_ref)

    ar = jnp.broadcast_to(a_ref[0:1, :], (SUBLANE, lw))
    ai = jnp.broadcast_to(a_ref[1:2, :], (SUBLANE, lw))

    def pass1(j, carry):
        xr, xi = carry
        off = pl.multiple_of(j * SUBLANE, SUBLANE)
        nr = ar * xr - ai * xi + bre_ref[0, pl.ds(off, SUBLANE), :]
        ni = ar * xi + ai * xr + bim_ref[0, pl.ds(off, SUBLANE), :]
        xr_ref[pl.ds(off, SUBLANE), :] = nr
        xi_ref[pl.ds(off, SUBLANE), :] = ni
        return nr, ni

    zero = jnp.zeros((SUBLANE, lw), F32)
    er, ei = lax.fori_loop(0, steps, pass1, (zero, zero))

    pr = a_ref[2:3, :]
    pi = a_ref[3:4, :]
    cr = car_ref[0:1, :]
    ci = car_ref[1:2, :]
    rows_r, rows_i = [], []
    for s in range(SUBLANE):
        rows_r.append(cr)
        rows_i.append(ci)
        nr = pr * cr - pi * ci + er[s:s + 1, :]
        ni = pr * ci + pi * cr + ei[s:s + 1, :]
        cr, ci = nr, ni
    car_ref[0:1, :] = cr
    car_ref[1:2, :] = ci
    cin_r = jnp.concatenate(rows_r, axis=0)
    cin_i = jnp.concatenate(rows_i, axis=0)

    def pass2(j, carry):
        cr_, ci_ = carry
        off = pl.multiple_of(j * SUBLANE, SUBLANE)
        nr = ar * cr_ - ai * ci_
        ni = ar * ci_ + ai * cr_
        o_re_ref[0, pl.ds(off, SUBLANE), :] = (xr_ref[pl.ds(off, SUBLANE), :] + nr).astype(o_re_ref.dtype)
        o_im_ref[0, pl.ds(off, SUBLANE), :] = (xi_ref[pl.ds(off, SUBLANE), :] + ni).astype(o_im_ref.dtype)
        return nr, ni

    lax.fori_loop(0, steps, pass2, (cin_r, cin_i))


def _s5_scan(bu_re, bu_im, a_tab, *, chunk):
    b, s, w = bu_re.shape
    lw = _pick(w, (512, 256, 128))
    steps = chunk // SUBLANE
    body = functools.partial(_s5_scan_body, steps=steps, seg_len=steps)
    blk = pl.BlockSpec((1, chunk, lw), lambda bi, li, ti: (bi, ti, li))
    return pl.pallas_call(
        body,
        grid=(b, w // lw, s // chunk),
        in_specs=[blk, blk, pl.BlockSpec((4, lw), lambda bi, li, ti: (0, li))],
        out_specs=[blk, blk],
        out_shape=[jax.ShapeDtypeStruct((b, s, w), BF16)] * 2,
        scratch_shapes=[pltpu.VMEM((chunk, lw), F32), pltpu.VMEM((chunk, lw), F32), pltpu.VMEM((SUBLANE, lw), F32)],
        compiler_params=_params(("parallel", "parallel", "arbitrary")),
        name="s5_scan",
    )(bu_re, bu_im, a_tab)


def _s5_chunk(s):
    return _pick(s, (512, 256, 128, 64))


def _to_chunk_order(t, chunk):
    b, s, w = t.shape
    return t.reshape(b, s // chunk, SUBLANE, chunk // SUBLANE, w).transpose(0, 1, 3, 2, 4).reshape(b, s, w)


def _from_chunk_order(t, chunk):
    b, s, w = t.shape
    return t.reshape(b, s // chunk, chunk // SUBLANE, SUBLANE, w).transpose(0, 1, 3, 2, 4).reshape(b, s, w)


def _s5_discretise(lam_re, lam_im, log_dt, b_re, b_im, c_re, c_im, steps):
    g, p = lam_re.shape
    h = b_re.shape[-1]
    lr, li = lam_re.astype(F32), lam_im.astype(F32)
    dt = jnp.exp(log_dt.astype(F32))[:, None]
    mag = jnp.exp(lr * dt)
    ab_re, ab_im = mag * jnp.cos(li * dt), mag * jnp.sin(li * dt)
    den = lr * lr + li * li
    nr = ab_re - 1.0
    zr = (nr * lr + ab_im * li) / den
    zi = (ab_im * lr - nr * li) / den
    bb_re = zr[..., None] * b_re - zi[..., None] * b_im
    bb_im = zr[..., None] * b_im + zi[..., None] * b_re
    gs = LANE // h
    ns = g // gs
    eye = jnp.eye(gs, dtype=F32)
    slab_in = lambda t: jnp.einsum("sgph,gk->sghkp", t.reshape(ns, gs, p, h), eye).reshape(ns, gs * h, gs * p)
    slab_out = lambda t: jnp.einsum("sghp,gk->sgpkh", t.reshape(ns, gs, h, p), eye).reshape(ns, gs * p, gs * h)
    pr, pi = ab_re, ab_im
    for _ in range(int(round(math.log2(steps)))):
        pr, pi = pr * pr - pi * pi, 2.0 * pr * pi
    a_tab = jnp.stack([ab_re.reshape(-1), ab_im.reshape(-1), pr.reshape(-1), pi.reshape(-1)])
    return (slab_in(bb_re).astype(BF16), slab_in(bb_im).astype(BF16), slab_out(c_re.astype(F32)).astype(BF16),
            slab_out(-c_im.astype(F32)).astype(BF16), a_tab)


def _s5_in_body(u_ref, wre_ref, wim_ref, re_ref, im_ref):
    u = u_ref[...].astype(BF16)
    re_ref[...] = jnp.dot(u, wre_ref[...], preferred_element_type=F32)
    im_ref[...] = jnp.dot(u, wim_ref[...], preferred_element_type=F32)


def _s5_out_body(xre_ref, xim_ref, cre_ref, cim_ref, u_ref, d_ref, y_ref):
    y = (jnp.dot(xre_ref[...], cre_ref[...], preferred_element_type=F32)
         + jnp.dot(xim_ref[...], cim_ref[...], preferred_element_type=F32) + d_ref[...] * u_ref[...])
    y_ref[...] = jax.nn.gelu(y).astype(y_ref.dtype)


def _s5_mixer(u, bsz, seq, lam_re, lam_im, log_dt, b_re, b_im, c_re, c_im, d_skip, w_glu):
    n, w = u.shape
    chunk = _s5_chunk(seq)
    steps = chunk // SUBLANE
    bin_re, bin_im, bout_re, bout_im, a_tab = _s5_discretise(lam_re, lam_im, log_dt, b_re, b_im, c_re, c_im, steps)
    ns, _, sw = bin_re.shape
    wst = ns * sw
    u_p = _to_chunk_order(u.reshape(bsz, seq, w), chunk).reshape(n, w)
    tm = _pick(n, (1024, 512, 256, 128, 64, 32, 16, 8))
    lanes = pl.BlockSpec((tm, LANE), lambda i, s: (i, s))
    states = pl.BlockSpec((tm, sw), lambda i, s: (i, s))
    bu_re, bu_im = pl.pallas_call(
        _s5_in_body,
        grid=(n // tm, ns),
        in_specs=[lanes, pl.BlockSpec((None, LANE, sw), lambda i, s: (s, 0, 0)),
                  pl.BlockSpec((None, LANE, sw), lambda i, s: (s, 0, 0))],
        out_specs=[states, states],
        out_shape=[jax.ShapeDtypeStruct((n, wst), F32)] * 2,
        compiler_params=_params(("parallel", "parallel")),
        name="s5_in",
    )(u_p, bin_re, bin_im)
    x_re, x_im = _s5_scan(bu_re.reshape(bsz, seq, wst), bu_im.reshape(bsz, seq, wst), a_tab, chunk=chunk)
    y = pl.pallas_call(
        _s5_out_body,
        grid=(n // tm, ns),
        in_specs=[states, states, pl.BlockSpec((None, sw, LANE), lambda i, s: (s, 0, 0)),
                  pl.BlockSpec((None, sw, LANE), lambda i, s: (s, 0, 0)), lanes,
                  pl.BlockSpec((1, LANE), lambda i, s: (0, s))],
        out_specs=lanes,
        out_shape=jax.ShapeDtypeStruct((n, w), BF16),
        compiler_params=_params(("parallel", "parallel")),
        name="s5_out",
    )(x_re.reshape(n, wst), x_im.reshape(n, wst), bout_re, bout_im, u_p, d_skip.reshape(1, w).astype(F32))
    tnw = _pick(w, (512, 256, 128))
    w_a = w_glu[:, :w].astype(BF16)
    w_g = w_glu[:, w:].astype(BF16)
    out = _matmul([y], [(0, w_a, ()), (0, w_g, ())], epilogue=_ep_glu, out_dtype=BF16, tn=tnw, name="s5_glu")
    return _from_chunk_order(out.reshape(bsz, seq, w), chunk).reshape(n, w)


def _rope_table_body(pos_ref, inv_ref, cos_ref, sin_ref):
    ang = pos_ref[...].astype(F32) * inv_ref[...]
    cos_ref[...] = jnp.cos(ang)
    sin_ref[...] = jnp.sin(ang)


def _rope_tables(positions):
    n = positions.size
    half = MLA_ROPE // 2
    inv = ROPE_THETA ** (-jnp.arange(half, dtype=F32) / half)
    inv = jnp.zeros((1, LANE), F32).at[0, :half].set(inv)
    tm = _pick(n, (512, 256, 128, 64, 32, 16, 8))
    return pl.pallas_call(
        _rope_table_body,
        grid=(n // tm,),
        in_specs=[pl.BlockSpec((tm, 1), lambda i: (i, 0)), pl.BlockSpec((1, LANE), lambda i: (0, 0))],
        out_specs=[pl.BlockSpec((tm, LANE), lambda i: (i, 0))] * 2,
        out_shape=[jax.ShapeDtypeStruct((n, LANE), F32)] * 2,
        compiler_params=_params(("parallel",)),
        name="rope_table",
    )(positions.reshape(n, 1).astype(I32), inv)


def _head_norm_rope(t0, t1, t2, g0, g1, g2, cos, sin, true_dim):
    ss = (jnp.sum(t0 * t0, axis=-1, keepdims=True) + jnp.sum(t1 * t1, axis=-1, keepdims=True)
          + jnp.sum(t2 * t2, axis=-1, keepdims=True))
    inv = lax.rsqrt(ss * (1.0 / true_dim) + EPS)
    n0 = t0 * inv * g0
    n1 = t1 * inv * g1
    n2 = t2 * inv * g2
    o1 = n1 * cos - n2 * sin
    o2 = n1 * sin + n2 * cos
    half = MLA_ROPE // 2
    lane = lax.broadcasted_iota(I32, o1.shape, 1)
    o1 = jnp.where(lane < half, o1, 0.0)
    o2 = jnp.where(lane < half, o2, 0.0)
    return jnp.concatenate([n0, o1 + pltpu.roll(o2, half, 1)], axis=-1)


def _mla_q_prep_body(q_ref, g_ref, cos_ref, sin_ref, o_ref):
    cos, sin = cos_ref[...], sin_ref[...]
    hw = 3 * LANE
    g0, g1, g2 = g_ref[:, 0:LANE], g_ref[:, LANE:2 * LANE], g_ref[:, 2 * LANE:hw]
    for h in range(MLA_HEADS):
        t0 = q_ref[:, h * hw:h * hw + LANE]
        t1 = q_ref[:, h * hw + LANE:h * hw + 2 * LANE]
        t2 = q_ref[:, h * hw + 2 * LANE:(h + 1) * hw]
        qh = _head_norm_rope(t0, t1, t2, g0, g1, g2, cos, sin, MLA_NOPE + MLA_ROPE)
        o_ref[0, h] = (qh * ((MLA_NOPE + MLA_ROPE) ** -0.5 * LOG2E)).astype(o_ref.dtype)


def _mla_kv_prep_body(kv_ref, pe_ref, g_ref, cos_ref, sin_ref, k_ref, v_ref):
    cos, sin = cos_ref[...], sin_ref[...]
    hw = 3 * LANE
    g0, g1, g2 = g_ref[:, 0:LANE], g_ref[:, LANE:2 * LANE], g_ref[:, 2 * LANE:hw]
    t1 = pe_ref[:, 0:LANE]
    t2 = pe_ref[:, LANE:2 * LANE]
    kvw = MLA_NOPE + MLA_V
    for h in range(MLA_HEADS):
        t0 = kv_ref[:, h * kvw:h * kvw + MLA_NOPE]
        k_ref[0, h] = _head_norm_rope(t0, t1, t2, g0, g1, g2, cos, sin, MLA_NOPE + MLA_ROPE).astype(k_ref.dtype)
        v_ref[0, h] = kv_ref[:, h * kvw + MLA_NOPE:(h + 1) * kvw].astype(v_ref.dtype)


def _pad_rope_gain(gain):
    half = MLA_ROPE // 2
    out = jnp.zeros((3 * LANE,), F32)
    out = out.at[:MLA_NOPE].set(gain[:MLA_NOPE].astype(F32))
    out = out.at[LANE:LANE + half].set(gain[MLA_NOPE:MLA_NOPE + half].astype(F32))
    out = out.at[2 * LANE:2 * LANE + half].set(gain[MLA_NOPE + half:].astype(F32))
    return out.reshape(1, 3 * LANE)


def _flash_body(q_ref, k_ref, v_ref, o_ref, *, tq, wide):
    qi = pl.program_id(2)
    dv = v_ref.shape[-1]
    q = q_ref[0, 0]

    def step(off, width, carry, masked):
        m, l, acc = carry
        k = k_ref[0, 0, pl.ds(off, width), :]
        v = v_ref[0, 0, pl.ds(off, width), :]
        s = lax.dot_general(q, k, (((1,), (1,)), ((), ())), preferred_element_type=F32)
        if masked:
            row = lax.broadcasted_iota(I32, (tq, width), 0)
            col = lax.broadcasted_iota(I32, (tq, width), 1)
            s = jnp.where(col <= row, s, NEG_BIG)
        m_new = jnp.maximum(m, jnp.max(s, axis=1, keepdims=True))
        alpha = jnp.exp2(m - m_new)
        p = jnp.exp2(s - m_new)
        l = alpha * l + jnp.sum(p, axis=1, keepdims=True)
        acc = alpha * acc + jnp.dot(p.astype(BF16), v, preferred_element_type=F32)
        return m_new, l, acc

    per = wide // tq
    n_wide = qi // per
    carry = (jnp.full((tq, 1), NEG_BIG, F32), jnp.zeros((tq, 1), F32), jnp.zeros((tq, dv), F32))
    carry = lax.fori_loop(0, n_wide, lambda j, c: step(pl.multiple_of(j * wide, wide), wide, c, False), carry)
    carry = lax.fori_loop(n_wide * per, qi, lambda j, c: step(pl.multiple_of(j * tq, tq), tq, c, False), carry)
    m, l, acc = step(pl.multiple_of(qi * tq, tq), tq, carry, True)
    o_ref[0] = (acc / l).astype(o_ref.dtype)


def _flash_attention(q, k, v):
    b, h, s, dk = q.shape
    dv = v.shape[-1]
    tq = _pick(s, (1024, 512, 256, 128))
    wide = _pick(s, (2 * tq, tq))
    return pl.pallas_call(
        functools.partial(_flash_body, tq=tq, wide=wide),
        grid=(b, h, s // tq),
        in_specs=[pl.BlockSpec((1, 1, tq, dk), lambda bi, hi, qi: (bi, hi, qi, 0)),
                  pl.BlockSpec((1, 1, s, dk), lambda bi, hi, qi: (bi, hi, 0, 0)),
                  pl.BlockSpec((1, 1, s, dv), lambda bi, hi, qi: (bi, hi, 0, 0))],
        out_specs=pl.BlockSpec((1, tq, dv), lambda bi, hi, qi: (bi, qi, hi)),
        out_shape=jax.ShapeDtypeStruct((b, s, h * dv), BF16),
        compiler_params=_params(("parallel", "parallel", "arbitrary")),
        name="mla_flash",
    )(q, k, v)


def _mla_mixer(hm, cos, sin, bsz, seq, q_lora, kv_lora, q_lat_gain, kv_lat_gain, w_uq, w_ukv, q_gain, k_gain):
    n = hm.shape[0]
    half = MLA_ROPE // 2
    qk = MLA_NOPE + MLA_ROPE
    cqn = _rmsnorm(hm, q_lat_gain, width=q_lora, col_block=0)
    ckvn = _rmsnorm(hm, kv_lat_gain, width=kv_lora, col_block=q_lora // kv_lora)
    wq = w_uq.reshape(q_lora, MLA_HEADS, qk)
    wq_p = jnp.zeros((q_lora, MLA_HEADS, 3 * LANE), BF16)
    wq_p = wq_p.at[:, :, :MLA_NOPE].set(wq[:, :, :MLA_NOPE].astype(BF16))
    wq_p = wq_p.at[:, :, LANE:LANE + half].set(wq[:, :, MLA_NOPE:MLA_NOPE + half].astype(BF16))
    wq_p = wq_p.at[:, :, 2 * LANE:2 * LANE + half].set(wq[:, :, MLA_NOPE + half:].astype(BF16))
    wq_p = wq_p.reshape(q_lora, MLA_HEADS * 3 * LANE)
    q_raw = _matmul([cqn], [(0, wq_p, ())], epilogue=_ep_plain, out_dtype=F32, tn=3 * LANE, name="mla_uq")
    kv_raw = _matmul([ckvn], [(0, w_ukv.astype(BF16), ())], epilogue=_ep_plain, out_dtype=F32,
                     tn=MLA_NOPE + MLA_V, name="mla_ukv")

    tm = _pick(seq, (256, 128, 64, 32, 16))
    nblk = seq // tm
    qg = _pad_rope_gain(q_gain)
    kg = _pad_rope_gain(k_gain)
    row = lambda bi, si: (bi * nblk + si, 0)
    q = pl.pallas_call(
        _mla_q_prep_body,
        grid=(bsz, nblk),
        in_specs=[pl.BlockSpec((tm, MLA_HEADS * 3 * LANE), row), pl.BlockSpec((1, 3 * LANE), lambda bi, si: (0, 0)),
                  pl.BlockSpec((tm, LANE), row), pl.BlockSpec((tm, LANE), row)],
        out_specs=pl.BlockSpec((1, MLA_HEADS, tm, 2 * LANE), lambda bi, si: (bi, 0, si, 0)),
        out_shape=jax.ShapeDtypeStruct((bsz, MLA_HEADS, seq, 2 * LANE), BF16),
        compiler_params=_params(("parallel", "parallel")),
        name="mla_q_prep",
    )(q_raw, qg, cos, sin)
    pe_block = (q_lora + kv_lora) // (2 * LANE)
    k, v = pl.pallas_call(
        _mla_kv_prep_body,
        grid=(bsz, nblk),
        in_specs=[pl.BlockSpec((tm, MLA_HEADS * (MLA_NOPE + MLA_V)), row),
                  pl.BlockSpec((tm, 2 * LANE), lambda bi, si: (bi * nblk + si, pe_block)),
                  pl.BlockSpec((1, 3 * LANE), lambda bi, si: (0, 0)),
                  pl.BlockSpec((tm, LANE), row), pl.BlockSpec((tm, LANE), row)],
        out_specs=[pl.BlockSpec((1, MLA_HEADS, tm, 2 * LANE), lambda bi, si: (bi, 0, si, 0)),
                   pl.BlockSpec((1, MLA_HEADS, tm, MLA_V), lambda bi, si: (bi, 0, si, 0))],
        out_shape=[jax.ShapeDtypeStruct((bsz, MLA_HEADS, seq, 2 * LANE), BF16),
                   jax.ShapeDtypeStruct((bsz, MLA_HEADS, seq, MLA_V), BF16)],
        compiler_params=_params(("parallel", "parallel")),
        name="mla_kv_prep",
    )(kv_raw, hm, kg, cos, sin)
    return _flash_attention(q, k, v).reshape(n, MLA_HEADS * MLA_V)


def _dsa_prep_body(hd_q_ref, hd_k_ref, hd_v_ref, hi_q_ref, hi_k_ref, hi_w_ref, qg_ref, kg_ref,
                   q_ref, k_ref, v_ref, qi_ref, klo_ref, khi_ref, w_ref):
    def norm(t, g):
        ms = jnp.mean(t * t, axis=-1, keepdims=True)
        return t * lax.rsqrt(ms + EPS) * g

    qg = qg_ref[...]
    for h in range(DSA_HEADS):
        qh = norm(hd_q_ref[:, h * DSA_HEAD_DIM:(h + 1) * DSA_HEAD_DIM], qg)
        q_ref[0, h] = (qh * (DSA_HEAD_DIM ** -0.5 * LOG2E)).astype(q_ref.dtype)
    k_ref[0] = norm(hd_k_ref[...], kg_ref[...]).astype(k_ref.dtype)
    v_ref[0] = hd_v_ref[...].astype(v_ref.dtype)
    qi_ref[0] = hi_q_ref[...].astype(qi_ref.dtype)
    ki = hi_k_ref[...]
    lane = lax.broadcasted_iota(I32, ki.shape, 1)
    ki = jnp.where(lane < IDX_DIM, ki, 0.0)
    klo_ref[0] = ki.astype(klo_ref.dtype)
    khi_ref[0] = pltpu.roll(ki, IDX_DIM, 1).astype(khi_ref.dtype)
    w_ref[0] = hi_w_ref[...] * ((IDX_HEADS ** -0.5) * (IDX_DIM ** -0.5))


def _float_key(x):
    bits = pltpu.bitcast(x, I32)
    return jnp.where(bits < 0, bits ^ jnp.int32(0x7FFFFFFF), bits)


def _dsa_body(q_ref, k_ref, v_ref, qi_ref, klo_ref, khi_ref, w_ref, o_ref, keys_ref, gmax_ref, acc_ref,
              *, tq, tk, wide, n_sel):
    qb = pl.program_id(1)
    n_kv = (qb * tq + tq + tk - 1) // tk
    row = lax.broadcasted_iota(I32, (tq, tk), 0) + qb * tq
    col0 = lax.broadcasted_iota(I32, (tq, tk), 1)

    w = w_ref[0]

    def score_tile(j, _):
        off = pl.multiple_of(j * tk, tk)
        klo = klo_ref[0, pl.ds(off, tk), :]
        khi = khi_ref[0, pl.ds(off, tk), :]
        acc = jnp.zeros((tq, tk), F32)
        for hp in range(IDX_HEADS // 2):
            q2 = qi_ref[0, :, hp * LANE:(hp + 1) * LANE]
            for half, kk in ((0, klo), (1, khi)):
                h = 2 * hp + half
                logit = lax.dot_general(q2, kk, (((1,), (1,)), ((), ())), preferred_element_type=F32)
                acc = acc + w[:, h:h + 1] * jnp.maximum(logit, 0.0)
        causal = col0 + off <= row
        keys_ref[:, pl.ds(off, tk)] = jnp.where(causal, _float_key(acc + 0.0), INT_MIN)
        sc = jnp.where(causal, acc, -jnp.inf)
        for c in range(tk // LANE):
            gmax_ref[c % 2] = jnp.maximum(gmax_ref[c % 2], sc[:, c * LANE:(c + 1) * LANE])
        return 0

    gmax_ref[...] = jnp.full(gmax_ref.shape, -jnp.inf, F32)
    lax.fori_loop(0, n_kv, score_tile, 0)

    rc = min(tq, 128)

    def row_group(g, carry):
        r0 = pl.multiple_of(g * rc, rc)
        ge = gmax_ref[0, pl.ds(r0, rc), :]
        go = gmax_ref[1, pl.ds(r0, rc), :]
        key_l = _float_key(jnp.min(jnp.minimum(ge, go), axis=1, keepdims=True) + 0.0)
        key_u = _float_key(jnp.max(jnp.maximum(ge, go), axis=1, keepdims=True) + 0.0)
        shared = lax.clz(key_l ^ key_u)
        start = jnp.min(shared)
        mask = jnp.where(start > 0, lax.shift_left(jnp.int32(-1), 32 - jnp.maximum(start, 1)), 0)
        lo_init = ((key_l ^ INT_MIN) & mask) ^ INT_MIN
        lo_init = jnp.broadcast_to(lo_init, (rc, LANE))

        def bit_step(b, lo):
            cand = lo + lax.shift_left(jnp.int32(1), 31 - b)

            def count_tile(j, part):
                off = pl.multiple_of(j * tk, tk)
                for c in range(tk // LANE):
                    kc = keys_ref[pl.ds(r0, rc), pl.ds(off + c * LANE, LANE)]
                    part = part + jnp.where(kc >= cand, 1.0, 0.0)
                return part

            part = lax.fori_loop(0, n_kv, count_tile, jnp.zeros((rc, LANE), F32))
            cnt = jnp.sum(part, axis=1, keepdims=True)
            return jnp.where(cnt >= n_sel, cand, lo)

        thr = lax.fori_loop(start, 32, bit_step, lo_init)
        thr = jnp.maximum(thr, INT_MIN + 1)

        def bias_tile(j, c2):
            off = pl.multiple_of(j * tk, tk)
            for c in range(tk // LANE):
                kc = keys_ref[pl.ds(r0, rc), pl.ds(off + c * LANE, LANE)]
                bias = jnp.where(kc >= thr, 0.0, NEG_BIG)
                keys_ref[pl.ds(r0, rc), pl.ds(off + c * LANE, LANE)] = pltpu.bitcast(bias, I32)
            return c2

        lax.fori_loop(0, n_kv, bias_tile, 0)
        return carry

    lax.fori_loop(0, tq // rc, row_group, 0)

    per = wide // tk
    n_wide = n_kv // per

    def head(h, carry):
        q = q_ref[0, h]

        def step(off, width, st):
            m, l, acc = st
            k = k_ref[0, pl.ds(off, width), :]
            v = v_ref[0, pl.ds(off, width), :]
            bias = pltpu.bitcast(keys_ref[:, pl.ds(off, width)], F32)
            s = lax.dot_general(q, k, (((1,), (1,)), ((), ())), preferred_element_type=F32) + bias
            m_new = jnp.maximum(m, jnp.max(s, axis=1, keepdims=True))
            alpha = jnp.exp2(m - m_new)
            p = jnp.exp2(s - m_new)
            l = alpha * l + jnp.sum(p, axis=1, keepdims=True)
            acc = alpha * acc + jnp.dot(p.astype(BF16), v, preferred_element_type=F32)
            return m_new, l, acc

        st = (jnp.full((tq, 1), NEG_BIG, F32), jnp.zeros((tq, 1), F32), jnp.zeros((tq, DSA_HEAD_DIM), F32))
        st = lax.fori_loop(0, n_wide, lambda j, c: step(pl.multiple_of(j * wide, wide), wide, c), st)
        st = lax.fori_loop(n_wide * per, n_kv, lambda j, c: step(pl.multiple_of(j * tk, tk), tk, c), st)
        m, l, acc = st
        acc_ref[h] = acc / l
        return carry

    lax.fori_loop(0, DSA_HEADS, head, 0)
    for h in range(DSA_HEADS):
        o_ref[0, :, h * DSA_HEAD_DIM:(h + 1) * DSA_HEAD_DIM] = acc_ref[h].astype(o_ref.dtype)


def _dsa_mixer(hd, hi, bsz, seq, q_gain, k_gain):
    n = hd.shape[0]
    qw = DSA_HEADS * DSA_HEAD_DIM
    iw = IDX_HEADS * IDX_DIM
    tm = _pick(seq, (256, 128, 64, 32, 16))
    nblk = seq // tm

    def cols(width, block):
        return pl.BlockSpec((tm, width), lambda bi, si: (bi * nblk + si, block))

    per_tok = lambda width: pl.BlockSpec((1, tm, width), lambda bi, si: (bi, si, 0))
    q, k, v, qi, klo, khi, w = pl.pallas_call(
        _dsa_prep_body,
        grid=(bsz, nblk),
        in_specs=[cols(qw, 0), cols(LANE, qw // LANE), cols(LANE, qw // LANE + 1),
                  cols(iw, 0), cols(LANE, iw // LANE), cols(LANE, iw // LANE + 1),
                  pl.BlockSpec((1, DSA_HEAD_DIM), lambda bi, si: (0, 0)),
                  pl.BlockSpec((1, DSA_HEAD_DIM), lambda bi, si: (0, 0))],
        out_specs=[pl.BlockSpec((1, DSA_HEADS, tm, DSA_HEAD_DIM), lambda bi, si: (bi, 0, si, 0)),
                   per_tok(DSA_HEAD_DIM), per_tok(DSA_HEAD_DIM), per_tok(iw), per_tok(LANE), per_tok(LANE),
                   per_tok(LANE)],
        out_shape=[jax.ShapeDtypeStruct((bsz, DSA_HEADS, seq, DSA_HEAD_DIM), BF16),
                   jax.ShapeDtypeStruct((bsz, seq, DSA_HEAD_DIM), BF16),
                   jax.ShapeDtypeStruct((bsz, seq, DSA_HEAD_DIM), BF16),
                   jax.ShapeDtypeStruct((bsz, seq, iw), BF16),
                   jax.ShapeDtypeStruct((bsz, seq, LANE), BF16),
                   jax.ShapeDtypeStruct((bsz, seq, LANE), BF16),
                   jax.ShapeDtypeStruct((bsz, seq, LANE), F32)],
        compiler_params=_params(("parallel", "parallel")),
        name="dsa_prep",
    )(hd, hd, hd, hi, hi, hi, q_gain.reshape(1, -1).astype(F32), k_gain.reshape(1, -1).astype(F32))

    tq = _pick(seq, (512, 256, 128))
    tk = _pick(seq, (512, 256, 128))
    n_sel = min(DSA_TOPK_MAX, seq // 4)
    assert n_sel <= 2 * LANE
    whole = lambda width: pl.BlockSpec((1, seq, width), lambda bi, qb: (bi, 0, 0), pipeline_mode=pl.Buffered(1))
    out = pl.pallas_call(
        functools.partial(_dsa_body, tq=tq, tk=tk, wide=_pick(seq, (4 * tk, 2 * tk, tk)), n_sel=n_sel),
        grid=(bsz, seq // tq),
        in_specs=[pl.BlockSpec((1, DSA_HEADS, tq, DSA_HEAD_DIM), lambda bi, qb: (bi, 0, qb, 0)),
                  whole(DSA_HEAD_DIM), whole(DSA_HEAD_DIM),
                  pl.BlockSpec((1, tq, iw), lambda bi, qb: (bi, qb, 0)),
                  whole(LANE), whole(LANE),
                  pl.BlockSpec((1, tq, LANE), lambda bi, qb: (bi, qb, 0))],
        out_specs=pl.BlockSpec((1, tq, qw), lambda bi, qb: (bi, qb, 0)),
        out_shape=jax.ShapeDtypeStruct((bsz, seq, qw), BF16),
        scratch_shapes=[pltpu.VMEM((tq, seq), I32), pltpu.VMEM((2, tq, LANE), F32),
                        pltpu.VMEM((DSA_HEADS, tq, DSA_HEAD_DIM), F32)],
        compiler_params=_params(("parallel", "arbitrary")),
        name="dsa_attention",
    )(q, k, v, qi, klo, khi, w)
    return out.reshape(n, qw)


def _pad_cols(w, width):
    return jnp.pad(w, ((0, 0), (0, width - w.shape[1])))


def _hybrid_mixer(x2, cos, sin, bsz, seq, norm_gain, w_in, q_lat_gain, kv_lat_gain, w_uq, w_ukv, mla_q_gain,
                  mla_k_gain, dsa_q_gain, dsa_k_gain, lam_re, lam_im, log_dt, b_re, b_im, c_re, c_im, d_skip, w_glu,
                  w_branch, w_out, layer):
    n, d = x2.shape
    ssm_w = d_skip.shape[-1]
    q_lora = q_lat_gain.shape[-1]
    kv_lora = kv_lat_gain.shape[-1]
    half = MLA_ROPE // 2
    dsa_w = DSA_HEADS * DSA_HEAD_DIM
    idx_w = IDX_HEADS * IDX_DIM
    sizes = (ssm_w, q_lora, kv_lora, MLA_ROPE, dsa_w, DSA_HEAD_DIM, DSA_HEAD_DIM, idx_w, IDX_DIM, IDX_HEADS, 3 * d)
    offs = [0]
    for s in sizes:
        offs.append(offs[-1] + s)
    seg = lambda i: w_in[:, offs[i]:offs[i + 1]]
    w_u = seg(0).astype(BF16)
    pe = seg(3)
    w_mla = jnp.concatenate([seg(1), seg(2), _pad_cols(pe[:, :half], LANE), _pad_cols(pe[:, half:], LANE)],
                            axis=1).astype(BF16)
    w_dsa = jnp.concatenate([seg(4), seg(5), seg(6)], axis=1).astype(BF16)
    w_idx = jnp.concatenate([seg(7), _pad_cols(seg(8), LANE), _pad_cols(seg(9), LANE)], axis=1).astype(BF16)
    w_gate = seg(10).astype(BF16)

    xn = _rmsnorm(x2, norm_gain)
    proj = lambda w, name: _matmul([xn], [(0, w, ())], epilogue=_ep_plain, out_dtype=F32,
                                   tn=_pick(w.shape[1], (256, 128)), name=name)
    u = proj(w_u, "in_ssm")
    hm = proj(w_mla, "in_mla")
    hd = proj(w_dsa, "in_dsa")
    hi = proj(w_idx, "in_idx")
    g = proj(w_gate, "in_gates")

    y_ssm = _s5_mixer(u, bsz, seq, lam_re, lam_im, log_dt, b_re, b_im, c_re, c_im, d_skip, w_glu)
    y_mla = _mla_mixer(hm, cos, sin, bsz, seq, q_lora, kv_lora, q_lat_gain, kv_lat_gain, w_uq, w_ukv,
                       mla_q_gain, mla_k_gain)
    y_dsa = _dsa_mixer(hd, hi, bsz, seq, dsa_q_gain, dsa_k_gain)

    br0 = ssm_w
    br1 = ssm_w + MLA_HEADS * MLA_V
    tn = _pick(d, (256, 128))
    merged = _matmul([y_ssm, y_mla, y_dsa],
                     [(0, w_branch[:br0].astype(BF16), ()), (1, w_branch[br0:br1].astype(BF16), ()),
                      (2, w_branch[br1:].astype(BF16), ())],
                     epilogue=_ep_branches, out_dtype=BF16, tn=tn,
                     extras=[(g, "mn", 0), (g, "mn", d // tn), (g, "mn", 2 * (d // tn))], name="branch_merge")
    return _matmul([merged], [(0, w_out, (layer,))], epilogue=_ep_residual, out_dtype=F32, tn=tn,
                   extras=[(x2, "mn", 0)], name="mixer_out")


def _dense_ffn(x2, gain, w_gate, w_up, w_down, idx):
    d = x2.shape[1]
    dff = w_gate.shape[-1]
    xn = _rmsnorm(x2, gain)
    tn = _pick(dff, (256, 128))
    h = _matmul([xn], [(0, w_gate, (idx,)), (0, w_up, (idx,))], epilogue=_ep_swiglu, out_dtype=BF16, tn=tn,
                name="ffn_up")
    tnd = _pick(d, (256, 128))
    n_slabs = 2 if dff % (2 * LANE) == 0 and dff > 8192 else 1
    out = x2
    for s in range(n_slabs):
        out = _matmul([h], [(0, w_down, (idx,))], epilogue=_ep_residual, out_dtype=F32, tn=tnd,
                      k_slice=(dff // n_slabs, s), extras=[(out, "mn", 0)], name="ffn_down")
    return out


def _router_body(x_ref, g_ref, w_ref, xn_ref, meta_ref, cnt_ref, run_ref, *, n_experts):
    i = pl.program_id(0)

    @pl.when(i == 0)
    def _():
        run_ref[...] = jnp.zeros_like(run_ref)

    x = x_ref[...]
    tm = x.shape[0]
    ms = jnp.mean(x * x, axis=-1, keepdims=True)
    xn = x * lax.rsqrt(ms + EPS) * g_ref[...]
    xn_ref[...] = xn.astype(xn_ref.dtype)
    logits = jnp.dot(xn, w_ref[...], preferred_element_type=F32, precision=lax.Precision.HIGHEST)
    lane = lax.broadcasted_iota(I32, logits.shape, 1).astype(F32)
    lg = jnp.where(lane < n_experts, logits, -jnp.inf)
    m1 = jnp.max(lg, axis=1, keepdims=True)
    i1 = jnp.min(jnp.where(lg == m1, lane, float(LANE)), axis=1, keepdims=True)
    lg2 = jnp.where(lane == i1, -jnp.inf, lg)
    m2 = jnp.max(lg2, axis=1, keepdims=True)
    i2 = jnp.min(jnp.where(lg2 == m2, lane, float(LANE)), axis=1, keepdims=True)
    e2 = jnp.exp(m2 - m1)
    den = 1.0 + e2
    oh1 = jnp.where(lane == i1, 1.0, 0.0)
    oh2 = jnp.where(lane == i2, 1.0, 0.0)
    oh = oh1 + oh2
    r = lax.broadcasted_iota(I32, (tm, tm), 0)
    c = lax.broadcasted_iota(I32, (tm, tm), 1)
    tri = jnp.where(c < r, 1.0, 0.0).astype(BF16)
    base = run_ref[0:1, :]
    rank = jnp.dot(tri, oh.astype(BF16), preferred_element_type=F32) + base
    r1 = jnp.sum(oh1 * rank, axis=1, keepdims=True)
    r2 = jnp.sum(oh2 * rank, axis=1, keepdims=True)
    total = base + jnp.sum(oh, axis=0, keepdims=True)
    run_ref[0:1, :] = total
    cnt_ref[...] = jnp.broadcast_to(total, cnt_ref.shape)
    meta = jnp.zeros_like(logits)
    for k, val in enumerate((i1, i2, r1, r2, 1.0 / den, e2 / den)):
        meta = jnp.where(lane == float(k), val, meta)
    meta_ref[...] = meta


def _gather_rows_body(src_ref, x_hbm, o_ref, sem, *, rows):
    base = pl.program_id(0) * rows

    def issue(r, carry):
        pltpu.make_async_copy(x_hbm.at[src_ref[base + r]], o_ref.at[r], sem).start()
        return carry

    lax.fori_loop(0, rows, issue, 0)
    pltpu.make_async_copy(x_hbm.at[pl.ds(0, rows)], o_ref, sem).wait()


def _gather_rows(x, src, *, rows):
    n_out = src.shape[0]
    n, d = x.shape
    x3 = x.reshape(n, d // LANE, LANE)
    out = pl.pallas_call(
        functools.partial(_gather_rows_body, rows=rows),
        grid_spec=pltpu.PrefetchScalarGridSpec(
            num_scalar_prefetch=1,
            grid=(n_out // rows,),
            in_specs=[pl.BlockSpec(memory_space=pl.ANY)],
            out_specs=pl.BlockSpec((rows, d // LANE, LANE), lambda i, src_ref: (i, 0, 0)),
            scratch_shapes=[pltpu.SemaphoreType.DMA(())],
        ),
        out_shape=jax.ShapeDtypeStruct((n_out, d // LANE, LANE), x.dtype),
        compiler_params=_params(("arbitrary",)),
        name="moe_gather",
    )(src, x3)
    return out.reshape(n_out, d)


def _experts_body(tile_ref, wexp_ref, lo_ref, hi_ref, first_ref, xs_ref, w1_ref, w3_ref, w2_ref, o_ref, *, tm):
    v = pl.program_id(0)
    f = pl.program_id(1)

    @pl.when((f == 0) & (first_ref[v] == 1))
    def _():
        o_ref[...] = jnp.zeros_like(o_ref)

    lo = lo_ref[v]
    hi = hi_ref[v]

    @pl.when(hi > lo)
    def _():
        xs = xs_ref[...]
        g = jnp.dot(xs, w1_ref[...], preferred_element_type=F32)
        u = jnp.dot(xs, w3_ref[...], preferred_element_type=F32)
        row = tile_ref[v] * tm + lax.broadcasted_iota(I32, (tm, 1), 0)
        h = jnp.where((row >= lo) & (row < hi), g * _sigmoid(g) * u, 0.0)
        o_ref[...] += jnp.dot(h.astype(BF16), w2_ref[...], preferred_element_type=F32)


def _combine_body(d1_ref, d2_ref, y_hbm, x_ref, w1_ref, w2_ref, o_ref, b1_ref, b2_ref, sem, *, rows):
    base = pl.program_id(0) * rows

    def issue(r, carry):
        pltpu.make_async_copy(y_hbm.at[d1_ref[base + r]], b1_ref.at[r], sem.at[0]).start()
        pltpu.make_async_copy(y_hbm.at[d2_ref[base + r]], b2_ref.at[r], sem.at[1]).start()
        return carry

    lax.fori_loop(0, rows, issue, 0)
    pltpu.make_async_copy(y_hbm.at[pl.ds(0, rows)], b1_ref, sem.at[0]).wait()
    pltpu.make_async_copy(y_hbm.at[pl.ds(0, rows)], b2_ref, sem.at[1]).wait()
    o_ref[...] = x_ref[...] + w1_ref[...] * b1_ref[...] + w2_ref[...] * b2_ref[...]


def _moe_ffn(x2, gain, w_router, w1, w3, w2, idx):
    n, d = x2.shape
    n_experts = w_router.shape[-1]
    dff = w1.shape[-1]
    tr = _pick(n, (256, 128, 64, 32, 16, 8))
    wr = _pad_cols(w_router[idx].astype(F32), LANE)
    xn, meta, cnt = pl.pallas_call(
        functools.partial(_router_body, n_experts=n_experts),
        grid=(n // tr,),
        in_specs=[pl.BlockSpec((tr, d), lambda i: (i, 0)), pl.BlockSpec((1, d), lambda i: (0, 0)),
                  pl.BlockSpec((d, LANE), lambda i: (0, 0))],
        out_specs=[pl.BlockSpec((tr, d), lambda i: (i, 0)), pl.BlockSpec((tr, LANE), lambda i: (i, 0)),
                   pl.BlockSpec((SUBLANE, LANE), lambda i: (0, 0))],
        out_shape=[jax.ShapeDtypeStruct((n, d), BF16), jax.ShapeDtypeStruct((n, LANE), F32),
                   jax.ShapeDtypeStruct((SUBLANE, LANE), F32)],
        scratch_shapes=[pltpu.VMEM((SUBLANE, LANE), F32)],
        compiler_params=_params(("arbitrary",)),
        name="moe_router",
    )(x2, gain.reshape(1, d).astype(F32), wr)

    e1, e2 = meta[:, 0].astype(I32), meta[:, 1].astype(I32)
    counts = cnt[0, :n_experts].astype(I32)
    ends = jnp.cumsum(counts)
    off = ends - counts
    dest1 = off[e1] + meta[:, 2].astype(I32)
    dest2 = off[e2] + meta[:, 3].astype(I32)
    na = TOP_K_EXPERTS * n
    tok = jnp.arange(n, dtype=I32)
    src = jnp.zeros((na,), I32).at[dest1].set(tok).at[dest2].set(tok)

    xs = _gather_rows(xn, src, rows=_pick(na, (256, 128, 64, 32, 16, 8)))

    tm = _pick(na, (512, 256, 128, 64, 32, 16, 8))
    n_tiles = na // tm
    nv = n_tiles + n_experts - 1
    lo = jnp.sort(jnp.concatenate([jnp.arange(n_tiles, dtype=I32) * tm, ends[:-1]]))
    hi = jnp.concatenate([lo[1:], jnp.full((1,), na, I32)])
    tile = jnp.minimum(lo // tm, n_tiles - 1)
    expert = jnp.minimum(jnp.sum((lo[:, None] >= ends[None, :]).astype(I32), axis=1), n_experts - 1)
    vid = jnp.arange(nv, dtype=I32)
    last = lax.cummax(jnp.where(hi > lo, vid, -1))
    wexp = expert[jnp.maximum(last, 0)]
    first = jnp.concatenate([jnp.ones((1,), I32), (tile[1:] != tile[:-1]).astype(I32)])

    fc = _pick(dff, (256, 128))
    nf = dff // fc

    def up_map(v, f, tile_ref, wexp_ref, lo_ref, hi_ref, first_ref):
        return wexp_ref[v], 0, jnp.where(hi_ref[v] > lo_ref[v], f, nf - 1)

    def down_map(v, f, tile_ref, wexp_ref, lo_ref, hi_ref, first_ref):
        return wexp_ref[v], jnp.where(hi_ref[v] > lo_ref[v], f, nf - 1), 0

    def row_map(v, f, tile_ref, wexp_ref, lo_ref, hi_ref, first_ref):
        return tile_ref[v], 0

    ys = pl.pallas_call(
        functools.partial(_experts_body, tm=tm),
        grid_spec=pltpu.PrefetchScalarGridSpec(
            num_scalar_prefetch=5,
            grid=(nv, nf),
            in_specs=[pl.BlockSpec((tm, d), row_map), pl.BlockSpec((None, d, fc), up_map),
                      pl.BlockSpec((None, d, fc), up_map), pl.BlockSpec((None, fc, d), down_map)],
            out_specs=pl.BlockSpec((tm, d), row_map),
        ),
        out_shape=jax.ShapeDtypeStruct((na, d), F32),
        compiler_params=_params(("arbitrary", "arbitrary")),
        name="moe_experts",
    )(tile, wexp, lo, hi, first, xs, w1[idx].astype(BF16), w3[idx].astype(BF16), w2[idx].astype(BF16))

    rows = _pick(n, (256, 128, 64, 32, 16, 8))
    slab = (rows, d // LANE, LANE)
    slab_spec = pl.BlockSpec(slab, lambda i, d1, d2: (i, 0, 0))
    weight_spec = pl.BlockSpec((rows, 1, LANE), lambda i, d1, d2: (i, 0, 0))
    wt1 = jnp.broadcast_to(meta[:, 4][:, None, None], (n, 1, LANE))
    wt2 = jnp.broadcast_to(meta[:, 5][:, None, None], (n, 1, LANE))
    out = pl.pallas_call(
        functools.partial(_combine_body, rows=rows),
        grid_spec=pltpu.PrefetchScalarGridSpec(
            num_scalar_prefetch=2,
            grid=(n // rows,),
            in_specs=[pl.BlockSpec(memory_space=pl.ANY), slab_spec, weight_spec, weight_spec],
            out_specs=slab_spec,
            scratch_shapes=[pltpu.VMEM(slab, F32), pltpu.VMEM(slab, F32), pltpu.SemaphoreType.DMA((2,))],
        ),
        out_shape=jax.ShapeDtypeStruct((n, d // LANE, LANE), F32),
        compiler_params=_params(("arbitrary",)),
        name="moe_combine",
    )(dest1, dest2, ys.reshape(na, d // LANE, LANE), x2.reshape(n, d // LANE, LANE), wt1, wt2)
    return out.reshape(n, d)


def kernel(x, positions, mix_norm, w_in, mla_q_lat_norm, mla_kv_lat_norm, mla_w_uq, mla_w_ukv, mla_q_norm, mla_k_norm, dsa_q_norm, dsa_k_norm, ssm_lam_re, ssm_lam_im, ssm_log_dt, ssm_b_re, ssm_b_im, ssm_c_re, ssm_c_im, ssm_d, ssm_w_glu, w_branch, w_out, ffn_norm, dense_w_gate, dense_w_up, dense_w_down, moe_w_router, moe_w1, moe_w3, moe_w2):
    bsz, seq, d = x.shape
    depth = mix_norm.shape[0]
    x2 = x.reshape(bsz * seq, d)
    cos, sin = _rope_tables(positions)
    for layer in range(depth):
        x2 = _hybrid_mixer(x2, cos, sin, bsz, seq, mix_norm[layer], w_in[layer], mla_q_lat_norm[layer],
                           mla_kv_lat_norm[layer], mla_w_uq[layer], mla_w_ukv[layer], mla_q_norm[layer],
                           mla_k_norm[layer], dsa_q_norm[layer], dsa_k_norm[layer], ssm_lam_re[layer],
                           ssm_lam_im[layer], ssm_log_dt[layer], ssm_b_re[layer], ssm_b_im[layer], ssm_c_re[layer],
                           ssm_c_im[layer], ssm_d[layer], ssm_w_glu[layer], w_branch[layer], w_out, layer)
        i = layer // 2
        if layer % 2 == 0:
            x2 = _dense_ffn(x2, ffn_norm[layer], dense_w_gate, dense_w_up, dense_w_down, i)
        else:
            x2 = _moe_ffn(x2, ffn_norm[layer], moe_w_router, moe_w1, moe_w3, moe_w2, i)
    return x2.reshape(bsz, seq, d)
```

```python
import functools
import math

import jax
import jax.numpy as jnp
from jax import lax
from jax.experimental import pallas as pl
from jax.experimental.pallas import tpu as pltpu

F32 = jnp.float32
BF16 = jnp.bfloat16
I32 = jnp.int32

SSM_GROUP = 16
SSM_STATE = 64
MLA_HEADS = 12
MLA_NOPE = 128
MLA_ROPE = 64
MLA_V = 128
DSA_HEADS = 12
DSA_HEAD_DIM = 128
IDX_HEADS = 16
IDX_DIM = 64
DSA_TOPK_MAX = 256
TOP_K_EXPERTS = 2
ROPE_THETA = 10000.0
EPS = 1e-6

LANE = 128
SUBLANE = 8
VMEM_LIMIT_BYTES = 56 * 1024 * 1024
NEG_BIG = -1e30
INT_MIN = -(2 ** 31)
LOG2E = 1.4426950408889634


def _params(semantics):
    return pltpu.CompilerParams(dimension_semantics=semantics, vmem_limit_bytes=VMEM_LIMIT_BYTES)


def _sigmoid(x):
    return 1.0 / (1.0 + jnp.exp(-x))


def _pick(n, candidates):
    for c in candidates:
        if n % c == 0:
            return c
    return n


def _rmsnorm_body(x_ref, g_ref, o_ref):
    x = x_ref[...].astype(F32)
    ms = jnp.mean(x * x, axis=-1, keepdims=True)
    o_ref[...] = (x * lax.rsqrt(ms + EPS) * g_ref[...]).astype(o_ref.dtype)


def _rmsnorm(x, gain, *, width=None, col_block=0, out_dtype=BF16):
    m = x.shape[0]
    width = width or x.shape[1]
    tm = _pick(m, (512, 256, 128, 64, 32, 16, 8))
    return pl.pallas_call(
        _rmsnorm_body,
        grid=(m // tm,),
        in_specs=[pl.BlockSpec((tm, width), lambda i: (i, col_block)),
                  pl.BlockSpec((1, width), lambda i: (0, 0))],
        out_specs=pl.BlockSpec((tm, width), lambda i: (i, 0)),
        out_shape=jax.ShapeDtypeStruct((m, width), out_dtype),
        compiler_params=_params(("parallel",)),
        name="rmsnorm",
    )(x, gain.reshape(1, width).astype(F32))


def _mm_body(*refs, n_a, pairs, n_extra, epilogue, normed):
    a_refs = refs[:n_a]
    w_refs = refs[n_a:n_a + len(pairs)]
    e_refs = refs[n_a + len(pairs):n_a + len(pairs) + n_extra]
    if normed:
        g_ref, o_ref, xn_ref = refs[n_a + len(pairs) + n_extra:]
    else:
        o_ref = refs[n_a + len(pairs) + n_extra]
    if normed:

        @pl.when(pl.program_id(1) == 0)
        def _():
            x = a_refs[0][...].astype(F32)
            ms = jnp.mean(x * x, axis=-1, keepdims=True)
            xn_ref[...] = (x * lax.rsqrt(ms + EPS) * g_ref[...]).astype(BF16)

        a_vals = [xn_ref[...]]
    else:
        a_vals = [a[...].astype(BF16) for a in a_refs]
    parts = [jnp.dot(a_vals[ai], w[...].astype(BF16), preferred_element_type=F32)
             for ai, w in zip(pairs, w_refs)]
    o_ref[...] = epilogue(parts, *[e[...] for e in e_refs]).astype(o_ref.dtype)


def _matmul(a_list, w_list, *, epilogue, out_dtype, tn, tm=None, k_slice=None, extras=(), norm_gain=None,
            name="matmul"):
    m = a_list[0].shape[0]
    n = w_list[0][1].shape[-1]
    tm = tm or _pick(m, (1024, 512, 256, 128, 64, 32, 16, 8))
    ksize, kidx = k_slice if k_slice else (None, 0)
    in_specs = []
    for a in a_list:
        in_specs.append(pl.BlockSpec((tm, ksize or a.shape[1]), lambda i, j: (i, kidx)))
    for _, w, prefix in w_list:
        in_specs.append(pl.BlockSpec((None,) * len(prefix) + (ksize or w.shape[-2], tn),
                                     lambda i, j, prefix=prefix: tuple(prefix) + (kidx, j)))
    for arr, kind, arg in extras:
        if kind == "mn":
            in_specs.append(pl.BlockSpec((tm, tn), lambda i, j, arg=arg: (i, j + arg)))
        else:
            in_specs.append(pl.BlockSpec((1, tn), lambda i, j: (0, j)))
    pairs = tuple(ai for ai, _, _ in w_list)
    operands = [*a_list, *[w for _, w, _ in w_list], *[arr for arr, _, _ in extras]]
    scratch = []
    if norm_gain is not None:
        kdim = a_list[0].shape[1]
        in_specs.append(pl.BlockSpec((1, kdim), lambda i, j: (0, 0)))
        operands.append(norm_gain.reshape(1, kdim).astype(F32))
        scratch.append(pltpu.VMEM((tm, kdim), BF16))
    body = functools.partial(_mm_body, n_a=len(a_list), pairs=pairs, n_extra=len(extras), epilogue=epilogue,
                             normed=norm_gain is not None)
    return pl.pallas_call(
        body,
        grid=(m // tm, n // tn),
        in_specs=in_specs,
        out_specs=pl.BlockSpec((tm, tn), lambda i, j: (i, j)),
        out_shape=jax.ShapeDtypeStruct((m, n), out_dtype),
        scratch_shapes=scratch,
        compiler_params=_params(("parallel", "arbitrary")),
        name=name,
    )(*operands)


def _ep_plain(ps):
    return ps[0]


def _ep_residual(ps, res):
    return res + ps[0]


def _ep_swiglu(ps):
    g = ps[0]
    return g * _sigmoid(g) * ps[1]


def _ep_glu(ps):
    return ps[0] * _sigmoid(ps[1])


def _ep_branches(ps, g0, g1, g2):
    return _sigmoid(g0) * ps[0] + _sigmoid(g1) * ps[1] + _sigmoid(g2) * ps[2]


def _s5_body(u_ref, wre_ref, wim_ref, a_ref, cre_ref, cim_ref, d_ref, y_ref,
             bre_ref, bim_ref, xr_ref, xi_ref, car_ref, *, steps):
    t = pl.program_id(2)
    lw = bre_ref.shape[-1]

    @pl.when(t == 0)
    def _():
        car_ref[...] = jnp.zeros_like(car_ref)

    u = u_ref[0]
    ub = u.astype(BF16)
    bre_ref[...] = jnp.dot(ub, wre_ref[...], preferred_element_type=F32)
    bim_ref[...] = jnp.dot(ub, wim_ref[...], preferred_element_type=F32)

    ar = jnp.broadcast_to(a_ref[0:1, :], (SUBLANE, lw))
    ai = jnp.broadcast_to(a_ref[1:2, :], (SUBLANE, lw))

    def pass1(j, carry):
        xr, xi = carry
        off = pl.multiple_of(j * SUBLANE, SUBLANE)
        nr = ar * xr - ai * xi + bre_ref[pl.ds(off, SUBLANE), :]
        ni = ar * xi + ai * xr + bim_ref[pl.ds(off, SUBLANE), :]
        xr_ref[pl.ds(off, SUBLANE), :] = nr
        xi_ref[pl.ds(off, SUBLANE), :] = ni
        return nr, ni

    zero = jnp.zeros((SUBLANE, lw), F32)
    er, ei = lax.fori_loop(0, steps, pass1, (zero, zero))

    pr = a_ref[2:3, :]
    pi = a_ref[3:4, :]
    cr = car_ref[0:1, :]
    ci = car_ref[1:2, :]
    rows_r, rows_i = [], []
    for s in range(SUBLANE):
        rows_r.append(cr)
        rows_i.append(ci)
        nr = pr * cr - pi * ci + er[s:s + 1, :]
        ni = pr * ci + pi * cr + ei[s:s + 1, :]
        cr, ci = nr, ni
    car_ref[0:1, :] = cr
    car_ref[1:2, :] = ci
    cin_r = jnp.concatenate(rows_r, axis=0)
    cin_i = jnp.concatenate(rows_i, axis=0)

    def pass2(j, carry):
        cr_, ci_ = carry
        off = pl.multiple_of(j * SUBLANE, SUBLANE)
        nr = ar * cr_ - ai * ci_
        ni = ar * ci_ + ai * cr_
        xr_ref[pl.ds(off, SUBLANE), :] = xr_ref[pl.ds(off, SUBLANE), :] + nr
        xi_ref[pl.ds(off, SUBLANE), :] = xi_ref[pl.ds(off, SUBLANE), :] + ni
        return nr, ni

    lax.fori_loop(0, steps, pass2, (cin_r, cin_i))

    y = (jnp.dot(xr_ref[...].astype(BF16), cre_ref[...], preferred_element_type=F32)
         + jnp.dot(xi_ref[...].astype(BF16), cim_ref[...], preferred_element_type=F32) + d_ref[...] * u)
    y_ref[0] = jax.nn.gelu(y).astype(y_ref.dtype)


def _s5_core(u, bin_re, bin_im, a_tab, bout_re, bout_im, d_skip, *, chunk):
    b, s, w = u.shape
    ns, _, sw = bin_re.shape
    steps = chunk // SUBLANE
    lanes = pl.BlockSpec((1, chunk, LANE), lambda bi, li, ti: (bi, ti, li))
    slab_in = pl.BlockSpec((None, LANE, sw), lambda bi, li, ti: (li, 0, 0))
    slab_out = pl.BlockSpec((None, sw, LANE), lambda bi, li, ti: (li, 0, 0))
    return pl.pallas_call(
        functools.partial(_s5_body, steps=steps),
        grid=(b, ns, s // chunk),
        in_specs=[lanes, slab_in, slab_in, pl.BlockSpec((4, sw), lambda bi, li, ti: (0, li)), slab_out, slab_out,
                  pl.BlockSpec((1, LANE), lambda bi, li, ti: (0, li))],
        out_specs=lanes,
        out_shape=jax.ShapeDtypeStruct((b, s, w), BF16),
        scratch_shapes=[pltpu.VMEM((chunk, sw), F32)] * 4 + [pltpu.VMEM((SUBLANE, sw), F32)],
        compiler_params=_params(("parallel", "parallel", "arbitrary")),
        name="s5_core",
    )(u, bin_re, bin_im, a_tab, bout_re, bout_im, d_skip)


def _s5_chunk(s):
    return _pick(s, (512, 256, 128, 64))


def _to_chunk_order(t, chunk):
    b, s, w = t.shape
    return t.reshape(b, s // chunk, SUBLANE, chunk // SUBLANE, w).transpose(0, 1, 3, 2, 4).reshape(b, s, w)


def _from_chunk_order(t, chunk):
    b, s, w = t.shape
    return t.reshape(b, s // chunk, chunk // SUBLANE, SUBLANE, w).transpose(0, 1, 3, 2, 4).reshape(b, s, w)


def _s5_discretise(lam_re, lam_im, log_dt, b_re, b_im, c_re, c_im, steps):
    g, p = lam_re.shape
    h = b_re.shape[-1]
    lr, li = lam_re.astype(F32), lam_im.astype(F32)
    dt = jnp.exp(log_dt.astype(F32))[:, None]
    mag = jnp.exp(lr * dt)
    ab_re, ab_im = mag * jnp.cos(li * dt), mag * jnp.sin(li * dt)
    den = lr * lr + li * li
    nr = ab_re - 1.0
    zr = (nr * lr + ab_im * li) / den
    zi = (ab_im * lr - nr * li) / den
    bb_re = zr[..., None] * b_re - zi[..., None] * b_im
    bb_im = zr[..., None] * b_im + zi[..., None] * b_re
    gs = LANE // h
    ns = g // gs
    eye = jnp.eye(gs, dtype=F32)
    slab_in = lambda t: jnp.einsum("sgph,gk->sghkp", t.reshape(ns, gs, p, h), eye).reshape(ns, gs * h, gs * p)
    slab_out = lambda t: jnp.einsum("sghp,gk->sgpkh", t.reshape(ns, gs, h, p), eye).reshape(ns, gs * p, gs * h)
    pr, pi = ab_re, ab_im
    for _ in range(int(round(math.log2(steps)))):
        pr, pi = pr * pr - pi * pi, 2.0 * pr * pi
    a_tab = jnp.stack([ab_re.reshape(-1), ab_im.reshape(-1), pr.reshape(-1), pi.reshape(-1)])
    return (slab_in(bb_re).astype(BF16), slab_in(bb_im).astype(BF16), slab_out(c_re.astype(F32)).astype(BF16),
            slab_out(-c_im.astype(F32)).astype(BF16), a_tab)


def _s5_mixer(u, bsz, seq, lam_re, lam_im, log_dt, b_re, b_im, c_re, c_im, d_skip, w_glu):
    n, w = u.shape
    chunk = _s5_chunk(seq)
    steps = chunk // SUBLANE
    bin_re, bin_im, bout_re, bout_im, a_tab = _s5_discretise(lam_re, lam_im, log_dt, b_re, b_im, c_re, c_im, steps)
    u_p = _to_chunk_order(u.reshape(bsz, seq, w), chunk)
    y = _s5_core(u_p, bin_re, bin_im, a_tab, bout_re, bout_im, d_skip.reshape(1, w).astype(F32), chunk=chunk)
    y = y.reshape(n, w)
    tnw = _pick(w, (512, 256, 128))
    w_a = w_glu[:, :w].astype(BF16)
    w_g = w_glu[:, w:].astype(BF16)
    out = _matmul([y], [(0, w_a, ()), (0, w_g, ())], epilogue=_ep_glu, out_dtype=BF16, tn=tnw, name="s5_glu")
    return _from_chunk_order(out.reshape(bsz, seq, w), chunk).reshape(n, w)


def _rope_table_body(pos_ref, inv_ref, cos_ref, sin_ref):
    ang = pos_ref[...].astype(F32) * inv_ref[...]
    cos_ref[...] = jnp.cos(ang)
    sin_ref[...] = jnp.sin(ang)


def _rope_tables(positions):
    n = positions.size
    half = MLA_ROPE // 2
    inv = ROPE_THETA ** (-jnp.arange(half, dtype=F32) / half)
    inv = jnp.zeros((1, LANE), F32).at[0, :half].set(inv)
    tm = _pick(n, (512, 256, 128, 64, 32, 16, 8))
    return pl.pallas_call(
        _rope_table_body,
        grid=(n // tm,),
        in_specs=[pl.BlockSpec((tm, 1), lambda i: (i, 0)), pl.BlockSpec((1, LANE), lambda i: (0, 0))],
        out_specs=[pl.BlockSpec((tm, LANE), lambda i: (i, 0))] * 2,
        out_shape=[jax.ShapeDtypeStruct((n, LANE), F32)] * 2,
        compiler_params=_params(("parallel",)),
        name="rope_table",
    )(positions.reshape(n, 1).astype(I32), inv)


def _head_norm_rope(t0, t1, t2, g0, g1, g2, cos, sin, true_dim):
    ss = (jnp.sum(t0 * t0, axis=-1, keepdims=True) + jnp.sum(t1 * t1, axis=-1, keepdims=True)
          + jnp.sum(t2 * t2, axis=-1, keepdims=True))
    inv = lax.rsqrt(ss * (1.0 / true_dim) + EPS)
    n0 = t0 * inv * g0
    n1 = t1 * inv * g1
    n2 = t2 * inv * g2
    o1 = n1 * cos - n2 * sin
    o2 = n1 * sin + n2 * cos
    half = MLA_ROPE // 2
    lane = lax.broadcasted_iota(I32, o1.shape, 1)
    o1 = jnp.where(lane < half, o1, 0.0)
    o2 = jnp.where(lane < half, o2, 0.0)
    return jnp.concatenate([n0, o1 + pltpu.roll(o2, half, 1)], axis=-1)


def _mla_qkv_body(cq_ref, ckv_ref, pe_ref, gq_lat_ref, gkv_lat_ref, wq_ref, wkv_ref, qg_ref, kg_ref, cos_ref, sin_ref,
                  q_ref, k_ref, v_ref):
    def lat_norm(ref, g_ref):
        t = ref[...]
        ms = jnp.mean(t * t, axis=-1, keepdims=True)
        return (t * lax.rsqrt(ms + EPS) * g_ref[...]).astype(BF16)

    cqn = lat_norm(cq_ref, gq_lat_ref)
    ckvn = lat_norm(ckv_ref, gkv_lat_ref)
    cos, sin = cos_ref[...], sin_ref[...]
    hw = 3 * LANE
    qg = (qg_ref[:, 0:LANE], qg_ref[:, LANE:2 * LANE], qg_ref[:, 2 * LANE:hw])
    kg = (kg_ref[:, 0:LANE], kg_ref[:, LANE:2 * LANE], kg_ref[:, 2 * LANE:hw])
    pe1 = pe_ref[:, 0:LANE]
    pe2 = pe_ref[:, LANE:2 * LANE]
    kvw = MLA_NOPE + MLA_V
    qk = MLA_NOPE + MLA_ROPE
    for h in range(MLA_HEADS):
        qh = jnp.dot(cqn, wq_ref[:, h * hw:(h + 1) * hw], preferred_element_type=F32)
        qh = _head_norm_rope(qh[:, 0:LANE], qh[:, LANE:2 * LANE], qh[:, 2 * LANE:hw], *qg, cos, sin, qk)
        q_ref[0, h] = (qh * (qk ** -0.5 * LOG2E)).astype(q_ref.dtype)
        kvh = jnp.dot(ckvn, wkv_ref[:, h * kvw:(h + 1) * kvw], preferred_element_type=F32)
        k_ref[0, h] = _head_norm_rope(kvh[:, 0:MLA_NOPE], pe1, pe2, *kg, cos, sin, qk).astype(k_ref.dtype)
        v_ref[0, h] = kvh[:, MLA_NOPE:kvw].astype(v_ref.dtype)


def _pad_rope_gain(gain):
    half = MLA_ROPE // 2
    out = jnp.zeros((3 * LANE,), F32)
    out = out.at[:MLA_NOPE].set(gain[:MLA_NOPE].astype(F32))
    out = out.at[LANE:LANE + half].set(gain[MLA_NOPE:MLA_NOPE + half].astype(F32))
    out = out.at[2 * LANE:2 * LANE + half].set(gain[MLA_NOPE + half:].astype(F32))
    return out.reshape(1, 3 * LANE)


def _flash_body(q_ref, k_ref, v_ref, o_ref, *, tq, wide):
    qi = pl.program_id(2)
    dv = v_ref.shape[-1]
    q = q_ref[0, 0]

    def step(off, width, carry, masked):
        m, l, acc = carry
        k = k_ref[0, 0, pl.ds(off, width), :]
        v = v_ref[0, 0, pl.ds(off, width), :]
        s = lax.dot_general(q, k, (((1,), (1,)), ((), ())), preferred_element_type=F32)
        if masked:
            row = lax.broadcasted_iota(I32, (tq, width), 0)
            col = lax.broadcasted_iota(I32, (tq, width), 1)
            s = jnp.where(col <= row, s, NEG_BIG)
        m_new = jnp.maximum(m, jnp.max(s, axis=1, keepdims=True))
        alpha = jnp.exp2(m - m_new)
        p = jnp.exp2(s - m_new)
        l = alpha * l + jnp.sum(p, axis=1, keepdims=True)
        acc = alpha * acc + jnp.dot(p.astype(BF16), v, preferred_element_type=F32)
        return m_new, l, acc

    per = wide // tq
    n_wide = qi // per
    carry = (jnp.full((tq, 1), NEG_BIG, F32), jnp.zeros((tq, 1), F32), jnp.zeros((tq, dv), F32))
    carry = lax.fori_loop(0, n_wide, lambda j, c: step(pl.multiple_of(j * wide, wide), wide, c, False), carry)
    carry = lax.fori_loop(n_wide * per, qi, lambda j, c: step(pl.multiple_of(j * tq, tq), tq, c, False), carry)
    m, l, acc = step(pl.multiple_of(qi * tq, tq), tq, carry, True)
    o_ref[0] = (acc / l).astype(o_ref.dtype)


def _flash_attention(q, k, v):
    b, h, s, dk = q.shape
    dv = v.shape[-1]
    tq = _pick(s, (1024, 512, 256, 128))
    wide = _pick(s, (2 * tq, tq))
    return pl.pallas_call(
        functools.partial(_flash_body, tq=tq, wide=wide),
        grid=(b, h, s // tq),
        in_specs=[pl.BlockSpec((1, 1, tq, dk), lambda bi, hi, qi: (bi, hi, qi, 0)),
                  pl.BlockSpec((1, 1, s, dk), lambda bi, hi, qi: (bi, hi, 0, 0)),
                  pl.BlockSpec((1, 1, s, dv), lambda bi, hi, qi: (bi, hi, 0, 0))],
        out_specs=pl.BlockSpec((1, tq, dv), lambda bi, hi, qi: (bi, qi, hi)),
        out_shape=jax.ShapeDtypeStruct((b, s, h * dv), BF16),
        compiler_params=_params(("parallel", "parallel", "arbitrary")),
        name="mla_flash",
    )(q, k, v)


def _mla_mixer(hm, col, cos, sin, bsz, seq, q_lora, kv_lora, q_lat_gain, kv_lat_gain, w_uq, w_ukv, q_gain, k_gain):
    n = hm.shape[0]
    half = MLA_ROPE // 2
    qk = MLA_NOPE + MLA_ROPE
    wq = w_uq.reshape(q_lora, MLA_HEADS, qk)
    wq_p = jnp.zeros((q_lora, MLA_HEADS, 3 * LANE), BF16)
    wq_p = wq_p.at[:, :, :MLA_NOPE].set(wq[:, :, :MLA_NOPE].astype(BF16))
    wq_p = wq_p.at[:, :, LANE:LANE + half].set(wq[:, :, MLA_NOPE:MLA_NOPE + half].astype(BF16))
    wq_p = wq_p.at[:, :, 2 * LANE:2 * LANE + half].set(wq[:, :, MLA_NOPE + half:].astype(BF16))
    wq_p = wq_p.reshape(q_lora, MLA_HEADS * 3 * LANE)
    wkv = w_ukv.astype(BF16)
    tm = _pick(seq, (256, 128, 64, 32, 16))
    nblk = seq // tm
    cols = lambda width, block: pl.BlockSpec((tm, width), lambda bi, si: (bi * nblk + si, block))
    const = lambda arr: pl.BlockSpec(arr.shape, lambda bi, si: (0, 0), pipeline_mode=pl.Buffered(1))
    heads = lambda width: pl.BlockSpec((1, MLA_HEADS, tm, width), lambda bi, si: (bi, 0, si, 0))
    gq_lat = q_lat_gain.reshape(1, q_lora).astype(F32)
    gkv_lat = kv_lat_gain.reshape(1, kv_lora).astype(F32)
    qg = _pad_rope_gain(q_gain)
    kg = _pad_rope_gain(k_gain)
    q, k, v = pl.pallas_call(
        _mla_qkv_body,
        grid=(bsz, nblk),
        in_specs=[cols(q_lora, col["c_q"] // q_lora), cols(kv_lora, col["c_kv"] // kv_lora),
                  cols(2 * LANE, col["pe"] // (2 * LANE)),
                  const(gq_lat), const(gkv_lat), const(wq_p), const(wkv), const(qg), const(kg),
                  cols(LANE, 0), cols(LANE, 0)],
        out_specs=[heads(2 * LANE), heads(2 * LANE), heads(MLA_V)],
        out_shape=[jax.ShapeDtypeStruct((bsz, MLA_HEADS, seq, 2 * LANE), BF16),
                   jax.ShapeDtypeStruct((bsz, MLA_HEADS, seq, 2 * LANE), BF16),
                   jax.ShapeDtypeStruct((bsz, MLA_HEADS, seq, MLA_V), BF16)],
        compiler_params=_params(("parallel", "parallel")),
        name="mla_qkv",
    )(hm, hm, hm, gq_lat, gkv_lat, wq_p, wkv, qg, kg, cos, sin)
    return _flash_attention(q, k, v).reshape(n, MLA_HEADS * MLA_V)


def _dsa_prep_body(hd_q_ref, hd_k_ref, hd_v_ref, hi_q_ref, hi_k_ref, hi_w_ref, qg_ref, kg_ref,
                   q_ref, k_ref, v_ref, qi_ref, klo_ref, khi_ref, w_ref):
    def norm(t, g):
        ms = jnp.mean(t * t, axis=-1, keepdims=True)
        return t * lax.rsqrt(ms + EPS) * g

    qg = qg_ref[...]
    for h in range(DSA_HEADS):
        qh = norm(hd_q_ref[:, h * DSA_HEAD_DIM:(h + 1) * DSA_HEAD_DIM], qg)
        q_ref[0, h] = (qh * (DSA_HEAD_DIM ** -0.5 * LOG2E)).astype(q_ref.dtype)
    k_ref[0] = norm(hd_k_ref[...], kg_ref[...]).astype(k_ref.dtype)
    v_ref[0] = hd_v_ref[...].astype(v_ref.dtype)
    qi_ref[0] = hi_q_ref[...].astype(qi_ref.dtype)
    ki = hi_k_ref[...]
    lane = lax.broadcasted_iota(I32, ki.shape, 1)
    ki = jnp.where(lane < IDX_DIM, ki, 0.0)
    klo_ref[0] = ki.astype(klo_ref.dtype)
    khi_ref[0] = pltpu.roll(ki, IDX_DIM, 1).astype(khi_ref.dtype)
    w_ref[0] = hi_w_ref[...] * ((IDX_HEADS ** -0.5) * (IDX_DIM ** -0.5))


def _float_key(x):
    bits = pltpu.bitcast(x, I32)
    return jnp.where(bits < 0, bits ^ jnp.int32(0x7FFFFFFF), bits)


def _dsa_body(q_ref, k_ref, v_ref, qi_ref, klo_ref, khi_ref, w_ref, o_ref, keys_ref, gmax_ref, acc_ref,
              *, tq, tk, wide, n_sel):
    qb = pl.program_id(1)
    n_kv = (qb * tq + tq + tk - 1) // tk
    row = lax.broadcasted_iota(I32, (tq, tk), 0) + qb * tq
    col0 = lax.broadcasted_iota(I32, (tq, tk), 1)

    w = w_ref[0]

    def score_tile(j, _):
        off = pl.multiple_of(j * tk, tk)
        klo = klo_ref[0, pl.ds(off, tk), :]
        khi = khi_ref[0, pl.ds(off, tk), :]
        acc = jnp.zeros((tq, tk), F32)
        for hp in range(IDX_HEADS // 2):
            q2 = qi_ref[0, :, hp * LANE:(hp + 1) * LANE]
            for half, kk in ((0, klo), (1, khi)):
                h = 2 * hp + half
                logit = lax.dot_general(q2, kk, (((1,), (1,)), ((), ())), preferred_element_type=F32)
                acc = acc + w[:, h:h + 1] * jnp.maximum(logit, 0.0)
        causal = col0 + off <= row
        keys_ref[:, pl.ds(off, tk)] = jnp.where(causal, _float_key(acc + 0.0), INT_MIN)
        sc = jnp.where(causal, acc, -jnp.inf)
        for c in range(tk // LANE):
            gmax_ref[c % 2] = jnp.maximum(gmax_ref[c % 2], sc[:, c * LANE:(c + 1) * LANE])
        return 0

    gmax_ref[...] = jnp.full(gmax_ref.shape, -jnp.inf, F32)
    lax.fori_loop(0, n_kv, score_tile, 0)

    rc = min(tq, 128)

    def row_group(g, carry):
        r0 = pl.multiple_of(g * rc, rc)
        ge = gmax_ref[0, pl.ds(r0, rc), :]
        go = gmax_ref[1, pl.ds(r0, rc), :]
        key_l = _float_key(jnp.min(jnp.minimum(ge, go), axis=1, keepdims=True) + 0.0)
        key_u = _float_key(jnp.max(jnp.maximum(ge, go), axis=1, keepdims=True) + 0.0)
        shared = lax.clz(key_l ^ key_u)
        start = jnp.min(shared)
        mask = jnp.where(start > 0, lax.shift_left(jnp.int32(-1), 32 - jnp.maximum(start, 1)), 0)
        lo_init = ((key_l ^ INT_MIN) & mask) ^ INT_MIN
        lo_init = jnp.broadcast_to(lo_init, (rc, LANE))

        def bit_step(b, lo):
            cand = lo + lax.shift_left(jnp.int32(1), 31 - b)

            def count_tile(j, part):
                off = pl.multiple_of(j * tk, tk)
                for c in range(tk // LANE):
                    kc = keys_ref[pl.ds(r0, rc), pl.ds(off + c * LANE, LANE)]
                    part = part + jnp.where(kc >= cand, 1.0, 0.0)
                return part

            part = lax.fori_loop(0, n_kv, count_tile, jnp.zeros((rc, LANE), F32))
            cnt = jnp.sum(part, axis=1, keepdims=True)
            return jnp.where(cnt >= n_sel, cand, lo)

        thr = lax.fori_loop(start, 32, bit_step, lo_init)
        thr = jnp.maximum(thr, INT_MIN + 1)

        def bias_tile(j, c2):
            off = pl.multiple_of(j * tk, tk)
            for c in range(tk // LANE):
                kc = keys_ref[pl.ds(r0, rc), pl.ds(off + c * LANE, LANE)]
                bias = jnp.where(kc >= thr, 0.0, NEG_BIG)
                keys_ref[pl.ds(r0, rc), pl.ds(off + c * LANE, LANE)] = pltpu.bitcast(bias, I32)
            return c2

        lax.fori_loop(0, n_kv, bias_tile, 0)
        return carry

    lax.fori_loop(0, tq // rc, row_group, 0)

    per = wide // tk
    n_wide = n_kv // per

    def head(h, carry):
        q = q_ref[0, h]

        def step(off, width, st):
            m, l, acc = st
            k = k_ref[0, pl.ds(off, width), :]
            v = v_ref[0, pl.ds(off, width), :]
            bias = pltpu.bitcast(keys_ref[:, pl.ds(off, width)], F32)
            s = lax.dot_general(q, k, (((1,), (1,)), ((), ())), preferred_element_type=F32) + bias
            m_new = jnp.maximum(m, jnp.max(s, axis=1, keepdims=True))
            alpha = jnp.exp2(m - m_new)
            p = jnp.exp2(s - m_new)
            l = alpha * l + jnp.sum(p, axis=1, keepdims=True)
            acc = alpha * acc + jnp.dot(p.astype(BF16), v, preferred_element_type=F32)
            return m_new, l, acc

        st = (jnp.full((tq, 1), NEG_BIG, F32), jnp.zeros((tq, 1), F32), jnp.zeros((tq, DSA_HEAD_DIM), F32))
        st = lax.fori_loop(0, n_wide, lambda j, c: step(pl.multiple_of(j * wide, wide), wide, c), st)
        st = lax.fori_loop(n_wide * per, n_kv, lambda j, c: step(pl.multiple_of(j * tk, tk), tk, c), st)
        m, l, acc = st
        acc_ref[h] = acc / l
        return carry

    lax.fori_loop(0, DSA_HEADS, head, 0)
    for h in range(DSA_HEADS):
        o_ref[0, :, h * DSA_HEAD_DIM:(h + 1) * DSA_HEAD_DIM] = acc_ref[h].astype(o_ref.dtype)


def _dsa_mixer(hd, col, bsz, seq, q_gain, k_gain):
    n = hd.shape[0]
    qw = DSA_HEADS * DSA_HEAD_DIM
    iw = IDX_HEADS * IDX_DIM
    tm = _pick(seq, (256, 128, 64, 32, 16))
    nblk = seq // tm

    def cols(width, block):
        return pl.BlockSpec((tm, width), lambda bi, si: (bi * nblk + si, block))

    per_tok = lambda width: pl.BlockSpec((1, tm, width), lambda bi, si: (bi, si, 0))
    q, k, v, qi, klo, khi, w = pl.pallas_call(
        _dsa_prep_body,
        grid=(bsz, nblk),
        in_specs=[cols(qw, col["q_c"] // qw), cols(LANE, col["k_c"] // LANE), cols(LANE, col["v_c"] // LANE),
                  cols(iw, col["q_i"] // iw), cols(LANE, col["k_i"] // LANE), cols(LANE, col["w_i"] // LANE),
                  pl.BlockSpec((1, DSA_HEAD_DIM), lambda bi, si: (0, 0)),
                  pl.BlockSpec((1, DSA_HEAD_DIM), lambda bi, si: (0, 0))],
        out_specs=[pl.BlockSpec((1, DSA_HEADS, tm, DSA_HEAD_DIM), lambda bi, si: (bi, 0, si, 0)),
                   per_tok(DSA_HEAD_DIM), per_tok(DSA_HEAD_DIM), per_tok(iw), per_tok(LANE), per_tok(LANE),
                   per_tok(LANE)],
        out_shape=[jax.ShapeDtypeStruct((bsz, DSA_HEADS, seq, DSA_HEAD_DIM), BF16),
                   jax.ShapeDtypeStruct((bsz, seq, DSA_HEAD_DIM), BF16),
                   jax.ShapeDtypeStruct((bsz, seq, DSA_HEAD_DIM), BF16),
                   jax.ShapeDtypeStruct((bsz, seq, iw), BF16),
                   jax.ShapeDtypeStruct((bsz, seq, LANE), BF16),
                   jax.ShapeDtypeStruct((bsz, seq, LANE), BF16),
                   jax.ShapeDtypeStruct((bsz, seq, LANE), F32)],
        compiler_params=_params(("parallel", "parallel")),
        name="dsa_prep",
    )(hd, hd, hd, hd, hd, hd, q_gain.reshape(1, -1).astype(F32), k_gain.reshape(1, -1).astype(F32))

    tq = _pick(seq, (512, 256, 128))
    tk = _pick(seq, (512, 256, 128))
    n_sel = min(DSA_TOPK_MAX, seq // 4)
    assert n_sel <= 2 * LANE
    whole = lambda width: pl.BlockSpec((1, seq, width), lambda bi, qb: (bi, 0, 0), pipeline_mode=pl.Buffered(1))
    out = pl.pallas_call(
        functools.partial(_dsa_body, tq=tq, tk=tk, wide=_pick(seq, (4 * tk, 2 * tk, tk)), n_sel=n_sel),
        grid=(bsz, seq // tq),
        in_specs=[pl.BlockSpec((1, DSA_HEADS, tq, DSA_HEAD_DIM), lambda bi, qb: (bi, 0, qb, 0)),
                  whole(DSA_HEAD_DIM), whole(DSA_HEAD_DIM),
                  pl.BlockSpec((1, tq, iw), lambda bi, qb: (bi, qb, 0)),
                  whole(LANE), whole(LANE),
                  pl.BlockSpec((1, tq, LANE), lambda bi, qb: (bi, qb, 0))],
        out_specs=pl.BlockSpec((1, tq, qw), lambda bi, qb: (bi, qb, 0)),
        out_shape=jax.ShapeDtypeStruct((bsz, seq, qw), BF16),
        scratch_shapes=[pltpu.VMEM((tq, seq), I32), pltpu.VMEM((2, tq, LANE), F32),
                        pltpu.VMEM((DSA_HEADS, tq, DSA_HEAD_DIM), F32)],
        compiler_params=_params(("parallel", "arbitrary")),
        name="dsa_attention",
    )(q, k, v, qi, klo, khi, w)
    return out.reshape(n, qw)


def _pad_cols(w, width):
    return jnp.pad(w, ((0, 0), (0, width - w.shape[1])))


def _hybrid_mixer(x2, cos, sin, bsz, seq, norm_gain, w_in, q_lat_gain, kv_lat_gain, w_uq, w_ukv, mla_q_gain,
                  mla_k_gain, dsa_q_gain, dsa_k_gain, lam_re, lam_im, log_dt, b_re, b_im, c_re, c_im, d_skip, w_glu,
                  w_branch, w_out, layer):
    n, d = x2.shape
    ssm_w = d_skip.shape[-1]
    q_lora = q_lat_gain.shape[-1]
    kv_lora = kv_lat_gain.shape[-1]
    half = MLA_ROPE // 2
    dsa_w = DSA_HEADS * DSA_HEAD_DIM
    idx_w = IDX_HEADS * IDX_DIM
    sizes = (ssm_w, q_lora, kv_lora, MLA_ROPE, dsa_w, DSA_HEAD_DIM, DSA_HEAD_DIM, idx_w, IDX_DIM, IDX_HEADS, 3 * d)
    offs = [0]
    for s in sizes:
        offs.append(offs[-1] + s)
    seg = lambda i: w_in[:, offs[i]:offs[i + 1]]
    pe = seg(3)
    tn = _pick(d, (256, 128))
    pe_pad = jnp.concatenate([_pad_cols(pe[:, :half], LANE), _pad_cols(pe[:, half:], LANE)], axis=1)
    parts = [("q_c", seg(4), dsa_w), ("c_kv", seg(2), kv_lora), ("u", seg(0), LANE), ("c_q", seg(1), q_lora),
             ("q_i", seg(7), idx_w), ("pe", pe_pad, 2 * LANE), ("k_c", seg(5), LANE), ("v_c", seg(6), LANE),
             ("k_i", _pad_cols(seg(8), LANE), LANE), ("w_i", _pad_cols(seg(9), LANE), LANE), ("gates", seg(10), tn)]
    col = {}
    at = 0
    for name, part, block_width in parts:
        assert at % block_width == 0, (name, at, block_width)
        col[name] = at
        at += part.shape[1]
    assert at % tn == 0
    w_cat = jnp.concatenate([part for _, part, _ in parts], axis=1).astype(BF16)
    h = _matmul([x2], [(0, w_cat, ())], epilogue=_ep_plain, out_dtype=F32, tn=tn, norm_gain=norm_gain,
                tm=_pick(n, (512, 256, 128, 64, 32, 16, 8)), name="in_proj")

    u = h[:, col["u"]:col["u"] + ssm_w]
    y_ssm = _s5_mixer(u, bsz, seq, lam_re, lam_im, log_dt, b_re, b_im, c_re, c_im, d_skip, w_glu)
    y_mla = _mla_mixer(h, col, cos, sin, bsz, seq, q_lora, kv_lora, q_lat_gain, kv_lat_gain, w_uq, w_ukv,
                       mla_q_gain, mla_k_gain)
    y_dsa = _dsa_mixer(h, col, bsz, seq, dsa_q_gain, dsa_k_gain)

    br0 = ssm_w
    br1 = ssm_w + MLA_HEADS * MLA_V
    g0 = col["gates"] // tn
    merged = _matmul([y_ssm, y_mla, y_dsa],
                     [(0, w_branch[:br0].astype(BF16), ()), (1, w_branch[br0:br1].astype(BF16), ()),
                      (2, w_branch[br1:].astype(BF16), ())],
                     epilogue=_ep_branches, out_dtype=BF16, tn=tn,
                     extras=[(h, "mn", g0), (h, "mn", g0 + d // tn), (h, "mn", g0 + 2 * (d // tn))],
                     name="branch_merge")
    return _matmul([merged], [(0, w_out, (layer,))], epilogue=_ep_residual, out_dtype=F32, tn=tn,
                   extras=[(x2, "mn", 0)], name="mixer_out")


def _dense_ffn(x2, gain, w_gate, w_up, w_down, idx):
    d = x2.shape[1]
    dff = w_gate.shape[-1]
    xn = _rmsnorm(x2, gain)
    tn = _pick(dff, (256, 128))
    h = _matmul([xn], [(0, w_gate, (idx,)), (0, w_up, (idx,))], epilogue=_ep_swiglu, out_dtype=BF16, tn=tn,
                name="ffn_up")
    tnd = _pick(d, (256, 128))
    n_slabs = 2 if dff % (2 * LANE) == 0 and dff > 8192 else 1
    out = x2
    for s in range(n_slabs):
        out = _matmul([h], [(0, w_down, (idx,))], epilogue=_ep_residual, out_dtype=F32, tn=tnd,
                      k_slice=(dff // n_slabs, s), extras=[(out, "mn", 0)], name="ffn_down")
    return out


def _router_body(x_ref, g_ref, w_ref, xn_ref, meta_ref, cnt_ref, run_ref, *, n_experts):
    i = pl.program_id(0)

    @pl.when(i == 0)
    def _():
        run_ref[...] = jnp.zeros_like(run_ref)

    x = x_ref[...]
    tm = x.shape[0]
    ms = jnp.mean(x * x, axis=-1, keepdims=True)
    xn = x * lax.rsqrt(ms + EPS) * g_ref[...]
    xn_ref[...] = xn.astype(xn_ref.dtype)
    logits = jnp.dot(xn, w_ref[...], preferred_element_type=F32, precision=lax.Precision.HIGHEST)
    lane = lax.broadcasted_iota(I32, logits.shape, 1).astype(F32)
    lg = jnp.where(lane < n_experts, logits, -jnp.inf)
    m1 = jnp.max(lg, axis=1, keepdims=True)
    i1 = jnp.min(jnp.where(lg == m1, lane, float(LANE)), axis=1, keepdims=True)
    lg2 = jnp.where(lane == i1, -jnp.inf, lg)
    m2 = jnp.max(lg2, axis=1, keepdims=True)
    i2 = jnp.min(jnp.where(lg2 == m2, lane, float(LANE)), axis=1, keepdims=True)
    e2 = jnp.exp(m2 - m1)
    den = 1.0 + e2
    oh1 = jnp.where(lane == i1, 1.0, 0.0)
    oh2 = jnp.where(lane == i2, 1.0, 0.0)
    oh = oh1 + oh2
    r = lax.broadcasted_iota(I32, (tm, tm), 0)
    c = lax.broadcasted_iota(I32, (tm, tm), 1)
    tri = jnp.where(c < r, 1.0, 0.0).astype(BF16)
    base = run_ref[0:1, :]
    rank = jnp.dot(tri, oh.astype(BF16), preferred_element_type=F32) + base
    r1 = jnp.sum(oh1 * rank, axis=1, keepdims=True)
    r2 = jnp.sum(oh2 * rank, axis=1, keepdims=True)
    total = base + jnp.sum(oh, axis=0, keepdims=True)
    run_ref[0:1, :] = total
    cnt_ref[...] = jnp.broadcast_to(total, cnt_ref.shape)
    meta = jnp.zeros_like(logits)
    for k, val in enumerate((i1, i2, r1, r2, 1.0 / den, e2 / den)):
        meta = jnp.where(lane == float(k), val, meta)
    meta_ref[...] = meta


def _gather_rows_body(src_ref, x_hbm, o_ref, sem, *, rows):
    base = pl.program_id(0) * rows

    def issue(r, carry):
        pltpu.make_async_copy(x_hbm.at[src_ref[base + r]], o_ref.at[r], sem).start()
        return carry

    lax.fori_loop(0, rows, issue, 0)
    pltpu.make_async_copy(x_hbm.at[pl.ds(0, rows)], o_ref, sem).wait()


def _gather_rows(x, src, *, rows):
    n_out = src.shape[0]
    n, d = x.shape
    x3 = x.reshape(n, d // LANE, LANE)
    out = pl.pallas_call(
        functools.partial(_gather_rows_body, rows=rows),
        grid_spec=pltpu.PrefetchScalarGridSpec(
            num_scalar_prefetch=1,
            grid=(n_out // rows,),
            in_specs=[pl.BlockSpec(memory_space=pl.ANY)],
            out_specs=pl.BlockSpec((rows, d // LANE, LANE), lambda i, src_ref: (i, 0, 0)),
            scratch_shapes=[pltpu.SemaphoreType.DMA(())],
        ),
        out_shape=jax.ShapeDtypeStruct((n_out, d // LANE, LANE), x.dtype),
        compiler_params=_params(("arbitrary",)),
        name="moe_gather",
    )(src, x3)
    return out.reshape(n_out, d)


def _experts_body(tile_ref, wexp_ref, lo_ref, hi_ref, first_ref, xs_ref, w1_ref, w3_ref, w2_ref, o_ref, *, tm):
    v = pl.program_id(0)
    f = pl.program_id(1)

    @pl.when((f == 0) & (first_ref[v] == 1))
    def _():
        o_ref[...] = jnp.zeros_like(o_ref)

    lo = lo_ref[v]
    hi = hi_ref[v]

    @pl.when(hi > lo)
    def _():
        xs = xs_ref[...]
        g = jnp.dot(xs, w1_ref[...], preferred_element_type=F32)
        u = jnp.dot(xs, w3_ref[...], preferred_element_type=F32)
        row = tile_ref[v] * tm + lax.broadcasted_iota(I32, (tm, 1), 0)
        h = jnp.where((row >= lo) & (row < hi), g * _sigmoid(g) * u, 0.0)
        o_ref[...] += jnp.dot(h.astype(BF16), w2_ref[...], preferred_element_type=F32)


def _combine_body(d1_ref, d2_ref, y_hbm, x_ref, w1_ref, w2_ref, o_ref, b1_ref, b2_ref, sem, *, rows):
    base = pl.program_id(0) * rows

    def issue(r, carry):
        pltpu.make_async_copy(y_hbm.at[d1_ref[base + r]], b1_ref.at[r], sem.at[0]).start()
        pltpu.make_async_copy(y_hbm.at[d2_ref[base + r]], b2_ref.at[r], sem.at[1]).start()
        return carry

    lax.fori_loop(0, rows, issue, 0)
    pltpu.make_async_copy(y_hbm.at[pl.ds(0, rows)], b1_ref, sem.at[0]).wait()
    pltpu.make_async_copy(y_hbm.at[pl.ds(0, rows)], b2_ref, sem.at[1]).wait()
    o_ref[...] = x_ref[...] + w1_ref[...] * b1_ref[...] + w2_ref[...] * b2_ref[...]


def _moe_ffn(x2, gain, w_router, w1, w3, w2, idx):
    n, d = x2.shape
    n_experts = w_router.shape[-1]
    dff = w1.shape[-1]
    tr = _pick(n, (256, 128, 64, 32, 16, 8))
    wr = _pad_cols(w_router[idx].astype(F32), LANE)
    xn, meta, cnt = pl.pallas_call(
        functools.partial(_router_body, n_experts=n_experts),
        grid=(n // tr,),
        in_specs=[pl.BlockSpec((tr, d), lambda i: (i, 0)), pl.BlockSpec((1, d), lambda i: (0, 0)),
                  pl.BlockSpec((d, LANE), lambda i: (0, 0))],
        out_specs=[pl.BlockSpec((tr, d), lambda i: (i, 0)), pl.BlockSpec((tr, LANE), lambda i: (i, 0)),
                   pl.BlockSpec((SUBLANE, LANE), lambda i: (0, 0))],
        out_shape=[jax.ShapeDtypeStruct((n, d), BF16), jax.ShapeDtypeStruct((n, LANE), F32),
                   jax.ShapeDtypeStruct((SUBLANE, LANE), F32)],
        scratch_shapes=[pltpu.VMEM((SUBLANE, LANE), F32)],
        compiler_params=_params(("arbitrary",)),
        name="moe_router",
    )(x2, gain.reshape(1, d).astype(F32), wr)

    e1, e2 = meta[:, 0].astype(I32), meta[:, 1].astype(I32)
    counts = cnt[0, :n_experts].astype(I32)
    ends = jnp.cumsum(counts)
    off = ends - counts
    dest1 = off[e1] + meta[:, 2].astype(I32)
    dest2 = off[e2] + meta[:, 3].astype(I32)
    na = TOP_K_EXPERTS * n
    tok = jnp.arange(n, dtype=I32)
    src = jnp.zeros((na,), I32).at[dest1].set(tok).at[dest2].set(tok)

    xs = _gather_rows(xn, src, rows=_pick(na, (256, 128, 64, 32, 16, 8)))

    tm = _pick(na, (512, 256, 128, 64, 32, 16, 8))
    n_tiles = na // tm
    nv = n_tiles + n_experts - 1
    lo = jnp.sort(jnp.concatenate([jnp.arange(n_tiles, dtype=I32) * tm, ends[:-1]]))
    hi = jnp.concatenate([lo[1:], jnp.full((1,), na, I32)])
    tile = jnp.minimum(lo // tm, n_tiles - 1)
    expert = jnp.minimum(jnp.sum((lo[:, None] >= ends[None, :]).astype(I32), axis=1), n_experts - 1)
    vid = jnp.arange(nv, dtype=I32)
    last = lax.cummax(jnp.where(hi > lo, vid, -1))
    wexp = expert[jnp.maximum(last, 0)]
    first = jnp.concatenate([jnp.ones((1,), I32), (tile[1:] != tile[:-1]).astype(I32)])

    fc = _pick(dff, (256, 128))
    nf = dff // fc

    def up_map(v, f, tile_ref, wexp_ref, lo_ref, hi_ref, first_ref):
        return wexp_ref[v], 0, jnp.where(hi_ref[v] > lo_ref[v], f, nf - 1)

    def down_map(v, f, tile_ref, wexp_ref, lo_ref, hi_ref, first_ref):
        return wexp_ref[v], jnp.where(hi_ref[v] > lo_ref[v], f, nf - 1), 0

    def row_map(v, f, tile_ref, wexp_ref, lo_ref, hi_ref, first_ref):
        return tile_ref[v], 0

    ys = pl.pallas_call(
        functools.partial(_experts_body, tm=tm),
        grid_spec=pltpu.PrefetchScalarGridSpec(
            num_scalar_prefetch=5,
            grid=(nv, nf),
            in_specs=[pl.BlockSpec((tm, d), row_map), pl.BlockSpec((None, d, fc), up_map),
                      pl.BlockSpec((None, d, fc), up_map), pl.BlockSpec((None, fc, d), down_map)],
            out_specs=pl.BlockSpec((tm, d), row_map),
        ),
        out_shape=jax.ShapeDtypeStruct((na, d), F32),
        compiler_params=_params(("arbitrary", "arbitrary")),
        name="moe_experts",
    )(tile, wexp, lo, hi, first, xs, w1[idx].astype(BF16), w3[idx].astype(BF16), w2[idx].astype(BF16))

    rows = _pick(n, (256, 128, 64, 32, 16, 8))
    slab = (rows, d // LANE, LANE)
    slab_spec = pl.BlockSpec(slab, lambda i, d1, d2: (i, 0, 0))
    weight_spec = pl.BlockSpec((rows, 1, LANE), lambda i, d1, d2: (i, 0, 0))
    wt1 = jnp.broadcast_to(meta[:, 4][:, None, None], (n, 1, LANE))
    wt2 = jnp.broadcast_to(meta[:, 5][:, None, None], (n, 1, LANE))
    out = pl.pallas_call(
        functools.partial(_combine_body, rows=rows),
        grid_spec=pltpu.PrefetchScalarGridSpec(
            num_scalar_prefetch=2,
            grid=(n // rows,),
            in_specs=[pl.BlockSpec(memory_space=pl.ANY), slab_spec, weight_spec, weight_spec],
            out_specs=slab_spec,
            scratch_shapes=[pltpu.VMEM(slab, F32), pltpu.VMEM(slab, F32), pltpu.SemaphoreType.DMA((2,))],
        ),
        out_shape=jax.ShapeDtypeStruct((n, d // LANE, LANE), F32),
        compiler_params=_params(("arbitrary",)),
        name="moe_combine",
    )(dest1, dest2, ys.reshape(na, d // LANE, LANE), x2.reshape(n, d // LANE, LANE), wt1, wt2)
    return out.reshape(n, d)


def kernel(x, positions, mix_norm, w_in, mla_q_lat_norm, mla_kv_lat_norm, mla_w_uq, mla_w_ukv, mla_q_norm, mla_k_norm, dsa_q_norm, dsa_k_norm, ssm_lam_re, ssm_lam_im, ssm_log_dt, ssm_b_re, ssm_b_im, ssm_c_re, ssm_c_im, ssm_d, ssm_w_glu, w_branch, w_out, ffn_norm, dense_w_gate, dense_w_up, dense_w_down, moe_w_router, moe_w1, moe_w3, moe_w2):
    bsz, seq, d = x.shape
    depth = mix_norm.shape[0]
    x2 = x.reshape(bsz * seq, d)
    cos, sin = _rope_tables(positions)
    for layer in range(depth):
        x2 = _hybrid_mixer(x2, cos, sin, bsz, seq, mix_norm[layer], w_in[layer], mla_q_lat_norm[layer],
                           mla_kv_lat_norm[layer], mla_w_uq[layer], mla_w_ukv[layer], mla_q_norm[layer],
                           mla_k_norm[layer], dsa_q_norm[layer], dsa_k_norm[layer], ssm_lam_re[layer],
                           ssm_lam_im[layer], ssm_log_dt[layer], ssm_b_re[layer], ssm_b_im[layer], ssm_c_re[layer],
                           ssm_c_im[layer], ssm_d[layer], ssm_w_glu[layer], w_branch[layer], w_out, layer)
        i = layer // 2
        if layer % 2 == 0:
            x2 = _dense_ffn(x2, ffn_norm[layer], dense_w_gate, dense_w_up, dense_w_down, i)
        else:
            x2 = _moe_ffn(x2, ffn_norm[layer], moe_w_router, moe_w1, moe_w3, moe_w2, i)
    return x2.reshape(bsz, seq, d)
```

```python
import functools
import math

import jax
import jax.numpy as jnp
from jax import lax
from jax.experimental import pallas as pl
from jax.experimental.pallas import tpu as pltpu

F32 = jnp.float32
BF16 = jnp.bfloat16
I32 = jnp.int32

SSM_GROUP = 16
SSM_STATE = 64
MLA_HEADS = 12
MLA_NOPE = 128
MLA_ROPE = 64
MLA_V = 128
DSA_HEADS = 12
DSA_HEAD_DIM = 128
IDX_HEADS = 16
IDX_DIM = 64
DSA_TOPK_MAX = 256
TOP_K_EXPERTS = 2
ROPE_THETA = 10000.0
EPS = 1e-6

LANE = 128
SUBLANE = 8
VMEM_LIMIT_BYTES = 56 * 1024 * 1024
NEG_BIG = -1e30
INT_MIN = -(2 ** 31)
LOG2E = 1.4426950408889634


def _params(semantics):
    return pltpu.CompilerParams(dimension_semantics=semantics, vmem_limit_bytes=VMEM_LIMIT_BYTES)


def _sigmoid(x):
    return 1.0 / (1.0 + jnp.exp(-x))


def _pick(n, candidates):
    for c in candidates:
        if n % c == 0:
            return c
    return n


def _rmsnorm_body(x_ref, g_ref, o_ref):
    x = x_ref[...].astype(F32)
    ms = jnp.mean(x * x, axis=-1, keepdims=True)
    o_ref[...] = (x * lax.rsqrt(ms + EPS) * g_ref[...]).astype(o_ref.dtype)


def _rmsnorm(x, gain, *, width=None, col_block=0, out_dtype=BF16):
    m = x.shape[0]
    width = width or x.shape[1]
    tm = _pick(m, (512, 256, 128, 64, 32, 16, 8))
    return pl.pallas_call(
        _rmsnorm_body,
        grid=(m // tm,),
        in_specs=[pl.BlockSpec((tm, width), lambda i: (i, col_block)),
                  pl.BlockSpec((1, width), lambda i: (0, 0))],
        out_specs=pl.BlockSpec((tm, width), lambda i: (i, 0)),
        out_shape=jax.ShapeDtypeStruct((m, width), out_dtype),
        compiler_params=_params(("parallel",)),
        name="rmsnorm",
    )(x, gain.reshape(1, width).astype(F32))


def _mm_body(*refs, n_a, pairs, n_extra, epilogue, normed):
    a_refs = refs[:n_a]
    w_refs = refs[n_a:n_a + len(pairs)]
    e_refs = refs[n_a + len(pairs):n_a + len(pairs) + n_extra]
    if normed:
        g_ref, o_ref, xn_ref = refs[n_a + len(pairs) + n_extra:]
    else:
        o_ref = refs[n_a + len(pairs) + n_extra]
    if normed:

        @pl.when(pl.program_id(1) == 0)
        def _():
            rows = a_refs[0].shape[0]
            step = math.gcd(rows, 128)

            def chunk(c, carry):
                r0 = pl.multiple_of(c * step, step)
                x = a_refs[0][pl.ds(r0, step), :].astype(F32)
                ms = jnp.mean(x * x, axis=-1, keepdims=True)
                xn_ref[pl.ds(r0, step), :] = (x * lax.rsqrt(ms + EPS) * g_ref[...]).astype(BF16)
                return carry

            lax.fori_loop(0, rows // step, chunk, 0)

        a_vals = [xn_ref[...]]
    else:
        a_vals = [a[...].astype(BF16) for a in a_refs]
    parts = [jnp.dot(a_vals[ai], w[...].astype(BF16), preferred_element_type=F32)
             for ai, w in zip(pairs, w_refs)]
    o_ref[...] = epilogue(parts, *[e[...] for e in e_refs]).astype(o_ref.dtype)


def _matmul(a_list, w_list, *, epilogue, out_dtype, tn, tm=None, k_slice=None, extras=(), norm_gain=None,
            name="matmul"):
    m = a_list[0].shape[0]
    n = w_list[0][1].shape[-1]
    tm = tm or _pick(m, (1024, 512, 256, 128, 64, 32, 16, 8))
    ksize, kidx = k_slice if k_slice else (None, 0)
    in_specs = []
    a_mode = {"pipeline_mode": pl.Buffered(1)} if norm_gain is not None else {}
    for a in a_list:
        in_specs.append(pl.BlockSpec((tm, ksize or a.shape[1]), lambda i, j: (i, kidx), **a_mode))
    for _, w, prefix in w_list:
        in_specs.append(pl.BlockSpec((None,) * len(prefix) + (ksize or w.shape[-2], tn),
                                     lambda i, j, prefix=prefix: tuple(prefix) + (kidx, j)))
    for arr, kind, arg in extras:
        if kind == "mn":
            in_specs.append(pl.BlockSpec((tm, tn), lambda i, j, arg=arg: (i, j + arg)))
        else:
            in_specs.append(pl.BlockSpec((1, tn), lambda i, j: (0, j)))
    pairs = tuple(ai for ai, _, _ in w_list)
    operands = [*a_list, *[w for _, w, _ in w_list], *[arr for arr, _, _ in extras]]
    scratch = []
    if norm_gain is not None:
        kdim = a_list[0].shape[1]
        in_specs.append(pl.BlockSpec((1, kdim), lambda i, j: (0, 0)))
        operands.append(norm_gain.reshape(1, kdim).astype(F32))
        scratch.append(pltpu.VMEM((tm, kdim), BF16))
    body = functools.partial(_mm_body, n_a=len(a_list), pairs=pairs, n_extra=len(extras), epilogue=epilogue,
                             normed=norm_gain is not None)
    return pl.pallas_call(
        body,
        grid=(m // tm, n // tn),
        in_specs=in_specs,
        out_specs=pl.BlockSpec((tm, tn), lambda i, j: (i, j)),
        out_shape=jax.ShapeDtypeStruct((m, n), out_dtype),
        scratch_shapes=scratch,
        compiler_params=_params(("parallel", "arbitrary")),
        name=name,
    )(*operands)


def _ep_plain(ps):
    return ps[0]


def _ep_residual(ps, res):
    return res + ps[0]


def _ep_swiglu(ps):
    g = ps[0]
    return g * _sigmoid(g) * ps[1]


def _ep_glu(ps):
    return ps[0] * _sigmoid(ps[1])


def _ep_branches(ps, g0, g1, g2):
    return _sigmoid(g0) * ps[0] + _sigmoid(g1) * ps[1] + _sigmoid(g2) * ps[2]


def _s5_body(u_ref, wre_ref, wim_ref, a_ref, cre_ref, cim_ref, d_ref, y_ref,
             bre_ref, bim_ref, xr_ref, xi_ref, car_ref, *, steps):
    t = pl.program_id(2)
    lw = bre_ref.shape[-1]

    @pl.when(t == 0)
    def _():
        car_ref[...] = jnp.zeros_like(car_ref)

    u = u_ref[0]
    ub = u.astype(BF16)
    bre_ref[...] = jnp.dot(ub, wre_ref[...], preferred_element_type=F32)
    bim_ref[...] = jnp.dot(ub, wim_ref[...], preferred_element_type=F32)

    ar = jnp.broadcast_to(a_ref[0:1, :], (SUBLANE, lw))
    ai = jnp.broadcast_to(a_ref[1:2, :], (SUBLANE, lw))

    def pass1(j, carry):
        xr, xi = carry
        off = pl.multiple_of(j * SUBLANE, SUBLANE)
        nr = ar * xr - ai * xi + bre_ref[pl.ds(off, SUBLANE), :]
        ni = ar * xi + ai * xr + bim_ref[pl.ds(off, SUBLANE), :]
        xr_ref[pl.ds(off, SUBLANE), :] = nr
        xi_ref[pl.ds(off, SUBLANE), :] = ni
        return nr, ni

    zero = jnp.zeros((SUBLANE, lw), F32)
    er, ei = lax.fori_loop(0, steps, pass1, (zero, zero))

    pr = a_ref[2:3, :]
    pi = a_ref[3:4, :]
    cr = car_ref[0:1, :]
    ci = car_ref[1:2, :]
    rows_r, rows_i = [], []
    for s in range(SUBLANE):
        rows_r.append(cr)
        rows_i.append(ci)
        nr = pr * cr - pi * ci + er[s:s + 1, :]
        ni = pr * ci + pi * cr + ei[s:s + 1, :]
        cr, ci = nr, ni
    car_ref[0:1, :] = cr
    car_ref[1:2, :] = ci
    cin_r = jnp.concatenate(rows_r, axis=0)
    cin_i = jnp.concatenate(rows_i, axis=0)

    def pass2(j, carry):
        cr_, ci_ = carry
        off = pl.multiple_of(j * SUBLANE, SUBLANE)
        nr = ar * cr_ - ai * ci_
        ni = ar * ci_ + ai * cr_
        xr_ref[pl.ds(off, SUBLANE), :] = xr_ref[pl.ds(off, SUBLANE), :] + nr
        xi_ref[pl.ds(off, SUBLANE), :] = xi_ref[pl.ds(off, SUBLANE), :] + ni
        return nr, ni

    lax.fori_loop(0, steps, pass2, (cin_r, cin_i))

    y = (jnp.dot(xr_ref[...].astype(BF16), cre_ref[...], preferred_element_type=F32)
         + jnp.dot(xi_ref[...].astype(BF16), cim_ref[...], preferred_element_type=F32) + d_ref[...] * u)
    y_ref[0] = jax.nn.gelu(y).astype(y_ref.dtype)


def _s5_core(u, bin_re, bin_im, a_tab, bout_re, bout_im, d_skip, *, chunk):
    b, s, w = u.shape
    ns, _, sw = bin_re.shape
    steps = chunk // SUBLANE
    lanes = pl.BlockSpec((1, chunk, LANE), lambda bi, li, ti: (bi, ti, li))
    slab_in = pl.BlockSpec((None, LANE, sw), lambda bi, li, ti: (li, 0, 0))
    slab_out = pl.BlockSpec((None, sw, LANE), lambda bi, li, ti: (li, 0, 0))
    return pl.pallas_call(
        functools.partial(_s5_body, steps=steps),
        grid=(b, ns, s // chunk),
        in_specs=[lanes, slab_in, slab_in, pl.BlockSpec((4, sw), lambda bi, li, ti: (0, li)), slab_out, slab_out,
                  pl.BlockSpec((1, LANE), lambda bi, li, ti: (0, li))],
        out_specs=lanes,
        out_shape=jax.ShapeDtypeStruct((b, s, w), BF16),
        scratch_shapes=[pltpu.VMEM((chunk, sw), F32)] * 4 + [pltpu.VMEM((SUBLANE, sw), F32)],
        compiler_params=_params(("parallel", "parallel", "arbitrary")),
        name="s5_core",
    )(u, bin_re, bin_im, a_tab, bout_re, bout_im, d_skip)


def _s5_chunk(s):
    return _pick(s, (512, 256, 128, 64))


def _to_chunk_order(t, chunk):
    b, s, w = t.shape
    return t.reshape(b, s // chunk, SUBLANE, chunk // SUBLANE, w).transpose(0, 1, 3, 2, 4).reshape(b, s, w)


def _from_chunk_order(t, chunk):
    b, s, w = t.shape
    return t.reshape(b, s // chunk, chunk // SUBLANE, SUBLANE, w).transpose(0, 1, 3, 2, 4).reshape(b, s, w)


def _s5_discretise(lam_re, lam_im, log_dt, b_re, b_im, c_re, c_im, steps):
    g, p = lam_re.shape
    h = b_re.shape[-1]
    lr, li = lam_re.astype(F32), lam_im.astype(F32)
    dt = jnp.exp(log_dt.astype(F32))[:, None]
    mag = jnp.exp(lr * dt)
    ab_re, ab_im = mag * jnp.cos(li * dt), mag * jnp.sin(li * dt)
    den = lr * lr + li * li
    nr = ab_re - 1.0
    zr = (nr * lr + ab_im * li) / den
    zi = (ab_im * lr - nr * li) / den
    bb_re = zr[..., None] * b_re - zi[..., None] * b_im
    bb_im = zr[..., None] * b_im + zi[..., None] * b_re
    gs = LANE // h
    ns = g // gs
    eye = jnp.eye(gs, dtype=F32)
    slab_in = lambda t: jnp.einsum("sgph,gk->sghkp", t.reshape(ns, gs, p, h), eye).reshape(ns, gs * h, gs * p)
    slab_out = lambda t: jnp.einsum("sghp,gk->sgpkh", t.reshape(ns, gs, h, p), eye).reshape(ns, gs * p, gs * h)
    pr, pi = ab_re, ab_im
    for _ in range(int(round(math.log2(steps)))):
        pr, pi = pr * pr - pi * pi, 2.0 * pr * pi
    a_tab = jnp.stack([ab_re.reshape(-1), ab_im.reshape(-1), pr.reshape(-1), pi.reshape(-1)])
    return (slab_in(bb_re).astype(BF16), slab_in(bb_im).astype(BF16), slab_out(c_re.astype(F32)).astype(BF16),
            slab_out(-c_im.astype(F32)).astype(BF16), a_tab)


def _s5_mixer(u, bsz, seq, lam_re, lam_im, log_dt, b_re, b_im, c_re, c_im, d_skip, w_glu):
    n, w = u.shape
    chunk = _s5_chunk(seq)
    steps = chunk // SUBLANE
    bin_re, bin_im, bout_re, bout_im, a_tab = _s5_discretise(lam_re, lam_im, log_dt, b_re, b_im, c_re, c_im, steps)
    u_p = _to_chunk_order(u.reshape(bsz, seq, w), chunk)
    y = _s5_core(u_p, bin_re, bin_im, a_tab, bout_re, bout_im, d_skip.reshape(1, w).astype(F32), chunk=chunk)
    y = y.reshape(n, w)
    tnw = _pick(w, (512, 256, 128))
    w_a = w_glu[:, :w].astype(BF16)
    w_g = w_glu[:, w:].astype(BF16)
    out = _matmul([y], [(0, w_a, ()), (0, w_g, ())], epilogue=_ep_glu, out_dtype=BF16, tn=tnw, name="s5_glu")
    return _from_chunk_order(out.reshape(bsz, seq, w), chunk).reshape(n, w)


def _rope_table_body(pos_ref, inv_ref, cos_ref, sin_ref):
    ang = pos_ref[...].astype(F32) * inv_ref[...]
    cos_ref[...] = jnp.cos(ang)
    sin_ref[...] = jnp.sin(ang)


def _rope_tables(positions):
    n = positions.size
    half = MLA_ROPE // 2
    inv = ROPE_THETA ** (-jnp.arange(half, dtype=F32) / half)
    inv = jnp.zeros((1, LANE), F32).at[0, :half].set(inv)
    tm = _pick(n, (512, 256, 128, 64, 32, 16, 8))
    return pl.pallas_call(
        _rope_table_body,
        grid=(n // tm,),
        in_specs=[pl.BlockSpec((tm, 1), lambda i: (i, 0)), pl.BlockSpec((1, LANE), lambda i: (0, 0))],
        out_specs=[pl.BlockSpec((tm, LANE), lambda i: (i, 0))] * 2,
        out_shape=[jax.ShapeDtypeStruct((n, LANE), F32)] * 2,
        compiler_params=_params(("parallel",)),
        name="rope_table",
    )(positions.reshape(n, 1).astype(I32), inv)


def _head_norm_rope(t0, t1, t2, g0, g1, g2, cos, sin, true_dim):
    ss = (jnp.sum(t0 * t0, axis=-1, keepdims=True) + jnp.sum(t1 * t1, axis=-1, keepdims=True)
          + jnp.sum(t2 * t2, axis=-1, keepdims=True))
    inv = lax.rsqrt(ss * (1.0 / true_dim) + EPS)
    n0 = t0 * inv * g0
    n1 = t1 * inv * g1
    n2 = t2 * inv * g2
    o1 = n1 * cos - n2 * sin
    o2 = n1 * sin + n2 * cos
    half = MLA_ROPE // 2
    lane = lax.broadcasted_iota(I32, o1.shape, 1)
    o1 = jnp.where(lane < half, o1, 0.0)
    o2 = jnp.where(lane < half, o2, 0.0)
    return jnp.concatenate([n0, o1 + pltpu.roll(o2, half, 1)], axis=-1)


def _mla_qkv_body(cq_ref, ckv_ref, pe_ref, gq_lat_ref, gkv_lat_ref, wq_ref, wkv_ref, qg_ref, kg_ref, cos_ref, sin_ref,
                  q_ref, k_ref, v_ref):
    def lat_norm(ref, g_ref):
        t = ref[...]
        ms = jnp.mean(t * t, axis=-1, keepdims=True)
        return (t * lax.rsqrt(ms + EPS) * g_ref[...]).astype(BF16)

    cqn = lat_norm(cq_ref, gq_lat_ref)
    ckvn = lat_norm(ckv_ref, gkv_lat_ref)
    cos, sin = cos_ref[...], sin_ref[...]
    hw = 3 * LANE
    qg = (qg_ref[:, 0:LANE], qg_ref[:, LANE:2 * LANE], qg_ref[:, 2 * LANE:hw])
    kg = (kg_ref[:, 0:LANE], kg_ref[:, LANE:2 * LANE], kg_ref[:, 2 * LANE:hw])
    pe1 = pe_ref[:, 0:LANE]
    pe2 = pe_ref[:, LANE:2 * LANE]
    kvw = MLA_NOPE + MLA_V
    qk = MLA_NOPE + MLA_ROPE
    for h in range(MLA_HEADS):
        qh = jnp.dot(cqn, wq_ref[:, h * hw:(h + 1) * hw], preferred_element_type=F32)
        qh = _head_norm_rope(qh[:, 0:LANE], qh[:, LANE:2 * LANE], qh[:, 2 * LANE:hw], *qg, cos, sin, qk)
        q_ref[0, h] = (qh * (qk ** -0.5 * LOG2E)).astype(q_ref.dtype)
        kvh = jnp.dot(ckvn, wkv_ref[:, h * kvw:(h + 1) * kvw], preferred_element_type=F32)
        k_ref[0, h] = _head_norm_rope(kvh[:, 0:MLA_NOPE], pe1, pe2, *kg, cos, sin, qk).astype(k_ref.dtype)
        v_ref[0, h] = kvh[:, MLA_NOPE:kvw].astype(v_ref.dtype)


def _pad_rope_gain(gain):
    half = MLA_ROPE // 2
    out = jnp.zeros((3 * LANE,), F32)
    out = out.at[:MLA_NOPE].set(gain[:MLA_NOPE].astype(F32))
    out = out.at[LANE:LANE + half].set(gain[MLA_NOPE:MLA_NOPE + half].astype(F32))
    out = out.at[2 * LANE:2 * LANE + half].set(gain[MLA_NOPE + half:].astype(F32))
    return out.reshape(1, 3 * LANE)


def _flash_body(q_ref, k_ref, v_ref, o_ref, *, tq, wide):
    qi = pl.program_id(2)
    dv = v_ref.shape[-1]
    q = q_ref[0, 0]

    def step(off, width, carry, masked):
        m, l, acc = carry
        k = k_ref[0, 0, pl.ds(off, width), :]
        v = v_ref[0, 0, pl.ds(off, width), :]
        s = lax.dot_general(q, k, (((1,), (1,)), ((), ())), preferred_element_type=F32)
        if masked:
            row = lax.broadcasted_iota(I32, (tq, width), 0)
            col = lax.broadcasted_iota(I32, (tq, width), 1)
            s = jnp.where(col <= row, s, NEG_BIG)
        m_new = jnp.maximum(m, jnp.max(s, axis=1, keepdims=True))
        alpha = jnp.exp2(m - m_new)
        p = jnp.exp2(s - m_new)
        l = alpha * l + jnp.sum(p, axis=1, keepdims=True)
        acc = alpha * acc + jnp.dot(p.astype(BF16), v, preferred_element_type=F32)
        return m_new, l, acc

    per = wide // tq
    n_wide = qi // per
    carry = (jnp.full((tq, 1), NEG_BIG, F32), jnp.zeros((tq, 1), F32), jnp.zeros((tq, dv), F32))
    carry = lax.fori_loop(0, n_wide, lambda j, c: step(pl.multiple_of(j * wide, wide), wide, c, False), carry)
    carry = lax.fori_loop(n_wide * per, qi, lambda j, c: step(pl.multiple_of(j * tq, tq), tq, c, False), carry)
    m, l, acc = step(pl.multiple_of(qi * tq, tq), tq, carry, True)
    o_ref[0] = (acc / l).astype(o_ref.dtype)


def _flash_attention(q, k, v):
    b, h, s, dk = q.shape
    dv = v.shape[-1]
    tq = _pick(s, (1024, 512, 256, 128))
    wide = _pick(s, (2 * tq, tq))
    return pl.pallas_call(
        functools.partial(_flash_body, tq=tq, wide=wide),
        grid=(b, h, s // tq),
        in_specs=[pl.BlockSpec((1, 1, tq, dk), lambda bi, hi, qi: (bi, hi, qi, 0)),
                  pl.BlockSpec((1, 1, s, dk), lambda bi, hi, qi: (bi, hi, 0, 0)),
                  pl.BlockSpec((1, 1, s, dv), lambda bi, hi, qi: (bi, hi, 0, 0))],
        out_specs=pl.BlockSpec((1, tq, dv), lambda bi, hi, qi: (bi, qi, hi)),
        out_shape=jax.ShapeDtypeStruct((b, s, h * dv), BF16),
        compiler_params=_params(("parallel", "parallel", "arbitrary")),
        name="mla_flash",
    )(q, k, v)


def _mla_mixer(hm, col, cos, sin, bsz, seq, q_lora, kv_lora, q_lat_gain, kv_lat_gain, w_uq, w_ukv, q_gain, k_gain):
    n = hm.shape[0]
    half = MLA_ROPE // 2
    qk = MLA_NOPE + MLA_ROPE
    wq = w_uq.reshape(q_lora, MLA_HEADS, qk)
    wq_p = jnp.zeros((q_lora, MLA_HEADS, 3 * LANE), BF16)
    wq_p = wq_p.at[:, :, :MLA_NOPE].set(wq[:, :, :MLA_NOPE].astype(BF16))
    wq_p = wq_p.at[:, :, LANE:LANE + half].set(wq[:, :, MLA_NOPE:MLA_NOPE + half].astype(BF16))
    wq_p = wq_p.at[:, :, 2 * LANE:2 * LANE + half].set(wq[:, :, MLA_NOPE + half:].astype(BF16))
    wq_p = wq_p.reshape(q_lora, MLA_HEADS * 3 * LANE)
    wkv = w_ukv.astype(BF16)
    tm = _pick(seq, (256, 128, 64, 32, 16))
    nblk = seq // tm
    cols = lambda width, block: pl.BlockSpec((tm, width), lambda bi, si: (bi * nblk + si, block))
    const = lambda arr: pl.BlockSpec(arr.shape, lambda bi, si: (0, 0), pipeline_mode=pl.Buffered(1))
    heads = lambda width: pl.BlockSpec((1, MLA_HEADS, tm, width), lambda bi, si: (bi, 0, si, 0))
    gq_lat = q_lat_gain.reshape(1, q_lora).astype(F32)
    gkv_lat = kv_lat_gain.reshape(1, kv_lora).astype(F32)
    qg = _pad_rope_gain(q_gain)
    kg = _pad_rope_gain(k_gain)
    q, k, v = pl.pallas_call(
        _mla_qkv_body,
        grid=(bsz, nblk),
        in_specs=[cols(q_lora, col["c_q"] // q_lora), cols(kv_lora, col["c_kv"] // kv_lora),
                  cols(2 * LANE, col["pe"] // (2 * LANE)),
                  const(gq_lat), const(gkv_lat), const(wq_p), const(wkv), const(qg), const(kg),
                  cols(LANE, 0), cols(LANE, 0)],
        out_specs=[heads(2 * LANE), heads(2 * LANE), heads(MLA_V)],
        out_shape=[jax.ShapeDtypeStruct((bsz, MLA_HEADS, seq, 2 * LANE), BF16),
                   jax.ShapeDtypeStruct((bsz, MLA_HEADS, seq, 2 * LANE), BF16),
                   jax.ShapeDtypeStruct((bsz, MLA_HEADS, seq, MLA_V), BF16)],
        compiler_params=_params(("parallel", "parallel")),
        name="mla_qkv",
    )(hm, hm, hm, gq_lat, gkv_lat, wq_p, wkv, qg, kg, cos, sin)
    return _flash_attention(q, k, v).reshape(n, MLA_HEADS * MLA_V)


def _dsa_prep_body(hd_q_ref, hd_k_ref, hd_v_ref, hi_q_ref, hi_k_ref, hi_w_ref, qg_ref, kg_ref,
                   q_ref, k_ref, v_ref, qi_ref, klo_ref, khi_ref, w_ref):
    def norm(t, g):
        ms = jnp.mean(t * t, axis=-1, keepdims=True)
        return t * lax.rsqrt(ms + EPS) * g

    qg = qg_ref[...]
    for h in range(DSA_HEADS):
        qh = norm(hd_q_ref[:, h * DSA_HEAD_DIM:(h + 1) * DSA_HEAD_DIM], qg)
        q_ref[0, h] = (qh * (DSA_HEAD_DIM ** -0.5 * LOG2E)).astype(q_ref.dtype)
    k_ref[0] = norm(hd_k_ref[...], kg_ref[...]).astype(k_ref.dtype)
    v_ref[0] = hd_v_ref[...].astype(v_ref.dtype)
    qi_ref[0] = hi_q_ref[...].astype(qi_ref.dtype)
    ki = hi_k_ref[...]
    lane = lax.broadcasted_iota(I32, ki.shape, 1)
    ki = jnp.where(lane < IDX_DIM, ki, 0.0)
    klo_ref[0] = ki.astype(klo_ref.dtype)
    khi_ref[0] = pltpu.roll(ki, IDX_DIM, 1).astype(khi_ref.dtype)
    w_ref[0] = hi_w_ref[...] * ((IDX_HEADS ** -0.5) * (IDX_DIM ** -0.5))


def _float_key(x):
    bits = pltpu.bitcast(x, I32)
    return jnp.where(bits < 0, bits ^ jnp.int32(0x7FFFFFFF), bits)


def _dsa_body(q_ref, k_ref, v_ref, qi_ref, klo_ref, khi_ref, w_ref, o_ref, keys_ref, gmax_ref, acc_ref,
              *, tq, tk, wide, n_sel):
    qb = pl.program_id(1)
    n_kv = (qb * tq + tq + tk - 1) // tk
    row = lax.broadcasted_iota(I32, (tq, tk), 0) + qb * tq
    col0 = lax.broadcasted_iota(I32, (tq, tk), 1)

    w = w_ref[0]

    def score_tile(j, _):
        off = pl.multiple_of(j * tk, tk)
        klo = klo_ref[0, pl.ds(off, tk), :]
        khi = khi_ref[0, pl.ds(off, tk), :]
        acc = jnp.zeros((tq, tk), F32)
        for hp in range(IDX_HEADS // 2):
            q2 = qi_ref[0, :, hp * LANE:(hp + 1) * LANE]
            for half, kk in ((0, klo), (1, khi)):
                h = 2 * hp + half
                logit = lax.dot_general(q2, kk, (((1,), (1,)), ((), ())), preferred_element_type=F32)
                acc = acc + w[:, h:h + 1] * jnp.maximum(logit, 0.0)
        causal = col0 + off <= row
        keys_ref[:, pl.ds(off, tk)] = jnp.where(causal, _float_key(acc + 0.0), INT_MIN)
        sc = jnp.where(causal, acc, -jnp.inf)
        for c in range(tk // LANE):
            gmax_ref[c % 2] = jnp.maximum(gmax_ref[c % 2], sc[:, c * LANE:(c + 1) * LANE])
        return 0

    gmax_ref[...] = jnp.full(gmax_ref.shape, -jnp.inf, F32)
    lax.fori_loop(0, n_kv, score_tile, 0)

    rc = min(tq, 128)

    def row_group(g, carry):
        r0 = pl.multiple_of(g * rc, rc)
        ge = gmax_ref[0, pl.ds(r0, rc), :]
        go = gmax_ref[1, pl.ds(r0, rc), :]
        key_l = _float_key(jnp.min(jnp.minimum(ge, go), axis=1, keepdims=True) + 0.0)
        key_u = _float_key(jnp.max(jnp.maximum(ge, go), axis=1, keepdims=True) + 0.0)
        shared = lax.clz(key_l ^ key_u)
        start = jnp.min(shared)
        mask = jnp.where(start > 0, lax.shift_left(jnp.int32(-1), 32 - jnp.maximum(start, 1)), 0)
        lo_init = ((key_l ^ INT_MIN) & mask) ^ INT_MIN
        lo_init = jnp.broadcast_to(lo_init, (rc, LANE))

        def bit_step(b, lo):
            cand = lo + lax.shift_left(jnp.int32(1), 31 - b)

            def count_tile(j, part):
                off = pl.multiple_of(j * tk, tk)
                for c in range(tk // LANE):
                    kc = keys_ref[pl.ds(r0, rc), pl.ds(off + c * LANE, LANE)]
                    part = part + jnp.where(kc >= cand, 1.0, 0.0)
                return part

            part = lax.fori_loop(0, n_kv, count_tile, jnp.zeros((rc, LANE), F32))
            cnt = jnp.sum(part, axis=1, keepdims=True)
            return jnp.where(cnt >= n_sel, cand, lo)

        thr = lax.fori_loop(start, 32, bit_step, lo_init)
        thr = jnp.maximum(thr, INT_MIN + 1)

        def bias_tile(j, c2):
            off = pl.multiple_of(j * tk, tk)
            for c in range(tk // LANE):
                kc = keys_ref[pl.ds(r0, rc), pl.ds(off + c * LANE, LANE)]
                bias = jnp.where(kc >= thr, 0.0, NEG_BIG)
                keys_ref[pl.ds(r0, rc), pl.ds(off + c * LANE, LANE)] = pltpu.bitcast(bias, I32)
            return c2

        lax.fori_loop(0, n_kv, bias_tile, 0)
        return carry

    lax.fori_loop(0, tq // rc, row_group, 0)

    per = wide // tk
    n_wide = n_kv // per

    def head(h, carry):
        q = q_ref[0, h]

        def step(off, width, st):
            m, l, acc = st
            k = k_ref[0, pl.ds(off, width), :]
            v = v_ref[0, pl.ds(off, width), :]
            bias = pltpu.bitcast(keys_ref[:, pl.ds(off, width)], F32)
            s = lax.dot_general(q, k, (((1,), (1,)), ((), ())), preferred_element_type=F32) + bias
            m_new = jnp.maximum(m, jnp.max(s, axis=1, keepdims=True))
            alpha = jnp.exp2(m - m_new)
            p = jnp.exp2(s - m_new)
            l = alpha * l + jnp.sum(p, axis=1, keepdims=True)
            acc = alpha * acc + jnp.dot(p.astype(BF16), v, preferred_element_type=F32)
            return m_new, l, acc

        st = (jnp.full((tq, 1), NEG_BIG, F32), jnp.zeros((tq, 1), F32), jnp.zeros((tq, DSA_HEAD_DIM), F32))
        st = lax.fori_loop(0, n_wide, lambda j, c: step(pl.multiple_of(j * wide, wide), wide, c), st)
        st = lax.fori_loop(n_wide * per, n_kv, lambda j, c: step(pl.multiple_of(j * tk, tk), tk, c), st)
        m, l, acc = st
        acc_ref[h] = acc / l
        return carry

    lax.fori_loop(0, DSA_HEADS, head, 0)
    for h in range(DSA_HEADS):
        o_ref[0, :, h * DSA_HEAD_DIM:(h + 1) * DSA_HEAD_DIM] = acc_ref[h].astype(o_ref.dtype)


def _dsa_mixer(hd, col, bsz, seq, q_gain, k_gain):
    n = hd.shape[0]
    qw = DSA_HEADS * DSA_HEAD_DIM
    iw = IDX_HEADS * IDX_DIM
    tm = _pick(seq, (256, 128, 64, 32, 16))
    nblk = seq // tm

    def cols(width, block):
        return pl.BlockSpec((tm, width), lambda bi, si: (bi * nblk + si, block))

    per_tok = lambda width: pl.BlockSpec((1, tm, width), lambda bi, si: (bi, si, 0))
    q, k, v, qi, klo, khi, w = pl.pallas_call(
        _dsa_prep_body,
        grid=(bsz, nblk),
        in_specs=[cols(qw, col["q_c"] // qw), cols(LANE, col["k_c"] // LANE), cols(LANE, col["v_c"] // LANE),
                  cols(iw, col["q_i"] // iw), cols(LANE, col["k_i"] // LANE), cols(LANE, col["w_i"] // LANE),
                  pl.BlockSpec((1, DSA_HEAD_DIM), lambda bi, si: (0, 0)),
                  pl.BlockSpec((1, DSA_HEAD_DIM), lambda bi, si: (0, 0))],
        out_specs=[pl.BlockSpec((1, DSA_HEADS, tm, DSA_HEAD_DIM), lambda bi, si: (bi, 0, si, 0)),
                   per_tok(DSA_HEAD_DIM), per_tok(DSA_HEAD_DIM), per_tok(iw), per_tok(LANE), per_tok(LANE),
                   per_tok(LANE)],
        out_shape=[jax.ShapeDtypeStruct((bsz, DSA_HEADS, seq, DSA_HEAD_DIM), BF16),
                   jax.ShapeDtypeStruct((bsz, seq, DSA_HEAD_DIM), BF16),
                   jax.ShapeDtypeStruct((bsz, seq, DSA_HEAD_DIM), BF16),
                   jax.ShapeDtypeStruct((bsz, seq, iw), BF16),
                   jax.ShapeDtypeStruct((bsz, seq, LANE), BF16),
                   jax.ShapeDtypeStruct((bsz, seq, LANE), BF16),
                   jax.ShapeDtypeStruct((bsz, seq, LANE), F32)],
        compiler_params=_params(("parallel", "parallel")),
        name="dsa_prep",
    )(hd, hd, hd, hd, hd, hd, q_gain.reshape(1, -1).astype(F32), k_gain.reshape(1, -1).astype(F32))

    tq = _pick(seq, (512, 256, 128))
    tk = _pick(seq, (512, 256, 128))
    n_sel = min(DSA_TOPK_MAX, seq // 4)
    assert n_sel <= 2 * LANE
    whole = lambda width: pl.BlockSpec((1, seq, width), lambda bi, qb: (bi, 0, 0), pipeline_mode=pl.Buffered(1))
    out = pl.pallas_call(
        functools.partial(_dsa_body, tq=tq, tk=tk, wide=_pick(seq, (4 * tk, 2 * tk, tk)), n_sel=n_sel),
        grid=(bsz, seq // tq),
        in_specs=[pl.BlockSpec((1, DSA_HEADS, tq, DSA_HEAD_DIM), lambda bi, qb: (bi, 0, qb, 0)),
                  whole(DSA_HEAD_DIM), whole(DSA_HEAD_DIM),
                  pl.BlockSpec((1, tq, iw), lambda bi, qb: (bi, qb, 0)),
                  whole(LANE), whole(LANE),
                  pl.BlockSpec((1, tq, LANE), lambda bi, qb: (bi, qb, 0))],
        out_specs=pl.BlockSpec((1, tq, qw), lambda bi, qb: (bi, qb, 0)),
        out_shape=jax.ShapeDtypeStruct((bsz, seq, qw), BF16),
        scratch_shapes=[pltpu.VMEM((tq, seq), I32), pltpu.VMEM((2, tq, LANE), F32),
                        pltpu.VMEM((DSA_HEADS, tq, DSA_HEAD_DIM), F32)],
        compiler_params=_params(("parallel", "arbitrary")),
        name="dsa_attention",
    )(q, k, v, qi, klo, khi, w)
    return out.reshape(n, qw)


def _pad_cols(w, width):
    return jnp.pad(w, ((0, 0), (0, width - w.shape[1])))


def _hybrid_mixer(x2, cos, sin, bsz, seq, norm_gain, w_in, q_lat_gain, kv_lat_gain, w_uq, w_ukv, mla_q_gain,
                  mla_k_gain, dsa_q_gain, dsa_k_gain, lam_re, lam_im, log_dt, b_re, b_im, c_re, c_im, d_skip, w_glu,
                  w_branch, w_out, layer):
    n, d = x2.shape
    ssm_w = d_skip.shape[-1]
    q_lora = q_lat_gain.shape[-1]
    kv_lora = kv_lat_gain.shape[-1]
    half = MLA_ROPE // 2
    dsa_w = DSA_HEADS * DSA_HEAD_DIM
    idx_w = IDX_HEADS * IDX_DIM
    sizes = (ssm_w, q_lora, kv_lora, MLA_ROPE, dsa_w, DSA_HEAD_DIM, DSA_HEAD_DIM, idx_w, IDX_DIM, IDX_HEADS, 3 * d)
    offs = [0]
    for s in sizes:
        offs.append(offs[-1] + s)
    seg = lambda i: w_in[:, offs[i]:offs[i + 1]]
    pe = seg(3)
    tn = _pick(d, (512, 256, 128))
    pe_pad = jnp.concatenate([_pad_cols(pe[:, :half], LANE), _pad_cols(pe[:, half:], LANE)], axis=1)
    parts = [("q_c", seg(4), dsa_w), ("c_kv", seg(2), kv_lora), ("u", seg(0), LANE), ("c_q", seg(1), q_lora),
             ("q_i", seg(7), idx_w), ("pe", pe_pad, 2 * LANE), ("k_c", seg(5), LANE), ("v_c", seg(6), LANE),
             ("k_i", _pad_cols(seg(8), LANE), LANE), ("w_i", _pad_cols(seg(9), LANE), LANE)]
    col = {}
    at = 0
    for name, part, block_width in parts:
        assert at % block_width == 0, (name, at, block_width)
        col[name] = at
        at += part.shape[1]
    gap = -at % tn
    col["gates"] = at + gap
    w_cat = jnp.concatenate([part for _, part, _ in parts] + [jnp.zeros((d, gap), F32), seg(10)],
                            axis=1).astype(BF16)
    h = _matmul([x2], [(0, w_cat, ())], epilogue=_ep_plain, out_dtype=F32, norm_gain=norm_gain,
                tn=_pick(w_cat.shape[1], (768, 512, 256, 128)), name="in_proj")

    u = h[:, col["u"]:col["u"] + ssm_w]
    y_ssm = _s5_mixer(u, bsz, seq, lam_re, lam_im, log_dt, b_re, b_im, c_re, c_im, d_skip, w_glu)
    y_mla = _mla_mixer(h, col, cos, sin, bsz, seq, q_lora, kv_lora, q_lat_gain, kv_lat_gain, w_uq, w_ukv,
                       mla_q_gain, mla_k_gain)
    y_dsa = _dsa_mixer(h, col, bsz, seq, dsa_q_gain, dsa_k_gain)

    br0 = ssm_w
    br1 = ssm_w + MLA_HEADS * MLA_V
    g0 = col["gates"] // tn
    merged = _matmul([y_ssm, y_mla, y_dsa],
                     [(0, w_branch[:br0].astype(BF16), ()), (1, w_branch[br0:br1].astype(BF16), ()),
                      (2, w_branch[br1:].astype(BF16), ())],
                     epilogue=_ep_branches, out_dtype=BF16, tn=tn,
                     extras=[(h, "mn", g0), (h, "mn", g0 + d // tn), (h, "mn", g0 + 2 * (d // tn))],
                     name="branch_merge")
    return _matmul([merged], [(0, w_out, (layer,))], epilogue=_ep_residual, out_dtype=F32, tn=tn,
                   extras=[(x2, "mn", 0)], name="mixer_out")


def _dense_ffn(x2, gain, w_gate, w_up, w_down, idx):
    d = x2.shape[1]
    dff = w_gate.shape[-1]
    xn = _rmsnorm(x2, gain)
    tn = _pick(dff, (256, 128))
    h = _matmul([xn], [(0, w_gate, (idx,)), (0, w_up, (idx,))], epilogue=_ep_swiglu, out_dtype=BF16, tn=tn,
                name="ffn_up")
    tnd = _pick(d, (256, 128))
    n_slabs = 2 if dff % (2 * LANE) == 0 and dff > 8192 else 1
    out = x2
    for s in range(n_slabs):
        out = _matmul([h], [(0, w_down, (idx,))], epilogue=_ep_residual, out_dtype=F32, tn=tnd,
                      k_slice=(dff // n_slabs, s), extras=[(out, "mn", 0)], name="ffn_down")
    return out


def _router_body(x_ref, g_ref, w_ref, xn_ref, meta_ref, cnt_ref, run_ref, *, n_experts):
    i = pl.program_id(0)

    @pl.when(i == 0)
    def _():
        run_ref[...] = jnp.zeros_like(run_ref)

    x = x_ref[...]
    tm = x.shape[0]
    ms = jnp.mean(x * x, axis=-1, keepdims=True)
    xn = x * lax.rsqrt(ms + EPS) * g_ref[...]
    xn_ref[...] = xn.astype(xn_ref.dtype)
    logits = jnp.dot(xn, w_ref[...], preferred_element_type=F32, precision=lax.Precision.HIGHEST)
    lane = lax.broadcasted_iota(I32, logits.shape, 1).astype(F32)
    lg = jnp.where(lane < n_experts, logits, -jnp.inf)
    m1 = jnp.max(lg, axis=1, keepdims=True)
    i1 = jnp.min(jnp.where(lg == m1, lane, float(LANE)), axis=1, keepdims=True)
    lg2 = jnp.where(lane == i1, -jnp.inf, lg)
    m2 = jnp.max(lg2, axis=1, keepdims=True)
    i2 = jnp.min(jnp.where(lg2 == m2, lane, float(LANE)), axis=1, keepdims=True)
    e2 = jnp.exp(m2 - m1)
    den = 1.0 + e2
    oh1 = jnp.where(lane == i1, 1.0, 0.0)
    oh2 = jnp.where(lane == i2, 1.0, 0.0)
    oh = oh1 + oh2
    r = lax.broadcasted_iota(I32, (tm, tm), 0)
    c = lax.broadcasted_iota(I32, (tm, tm), 1)
    tri = jnp.where(c < r, 1.0, 0.0).astype(BF16)
    base = run_ref[0:1, :]
    rank = jnp.dot(tri, oh.astype(BF16), preferred_element_type=F32) + base
    r1 = jnp.sum(oh1 * rank, axis=1, keepdims=True)
    r2 = jnp.sum(oh2 * rank, axis=1, keepdims=True)
    total = base + jnp.sum(oh, axis=0, keepdims=True)
    run_ref[0:1, :] = total
    cnt_ref[...] = jnp.broadcast_to(total, cnt_ref.shape)
    meta = jnp.zeros_like(logits)
    for k, val in enumerate((i1, i2, r1, r2, 1.0 / den, e2 / den)):
        meta = jnp.where(lane == float(k), val, meta)
    meta_ref[...] = meta


def _gather_rows_body(src_ref, x_hbm, o_ref, sem, *, rows):
    base = pl.program_id(0) * rows

    def issue(r, carry):
        pltpu.make_async_copy(x_hbm.at[src_ref[base + r]], o_ref.at[r], sem).start()
        return carry

    lax.fori_loop(0, rows, issue, 0)
    pltpu.make_async_copy(x_hbm.at[pl.ds(0, rows)], o_ref, sem).wait()


def _gather_rows(x, src, *, rows):
    n_out = src.shape[0]
    n, d = x.shape
    x3 = x.reshape(n, d // LANE, LANE)
    out = pl.pallas_call(
        functools.partial(_gather_rows_body, rows=rows),
        grid_spec=pltpu.PrefetchScalarGridSpec(
            num_scalar_prefetch=1,
            grid=(n_out // rows,),
            in_specs=[pl.BlockSpec(memory_space=pl.ANY)],
            out_specs=pl.BlockSpec((rows, d // LANE, LANE), lambda i, src_ref: (i, 0, 0)),
            scratch_shapes=[pltpu.SemaphoreType.DMA(())],
        ),
        out_shape=jax.ShapeDtypeStruct((n_out, d // LANE, LANE), x.dtype),
        compiler_params=_params(("arbitrary",)),
        name="moe_gather",
    )(src, x3)
    return out.reshape(n_out, d)


def _experts_body(tile_ref, wexp_ref, lo_ref, hi_ref, first_ref, xs_ref, w1_ref, w3_ref, w2_ref, o_ref, *, tm):
    v = pl.program_id(0)
    f = pl.program_id(1)

    @pl.when((f == 0) & (first_ref[v] == 1))
    def _():
        o_ref[...] = jnp.zeros_like(o_ref)

    lo = lo_ref[v]
    hi = hi_ref[v]

    @pl.when(hi > lo)
    def _():
        xs = xs_ref[...]
        g = jnp.dot(xs, w1_ref[...], preferred_element_type=F32)
        u = jnp.dot(xs, w3_ref[...], preferred_element_type=F32)
        row = tile_ref[v] * tm + lax.broadcasted_iota(I32, (tm, 1), 0)
        h = jnp.where((row >= lo) & (row < hi), g * _sigmoid(g) * u, 0.0)
        o_ref[...] += jnp.dot(h.astype(BF16), w2_ref[...], preferred_element_type=F32)


def _combine_body(d1_ref, d2_ref, y_hbm, x_ref, w1_ref, w2_ref, o_ref, b1_ref, b2_ref, sem, *, rows):
    base = pl.program_id(0) * rows

    def issue(r, carry):
        pltpu.make_async_copy(y_hbm.at[d1_ref[base + r]], b1_ref.at[r], sem.at[0]).start()
        pltpu.make_async_copy(y_hbm.at[d2_ref[base + r]], b2_ref.at[r], sem.at[1]).start()
        return carry

    lax.fori_loop(0, rows, issue, 0)
    pltpu.make_async_copy(y_hbm.at[pl.ds(0, rows)], b1_ref, sem.at[0]).wait()
    pltpu.make_async_copy(y_hbm.at[pl.ds(0, rows)], b2_ref, sem.at[1]).wait()
    o_ref[...] = x_ref[...] + w1_ref[...] * b1_ref[...] + w2_ref[...] * b2_ref[...]


def _moe_ffn(x2, gain, w_router, w1, w3, w2, idx):
    n, d = x2.shape
    n_experts = w_router.shape[-1]
    dff = w1.shape[-1]
    tr = _pick(n, (256, 128, 64, 32, 16, 8))
    wr = _pad_cols(w_router[idx].astype(F32), LANE)
    xn, meta, cnt = pl.pallas_call(
        functools.partial(_router_body, n_experts=n_experts),
        grid=(n // tr,),
        in_specs=[pl.BlockSpec((tr, d), lambda i: (i, 0)), pl.BlockSpec((1, d), lambda i: (0, 0)),
                  pl.BlockSpec((d, LANE), lambda i: (0, 0))],
        out_specs=[pl.BlockSpec((tr, d), lambda i: (i, 0)), pl.BlockSpec((tr, LANE), lambda i: (i, 0)),
                   pl.BlockSpec((SUBLANE, LANE), lambda i: (0, 0))],
        out_shape=[jax.ShapeDtypeStruct((n, d), BF16), jax.ShapeDtypeStruct((n, LANE), F32),
                   jax.ShapeDtypeStruct((SUBLANE, LANE), F32)],
        scratch_shapes=[pltpu.VMEM((SUBLANE, LANE), F32)],
        compiler_params=_params(("arbitrary",)),
        name="moe_router",
    )(x2, gain.reshape(1, d).astype(F32), wr)

    e1, e2 = meta[:, 0].astype(I32), meta[:, 1].astype(I32)
    counts = cnt[0, :n_experts].astype(I32)
    ends = jnp.cumsum(counts)
    off = ends - counts
    dest1 = off[e1] + meta[:, 2].astype(I32)
    dest2 = off[e2] + meta[:, 3].astype(I32)
    na = TOP_K_EXPERTS * n
    tok = jnp.arange(n, dtype=I32)
    src = jnp.zeros((na,), I32).at[dest1].set(tok).at[dest2].set(tok)

    xs = _gather_rows(xn, src, rows=_pick(na, (256, 128, 64, 32, 16, 8)))

    tm = _pick(na, (512, 256, 128, 64, 32, 16, 8))
    n_tiles = na // tm
    nv = n_tiles + n_experts - 1
    lo = jnp.sort(jnp.concatenate([jnp.arange(n_tiles, dtype=I32) * tm, ends[:-1]]))
    hi = jnp.concatenate([lo[1:], jnp.full((1,), na, I32)])
    tile = jnp.minimum(lo // tm, n_tiles - 1)
    expert = jnp.minimum(jnp.sum((lo[:, None] >= ends[None, :]).astype(I32), axis=1), n_experts - 1)
    vid = jnp.arange(nv, dtype=I32)
    last = lax.cummax(jnp.where(hi > lo, vid, -1))
    wexp = expert[jnp.maximum(last, 0)]
    first = jnp.concatenate([jnp.ones((1,), I32), (tile[1:] != tile[:-1]).astype(I32)])

    fc = _pick(dff, (512, 256, 128))
    nf = dff // fc

    def up_map(v, f, tile_ref, wexp_ref, lo_ref, hi_ref, first_ref):
        return wexp_ref[v], 0, jnp.where(hi_ref[v] > lo_ref[v], f, nf - 1)

    def down_map(v, f, tile_ref, wexp_ref, lo_ref, hi_ref, first_ref):
        return wexp_ref[v], jnp.where(hi_ref[v] > lo_ref[v], f, nf - 1), 0

    def row_map(v, f, tile_ref, wexp_ref, lo_ref, hi_ref, first_ref):
        return tile_ref[v], 0

    ys = pl.pallas_call(
        functools.partial(_experts_body, tm=tm),
        grid_spec=pltpu.PrefetchScalarGridSpec(
            num_scalar_prefetch=5,
            grid=(nv, nf),
            in_specs=[pl.BlockSpec((tm, d), row_map), pl.BlockSpec((None, d, fc), up_map),
                      pl.BlockSpec((None, d, fc), up_map), pl.BlockSpec((None, fc, d), down_map)],
            out_specs=pl.BlockSpec((tm, d), row_map, pipeline_mode=pl.Buffered(1)),
        ),
        out_shape=jax.ShapeDtypeStruct((na, d), F32),
        compiler_params=_params(("arbitrary", "arbitrary")),
        name="moe_experts",
    )(tile, wexp, lo, hi, first, xs, w1[idx].astype(BF16), w3[idx].astype(BF16), w2[idx].astype(BF16))

    rows = _pick(n, (256, 128, 64, 32, 16, 8))
    slab = (rows, d // LANE, LANE)
    slab_spec = pl.BlockSpec(slab, lambda i, d1, d2: (i, 0, 0))
    weight_spec = pl.BlockSpec((rows, 1, LANE), lambda i, d1, d2: (i, 0, 0))
    wt1 = jnp.broadcast_to(meta[:, 4][:, None, None], (n, 1, LANE))
    wt2 = jnp.broadcast_to(meta[:, 5][:, None, None], (n, 1, LANE))
    out = pl.pallas_call(
        functools.partial(_combine_body, rows=rows),
        grid_spec=pltpu.PrefetchScalarGridSpec(
            num_scalar_prefetch=2,
            grid=(n // rows,),
            in_specs=[pl.BlockSpec(memory_space=pl.ANY), slab_spec, weight_spec, weight_spec],
            out_specs=slab_spec,
            scratch_shapes=[pltpu.VMEM(slab, F32), pltpu.VMEM(slab, F32), pltpu.SemaphoreType.DMA((2,))],
        ),
        out_shape=jax.ShapeDtypeStruct((n, d // LANE, LANE), F32),
        compiler_params=_params(("arbitrary",)),
        name="moe_combine",
    )(dest1, dest2, ys.reshape(na, d // LANE, LANE), x2.reshape(n, d // LANE, LANE), wt1, wt2)
    return out.reshape(n, d)


def kernel(x, positions, mix_norm, w_in, mla_q_lat_norm, mla_kv_lat_norm, mla_w_uq, mla_w_ukv, mla_q_norm, mla_k_norm, dsa_q_norm, dsa_k_norm, ssm_lam_re, ssm_lam_im, ssm_log_dt, ssm_b_re, ssm_b_im, ssm_c_re, ssm_c_im, ssm_d, ssm_w_glu, w_branch, w_out, ffn_norm, dense_w_gate, dense_w_up, dense_w_down, moe_w_router, moe_w1, moe_w3, moe_w2):
    bsz, seq, d = x.shape
    depth = mix_norm.shape[0]
    x2 = x.reshape(bsz * seq, d)
    cos, sin = _rope_tables(positions)
    for layer in range(depth):
        x2 = _hybrid_mixer(x2, cos, sin, bsz, seq, mix_norm[layer], w_in[layer], mla_q_lat_norm[layer],
                           mla_kv_lat_norm[layer], mla_w_uq[layer], mla_w_ukv[layer], mla_q_norm[layer],
                           mla_k_norm[layer], dsa_q_norm[layer], dsa_k_norm[layer], ssm_lam_re[layer],
                           ssm_lam_im[layer], ssm_log_dt[layer], ssm_b_re[layer], ssm_b_im[layer], ssm_c_re[layer],
                           ssm_c_im[layer], ssm_d[layer], ssm_w_glu[layer], w_branch[layer], w_out, layer)
        i = layer // 2
        if layer % 2 == 0:
            x2 = _dense_ffn(x2, ffn_norm[layer], dense_w_gate, dense_w_up, dense_w_down, i)
        else:
            x2 = _moe_ffn(x2, ffn_norm[layer], moe_w_router, moe_w1, moe_w3, moe_w2, i)
    return x2.reshape(bsz, seq, d)
```

```python
import functools
import math

import jax
import jax.numpy as jnp
from jax import lax
from jax.experimental import pallas as pl
from jax.experimental.pallas import tpu as pltpu

F32 = jnp.float32
BF16 = jnp.bfloat16
I32 = jnp.int32

SSM_GROUP = 16
SSM_STATE = 64
MLA_HEADS = 12
MLA_NOPE = 128
MLA_ROPE = 64
MLA_V = 128
DSA_HEADS = 12
DSA_HEAD_DIM = 128
IDX_HEADS = 16
IDX_DIM = 64
DSA_TOPK_MAX = 256
TOP_K_EXPERTS = 2
ROPE_THETA = 10000.0
EPS = 1e-6

LANE = 128
SUBLANE = 8
VMEM_LIMIT_BYTES = 56 * 1024 * 1024
NEG_BIG = -1e30
INT_MIN = -(2 ** 31)
LOG2E = 1.4426950408889634


def _params(semantics):
    return pltpu.CompilerParams(dimension_semantics=semantics, vmem_limit_bytes=VMEM_LIMIT_BYTES)


def _sigmoid(x):
    return 1.0 / (1.0 + jnp.exp(-x))


def _pick(n, candidates):
    for c in candidates:
        if n % c == 0:
            return c
    return n


def _rmsnorm_body(x_ref, g_ref, o_ref):
    x = x_ref[...].astype(F32)
    ms = jnp.mean(x * x, axis=-1, keepdims=True)
    o_ref[...] = (x * lax.rsqrt(ms + EPS) * g_ref[...]).astype(o_ref.dtype)


def _rmsnorm(x, gain, *, width=None, col_block=0, out_dtype=BF16):
    m = x.shape[0]
    width = width or x.shape[1]
    tm = _pick(m, (512, 256, 128, 64, 32, 16, 8))
    return pl.pallas_call(
        _rmsnorm_body,
        grid=(m // tm,),
        in_specs=[pl.BlockSpec((tm, width), lambda i: (i, col_block)),
                  pl.BlockSpec((1, width), lambda i: (0, 0))],
        out_specs=pl.BlockSpec((tm, width), lambda i: (i, 0)),
        out_shape=jax.ShapeDtypeStruct((m, width), out_dtype),
        compiler_params=_params(("parallel",)),
        name="rmsnorm",
    )(x, gain.reshape(1, width).astype(F32))


def _mm_body(*refs, n_a, pairs, n_extra, epilogue, normed):
    a_refs = refs[:n_a]
    w_refs = refs[n_a:n_a + len(pairs)]
    e_refs = refs[n_a + len(pairs):n_a + len(pairs) + n_extra]
    if normed:
        g_ref, o_ref, xn_ref = refs[n_a + len(pairs) + n_extra:]
    else:
        o_ref = refs[n_a + len(pairs) + n_extra]
    if normed:

        @pl.when(pl.program_id(1) == 0)
        def _():
            rows = a_refs[0].shape[0]
            step = math.gcd(rows, 128)

            def chunk(c, carry):
                r0 = pl.multiple_of(c * step, step)
                x = a_refs[0][pl.ds(r0, step), :].astype(F32)
                ms = jnp.mean(x * x, axis=-1, keepdims=True)
                xn_ref[pl.ds(r0, step), :] = (x * lax.rsqrt(ms + EPS) * g_ref[...]).astype(BF16)
                return carry

            lax.fori_loop(0, rows // step, chunk, 0)

        a_vals = [xn_ref[...]]
    else:
        a_vals = [a[...].astype(BF16) for a in a_refs]
    parts = [jnp.dot(a_vals[ai], w[...].astype(BF16), preferred_element_type=F32)
             for ai, w in zip(pairs, w_refs)]
    o_ref[...] = epilogue(parts, *[e[...] for e in e_refs]).astype(o_ref.dtype)


def _matmul(a_list, w_list, *, epilogue, out_dtype, tn, tm=None, k_slice=None, extras=(), norm_gain=None,
            name="matmul"):
    m = a_list[0].shape[0]
    n = w_list[0][1].shape[-1]
    tm = tm or _pick(m, (1024, 512, 256, 128, 64, 32, 16, 8))
    ksize, kidx = k_slice if k_slice else (None, 0)
    in_specs = []
    a_mode = {"pipeline_mode": pl.Buffered(1)} if norm_gain is not None else {}
    for a in a_list:
        in_specs.append(pl.BlockSpec((tm, ksize or a.shape[1]), lambda i, j: (i, kidx), **a_mode))
    for _, w, prefix in w_list:
        in_specs.append(pl.BlockSpec((None,) * len(prefix) + (ksize or w.shape[-2], tn),
                                     lambda i, j, prefix=prefix: tuple(prefix) + (kidx, j)))
    for arr, kind, arg in extras:
        if kind == "mn":
            in_specs.append(pl.BlockSpec((tm, tn), lambda i, j, arg=arg: (i, j + arg)))
        else:
            in_specs.append(pl.BlockSpec((1, tn), lambda i, j: (0, j)))
    pairs = tuple(ai for ai, _, _ in w_list)
    operands = [*a_list, *[w for _, w, _ in w_list], *[arr for arr, _, _ in extras]]
    scratch = []
    if norm_gain is not None:
        kdim = a_list[0].shape[1]
        in_specs.append(pl.BlockSpec((1, kdim), lambda i, j: (0, 0)))
        operands.append(norm_gain.reshape(1, kdim).astype(F32))
        scratch.append(pltpu.VMEM((tm, kdim), BF16))
    body = functools.partial(_mm_body, n_a=len(a_list), pairs=pairs, n_extra=len(extras), epilogue=epilogue,
                             normed=norm_gain is not None)
    return pl.pallas_call(
        body,
        grid=(m // tm, n // tn),
        in_specs=in_specs,
        out_specs=pl.BlockSpec((tm, tn), lambda i, j: (i, j)),
        out_shape=jax.ShapeDtypeStruct((m, n), out_dtype),
        scratch_shapes=scratch,
        compiler_params=_params(("parallel", "arbitrary")),
        name=name,
    )(*operands)


def _ep_plain(ps):
    return ps[0]


def _ep_residual(ps, res):
    return res + ps[0]


def _ep_swiglu(ps):
    g = ps[0]
    return g * _sigmoid(g) * ps[1]


def _ep_glu(ps):
    return ps[0] * _sigmoid(ps[1])


def _ep_branches(ps, g0, g1, g2):
    return _sigmoid(g0) * ps[0] + _sigmoid(g1) * ps[1] + _sigmoid(g2) * ps[2]


def _s5_body(u_ref, wre_ref, wim_ref, a_ref, cre_ref, cim_ref, d_ref, y_ref,
             bre_ref, bim_ref, xr_ref, xi_ref, car_ref, *, steps):
    t = pl.program_id(2)
    lw = bre_ref.shape[-1]

    @pl.when(t == 0)
    def _():
        car_ref[...] = jnp.zeros_like(car_ref)

    u = u_ref[0]
    ub = u.astype(BF16)
    bre_ref[...] = jnp.dot(ub, wre_ref[...], preferred_element_type=F32)
    bim_ref[...] = jnp.dot(ub, wim_ref[...], preferred_element_type=F32)

    ar = jnp.broadcast_to(a_ref[0:1, :], (SUBLANE, lw))
    ai = jnp.broadcast_to(a_ref[1:2, :], (SUBLANE, lw))

    def pass1(j, carry):
        xr, xi = carry
        off = pl.multiple_of(j * SUBLANE, SUBLANE)
        nr = ar * xr - ai * xi + bre_ref[pl.ds(off, SUBLANE), :]
        ni = ar * xi + ai * xr + bim_ref[pl.ds(off, SUBLANE), :]
        xr_ref[pl.ds(off, SUBLANE), :] = nr
        xi_ref[pl.ds(off, SUBLANE), :] = ni
        return nr, ni

    zero = jnp.zeros((SUBLANE, lw), F32)
    er, ei = lax.fori_loop(0, steps, pass1, (zero, zero))

    pr = a_ref[2:3, :]
    pi = a_ref[3:4, :]
    cr = car_ref[0:1, :]
    ci = car_ref[1:2, :]
    rows_r, rows_i = [], []
    for s in range(SUBLANE):
        rows_r.append(cr)
        rows_i.append(ci)
        nr = pr * cr - pi * ci + er[s:s + 1, :]
        ni = pr * ci + pi * cr + ei[s:s + 1, :]
        cr, ci = nr, ni
    car_ref[0:1, :] = cr
    car_ref[1:2, :] = ci
    cin_r = jnp.concatenate(rows_r, axis=0)
    cin_i = jnp.concatenate(rows_i, axis=0)

    def pass2(j, carry):
        cr_, ci_ = carry
        off = pl.multiple_of(j * SUBLANE, SUBLANE)
        nr = ar * cr_ - ai * ci_
        ni = ar * ci_ + ai * cr_
        xr_ref[pl.ds(off, SUBLANE), :] = xr_ref[pl.ds(off, SUBLANE), :] + nr
        xi_ref[pl.ds(off, SUBLANE), :] = xi_ref[pl.ds(off, SUBLANE), :] + ni
        return nr, ni

    lax.fori_loop(0, steps, pass2, (cin_r, cin_i))

    y = (jnp.dot(xr_ref[...].astype(BF16), cre_ref[...], preferred_element_type=F32)
         + jnp.dot(xi_ref[...].astype(BF16), cim_ref[...], preferred_element_type=F32) + d_ref[...] * u)
    y_ref[0] = jax.nn.gelu(y).astype(y_ref.dtype)


def _s5_core(u, bin_re, bin_im, a_tab, bout_re, bout_im, d_skip, *, chunk):
    b, s, w = u.shape
    ns, _, sw = bin_re.shape
    steps = chunk // SUBLANE
    lanes = pl.BlockSpec((1, chunk, LANE), lambda bi, li, ti: (bi, ti, li))
    slab_in = pl.BlockSpec((None, LANE, sw), lambda bi, li, ti: (li, 0, 0))
    slab_out = pl.BlockSpec((None, sw, LANE), lambda bi, li, ti: (li, 0, 0))
    return pl.pallas_call(
        functools.partial(_s5_body, steps=steps),
        grid=(b, ns, s // chunk),
        in_specs=[lanes, slab_in, slab_in, pl.BlockSpec((4, sw), lambda bi, li, ti: (0, li)), slab_out, slab_out,
                  pl.BlockSpec((1, LANE), lambda bi, li, ti: (0, li))],
        out_specs=lanes,
        out_shape=jax.ShapeDtypeStruct((b, s, w), BF16),
        scratch_shapes=[pltpu.VMEM((chunk, sw), F32)] * 4 + [pltpu.VMEM((SUBLANE, sw), F32)],
        compiler_params=_params(("parallel", "parallel", "arbitrary")),
        name="s5_core",
    )(u, bin_re, bin_im, a_tab, bout_re, bout_im, d_skip)


def _s5_chunk(s):
    return _pick(s, (512, 256, 128, 64))


def _to_chunk_order(t, chunk):
    b, s, w = t.shape
    return t.reshape(b, s // chunk, SUBLANE, chunk // SUBLANE, w).transpose(0, 1, 3, 2, 4).reshape(b, s, w)


def _from_chunk_order(t, chunk):
    b, s, w = t.shape
    return t.reshape(b, s // chunk, chunk // SUBLANE, SUBLANE, w).transpose(0, 1, 3, 2, 4).reshape(b, s, w)


def _s5_discretise(lam_re, lam_im, log_dt, b_re, b_im, c_re, c_im, steps):
    g, p = lam_re.shape
    h = b_re.shape[-1]
    lr, li = lam_re.astype(F32), lam_im.astype(F32)
    dt = jnp.exp(log_dt.astype(F32))[:, None]
    mag = jnp.exp(lr * dt)
    ab_re, ab_im = mag * jnp.cos(li * dt), mag * jnp.sin(li * dt)
    den = lr * lr + li * li
    nr = ab_re - 1.0
    zr = (nr * lr + ab_im * li) / den
    zi = (ab_im * lr - nr * li) / den
    bb_re = zr[..., None] * b_re - zi[..., None] * b_im
    bb_im = zr[..., None] * b_im + zi[..., None] * b_re
    gs = LANE // h
    ns = g // gs
    eye = jnp.eye(gs, dtype=F32)
    slab_in = lambda t: jnp.einsum("sgph,gk->sghkp", t.reshape(ns, gs, p, h), eye).reshape(ns, gs * h, gs * p)
    slab_out = lambda t: jnp.einsum("sghp,gk->sgpkh", t.reshape(ns, gs, h, p), eye).reshape(ns, gs * p, gs * h)
    pr, pi = ab_re, ab_im
    for _ in range(int(round(math.log2(steps)))):
        pr, pi = pr * pr - pi * pi, 2.0 * pr * pi
    a_tab = jnp.stack([ab_re.reshape(-1), ab_im.reshape(-1), pr.reshape(-1), pi.reshape(-1)])
    return (slab_in(bb_re).astype(BF16), slab_in(bb_im).astype(BF16), slab_out(c_re.astype(F32)).astype(BF16),
            slab_out(-c_im.astype(F32)).astype(BF16), a_tab)


def _s5_mixer(u, bsz, seq, lam_re, lam_im, log_dt, b_re, b_im, c_re, c_im, d_skip, w_glu):
    n, w = u.shape
    chunk = _s5_chunk(seq)
    steps = chunk // SUBLANE
    bin_re, bin_im, bout_re, bout_im, a_tab = _s5_discretise(lam_re, lam_im, log_dt, b_re, b_im, c_re, c_im, steps)
    u_p = _to_chunk_order(u.reshape(bsz, seq, w), chunk)
    y = _s5_core(u_p, bin_re, bin_im, a_tab, bout_re, bout_im, d_skip.reshape(1, w).astype(F32), chunk=chunk)
    y = y.reshape(n, w)
    tnw = _pick(w, (512, 256, 128))
    w_a = w_glu[:, :w].astype(BF16)
    w_g = w_glu[:, w:].astype(BF16)
    out = _matmul([y], [(0, w_a, ()), (0, w_g, ())], epilogue=_ep_glu, out_dtype=BF16, tn=tnw, name="s5_glu")
    return _from_chunk_order(out.reshape(bsz, seq, w), chunk).reshape(n, w)


def _rope_table_body(pos_ref, inv_ref, cos_ref, sin_ref):
    ang = pos_ref[...].astype(F32) * inv_ref[...]
    cos_ref[...] = jnp.cos(ang)
    sin_ref[...] = jnp.sin(ang)


def _rope_tables(positions):
    n = positions.size
    half = MLA_ROPE // 2
    inv = ROPE_THETA ** (-jnp.arange(half, dtype=F32) / half)
    inv = jnp.zeros((1, LANE), F32).at[0, :half].set(inv)
    tm = _pick(n, (512, 256, 128, 64, 32, 16, 8))
    return pl.pallas_call(
        _rope_table_body,
        grid=(n // tm,),
        in_specs=[pl.BlockSpec((tm, 1), lambda i: (i, 0)), pl.BlockSpec((1, LANE), lambda i: (0, 0))],
        out_specs=[pl.BlockSpec((tm, LANE), lambda i: (i, 0))] * 2,
        out_shape=[jax.ShapeDtypeStruct((n, LANE), F32)] * 2,
        compiler_params=_params(("parallel",)),
        name="rope_table",
    )(positions.reshape(n, 1).astype(I32), inv)


def _head_norm_rope(t0, t1, t2, g0, g1, g2, cos, sin, true_dim):
    ss = (jnp.sum(t0 * t0, axis=-1, keepdims=True) + jnp.sum(t1 * t1, axis=-1, keepdims=True)
          + jnp.sum(t2 * t2, axis=-1, keepdims=True))
    inv = lax.rsqrt(ss * (1.0 / true_dim) + EPS)
    n0 = t0 * inv * g0
    n1 = t1 * inv * g1
    n2 = t2 * inv * g2
    o1 = n1 * cos - n2 * sin
    o2 = n1 * sin + n2 * cos
    half = MLA_ROPE // 2
    lane = lax.broadcasted_iota(I32, o1.shape, 1)
    o1 = jnp.where(lane < half, o1, 0.0)
    o2 = jnp.where(lane < half, o2, 0.0)
    return jnp.concatenate([n0, o1 + pltpu.roll(o2, half, 1)], axis=-1)


def _mla_qkv_body(cq_ref, ckv_ref, pe_ref, gq_lat_ref, gkv_lat_ref, wq_ref, wkv_ref, qg_ref, kg_ref, cos_ref, sin_ref,
                  q_ref, k_ref, v_ref):
    def lat_norm(ref, g_ref):
        t = ref[...]
        ms = jnp.mean(t * t, axis=-1, keepdims=True)
        return (t * lax.rsqrt(ms + EPS) * g_ref[...]).astype(BF16)

    cqn = lat_norm(cq_ref, gq_lat_ref)
    ckvn = lat_norm(ckv_ref, gkv_lat_ref)
    cos, sin = cos_ref[...], sin_ref[...]
    hw = 3 * LANE
    qg = (qg_ref[:, 0:LANE], qg_ref[:, LANE:2 * LANE], qg_ref[:, 2 * LANE:hw])
    kg = (kg_ref[:, 0:LANE], kg_ref[:, LANE:2 * LANE], kg_ref[:, 2 * LANE:hw])
    pe1 = pe_ref[:, 0:LANE]
    pe2 = pe_ref[:, LANE:2 * LANE]
    kvw = MLA_NOPE + MLA_V
    qk = MLA_NOPE + MLA_ROPE
    for h in range(MLA_HEADS):
        qh = jnp.dot(cqn, wq_ref[:, h * hw:(h + 1) * hw], preferred_element_type=F32)
        qh = _head_norm_rope(qh[:, 0:LANE], qh[:, LANE:2 * LANE], qh[:, 2 * LANE:hw], *qg, cos, sin, qk)
        q_ref[0, h] = (qh * (qk ** -0.5 * LOG2E)).astype(q_ref.dtype)
        kvh = jnp.dot(ckvn, wkv_ref[:, h * kvw:(h + 1) * kvw], preferred_element_type=F32)
        k_ref[0, h] = _head_norm_rope(kvh[:, 0:MLA_NOPE], pe1, pe2, *kg, cos, sin, qk).astype(k_ref.dtype)
        v_ref[0, h] = kvh[:, MLA_NOPE:kvw].astype(v_ref.dtype)


def _pad_rope_gain(gain):
    half = MLA_ROPE // 2
    out = jnp.zeros((3 * LANE,), F32)
    out = out.at[:MLA_NOPE].set(gain[:MLA_NOPE].astype(F32))
    out = out.at[LANE:LANE + half].set(gain[MLA_NOPE:MLA_NOPE + half].astype(F32))
    out = out.at[2 * LANE:2 * LANE + half].set(gain[MLA_NOPE + half:].astype(F32))
    return out.reshape(1, 3 * LANE)


def _flash_body(q_ref, k_ref, v_ref, o_ref, *, tq, wide):
    qi = pl.program_id(2)
    dv = v_ref.shape[-1]
    q = q_ref[0, 0]

    def step(off, width, carry, masked):
        m, l, acc = carry
        k = k_ref[0, 0, pl.ds(off, width), :]
        v = v_ref[0, 0, pl.ds(off, width), :]
        s = lax.dot_general(q, k, (((1,), (1,)), ((), ())), preferred_element_type=F32)
        if masked:
            row = lax.broadcasted_iota(I32, (tq, width), 0)
            col = lax.broadcasted_iota(I32, (tq, width), 1)
            s = jnp.where(col <= row, s, NEG_BIG)
        m_new = jnp.maximum(m, jnp.max(s, axis=1, keepdims=True))
        alpha = jnp.exp2(m - m_new)
        p = jnp.exp2(s - m_new)
        l = alpha * l + jnp.sum(p, axis=1, keepdims=True)
        acc = alpha * acc + jnp.dot(p.astype(BF16), v, preferred_element_type=F32)
        return m_new, l, acc

    per = wide // tq
    n_wide = qi // per
    carry = (jnp.full((tq, 1), NEG_BIG, F32), jnp.zeros((tq, 1), F32), jnp.zeros((tq, dv), F32))
    carry = lax.fori_loop(0, n_wide, lambda j, c: step(pl.multiple_of(j * wide, wide), wide, c, False), carry)
    carry = lax.fori_loop(n_wide * per, qi, lambda j, c: step(pl.multiple_of(j * tq, tq), tq, c, False), carry)
    m, l, acc = step(pl.multiple_of(qi * tq, tq), tq, carry, True)
    o_ref[0] = (acc / l).astype(o_ref.dtype)


def _flash_attention(q, k, v):
    b, h, s, dk = q.shape
    dv = v.shape[-1]
    tq = _pick(s, (1024, 512, 256, 128))
    wide = _pick(s, (2 * tq, tq))
    return pl.pallas_call(
        functools.partial(_flash_body, tq=tq, wide=wide),
        grid=(b, h, s // tq),
        in_specs=[pl.BlockSpec((1, 1, tq, dk), lambda bi, hi, qi: (bi, hi, qi, 0)),
                  pl.BlockSpec((1, 1, s, dk), lambda bi, hi, qi: (bi, hi, 0, 0)),
                  pl.BlockSpec((1, 1, s, dv), lambda bi, hi, qi: (bi, hi, 0, 0))],
        out_specs=pl.BlockSpec((1, tq, dv), lambda bi, hi, qi: (bi, qi, hi)),
        out_shape=jax.ShapeDtypeStruct((b, s, h * dv), BF16),
        compiler_params=_params(("parallel", "parallel", "arbitrary")),
        name="mla_flash",
    )(q, k, v)


def _mla_mixer(hm, col, cos, sin, bsz, seq, q_lora, kv_lora, q_lat_gain, kv_lat_gain, w_uq, w_ukv, q_gain, k_gain):
    n = hm.shape[0]
    half = MLA_ROPE // 2
    qk = MLA_NOPE + MLA_ROPE
    wq = w_uq.reshape(q_lora, MLA_HEADS, qk)
    wq_p = jnp.zeros((q_lora, MLA_HEADS, 3 * LANE), BF16)
    wq_p = wq_p.at[:, :, :MLA_NOPE].set(wq[:, :, :MLA_NOPE].astype(BF16))
    wq_p = wq_p.at[:, :, LANE:LANE + half].set(wq[:, :, MLA_NOPE:MLA_NOPE + half].astype(BF16))
    wq_p = wq_p.at[:, :, 2 * LANE:2 * LANE + half].set(wq[:, :, MLA_NOPE + half:].astype(BF16))
    wq_p = wq_p.reshape(q_lora, MLA_HEADS * 3 * LANE)
    wkv = w_ukv.astype(BF16)
    tm = _pick(seq, (256, 128, 64, 32, 16))
    nblk = seq // tm
    cols = lambda width, block: pl.BlockSpec((tm, width), lambda bi, si: (bi * nblk + si, block))
    const = lambda arr: pl.BlockSpec(arr.shape, lambda bi, si: (0, 0), pipeline_mode=pl.Buffered(1))
    heads = lambda width: pl.BlockSpec((1, MLA_HEADS, tm, width), lambda bi, si: (bi, 0, si, 0))
    gq_lat = q_lat_gain.reshape(1, q_lora).astype(F32)
    gkv_lat = kv_lat_gain.reshape(1, kv_lora).astype(F32)
    qg = _pad_rope_gain(q_gain)
    kg = _pad_rope_gain(k_gain)
    q, k, v = pl.pallas_call(
        _mla_qkv_body,
        grid=(bsz, nblk),
        in_specs=[cols(q_lora, col["c_q"] // q_lora), cols(kv_lora, col["c_kv"] // kv_lora),
                  cols(2 * LANE, col["pe"] // (2 * LANE)),
                  const(gq_lat), const(gkv_lat), const(wq_p), const(wkv), const(qg), const(kg),
                  cols(LANE, 0), cols(LANE, 0)],
        out_specs=[heads(2 * LANE), heads(2 * LANE), heads(MLA_V)],
        out_shape=[jax.ShapeDtypeStruct((bsz, MLA_HEADS, seq, 2 * LANE), BF16),
                   jax.ShapeDtypeStruct((bsz, MLA_HEADS, seq, 2 * LANE), BF16),
                   jax.ShapeDtypeStruct((bsz, MLA_HEADS, seq, MLA_V), BF16)],
        compiler_params=_params(("parallel", "parallel")),
        name="mla_qkv",
    )(hm, hm, hm, gq_lat, gkv_lat, wq_p, wkv, qg, kg, cos, sin)
    return _flash_attention(q, k, v).reshape(n, MLA_HEADS * MLA_V)


def _dsa_prep_body(hd_q_ref, hd_k_ref, hd_v_ref, hi_q_ref, hi_k_ref, hi_w_ref, qg_ref, kg_ref,
                   q_ref, k_ref, v_ref, qi_ref, klo_ref, khi_ref, w_ref):
    def norm(t, g):
        ms = jnp.mean(t * t, axis=-1, keepdims=True)
        return t * lax.rsqrt(ms + EPS) * g

    qg = qg_ref[...]
    for h in range(DSA_HEADS):
        qh = norm(hd_q_ref[:, h * DSA_HEAD_DIM:(h + 1) * DSA_HEAD_DIM], qg)
        q_ref[0, h] = (qh * (DSA_HEAD_DIM ** -0.5 * LOG2E)).astype(q_ref.dtype)
    k_ref[0] = norm(hd_k_ref[...], kg_ref[...]).astype(k_ref.dtype)
    v_ref[0] = hd_v_ref[...].astype(v_ref.dtype)
    qi_ref[0] = hi_q_ref[...].astype(qi_ref.dtype)
    ki = hi_k_ref[...]
    lane = lax.broadcasted_iota(I32, ki.shape, 1)
    ki = jnp.where(lane < IDX_DIM, ki, 0.0)
    klo_ref[0] = ki.astype(klo_ref.dtype)
    khi_ref[0] = pltpu.roll(ki, IDX_DIM, 1).astype(khi_ref.dtype)
    w_ref[0] = hi_w_ref[...] * ((IDX_HEADS ** -0.5) * (IDX_DIM ** -0.5))


def _float_key(x):
    bits = pltpu.bitcast(x, I32)
    return jnp.where(bits < 0, bits ^ jnp.int32(0x7FFFFFFF), bits)


def _dsa_body(q_ref, k_ref, v_ref, qi_ref, klo_ref, khi_ref, w_ref, o_ref, keys_ref, gmax_ref, acc_ref,
              *, tq, tk, wide, n_sel):
    qb = pl.program_id(1)
    n_kv = (qb * tq + tq + tk - 1) // tk
    row = lax.broadcasted_iota(I32, (tq, tk), 0) + qb * tq
    col0 = lax.broadcasted_iota(I32, (tq, tk), 1)

    w = w_ref[0]

    def score_tile(j, _):
        off = pl.multiple_of(j * tk, tk)
        klo = klo_ref[0, pl.ds(off, tk), :]
        khi = khi_ref[0, pl.ds(off, tk), :]
        acc = jnp.zeros((tq, tk), F32)
        for hp in range(IDX_HEADS // 2):
            q2 = qi_ref[0, :, hp * LANE:(hp + 1) * LANE]
            for half, kk in ((0, klo), (1, khi)):
                h = 2 * hp + half
                logit = lax.dot_general(q2, kk, (((1,), (1,)), ((), ())), preferred_element_type=F32)
                acc = acc + w[:, h:h + 1] * jnp.maximum(logit, 0.0)
        causal = col0 + off <= row
        keys_ref[:, pl.ds(off, tk)] = jnp.where(causal, _float_key(acc + 0.0), INT_MIN)
        sc = jnp.where(causal, acc, -jnp.inf)
        for c in range(tk // LANE):
            gmax_ref[c % 2] = jnp.maximum(gmax_ref[c % 2], sc[:, c * LANE:(c + 1) * LANE])
        return 0

    gmax_ref[...] = jnp.full(gmax_ref.shape, -jnp.inf, F32)
    lax.fori_loop(0, n_kv, score_tile, 0)

    rc = min(tq, 128)

    def row_group(g, carry):
        r0 = pl.multiple_of(g * rc, rc)
        ge = gmax_ref[0, pl.ds(r0, rc), :]
        go = gmax_ref[1, pl.ds(r0, rc), :]
        key_l = _float_key(jnp.min(jnp.minimum(ge, go), axis=1, keepdims=True) + 0.0)
        key_u = _float_key(jnp.max(jnp.maximum(ge, go), axis=1, keepdims=True) + 0.0)
        span = key_u - key_l
        n_iter = jnp.max(jnp.where(span < 0, 32, 32 - lax.clz(span)))
        lo0 = jnp.broadcast_to(key_l, (rc, LANE))
        hi0 = jnp.broadcast_to(key_u + 1, (rc, LANE))

        def halve(b, bounds):
            lo, hi = bounds
            cand = (lo >> 1) + (hi >> 1) + (lo & hi & 1)

            def count_tile(j, part):
                off = pl.multiple_of(j * tk, tk)
                for c in range(tk // LANE):
                    kc = keys_ref[pl.ds(r0, rc), pl.ds(off + c * LANE, LANE)]
                    part = part + jnp.where(kc >= cand, 1.0, 0.0)
                return part

            part = lax.fori_loop(0, n_kv, count_tile, jnp.zeros((rc, LANE), F32))
            enough = jnp.sum(part, axis=1, keepdims=True) >= n_sel
            return jnp.where(enough, cand, lo), jnp.where(enough, hi, cand)

        thr, _ = lax.fori_loop(0, n_iter, halve, (lo0, hi0))
        thr = jnp.maximum(thr, INT_MIN + 1)

        def bias_tile(j, c2):
            off = pl.multiple_of(j * tk, tk)
            for c in range(tk // LANE):
                kc = keys_ref[pl.ds(r0, rc), pl.ds(off + c * LANE, LANE)]
                bias = jnp.where(kc >= thr, 0.0, NEG_BIG)
                keys_ref[pl.ds(r0, rc), pl.ds(off + c * LANE, LANE)] = pltpu.bitcast(bias, I32)
            return c2

        lax.fori_loop(0, n_kv, bias_tile, 0)
        return carry

    lax.fori_loop(0, tq // rc, row_group, 0)

    per = wide // tk
    n_wide = n_kv // per

    def head(h, carry):
        q = q_ref[0, h]

        def step(off, width, st):
            m, l, acc = st
            k = k_ref[0, pl.ds(off, width), :]
            v = v_ref[0, pl.ds(off, width), :]
            bias = pltpu.bitcast(keys_ref[:, pl.ds(off, width)], F32)
            s = lax.dot_general(q, k, (((1,), (1,)), ((), ())), preferred_element_type=F32) + bias
            m_new = jnp.maximum(m, jnp.max(s, axis=1, keepdims=True))
            alpha = jnp.exp2(m - m_new)
            p = jnp.exp2(s - m_new)
            l = alpha * l + jnp.sum(p, axis=1, keepdims=True)
            acc = alpha * acc + jnp.dot(p.astype(BF16), v, preferred_element_type=F32)
            return m_new, l, acc

        st = (jnp.full((tq, 1), NEG_BIG, F32), jnp.zeros((tq, 1), F32), jnp.zeros((tq, DSA_HEAD_DIM), F32))
        st = lax.fori_loop(0, n_wide, lambda j, c: step(pl.multiple_of(j * wide, wide), wide, c), st)
        st = lax.fori_loop(n_wide * per, n_kv, lambda j, c: step(pl.multiple_of(j * tk, tk), tk, c), st)
        m, l, acc = st
        acc_ref[h] = acc / l
        return carry

    lax.fori_loop(0, DSA_HEADS, head, 0)
    for h in range(DSA_HEADS):
        o_ref[0, :, h * DSA_HEAD_DIM:(h + 1) * DSA_HEAD_DIM] = acc_ref[h].astype(o_ref.dtype)


def _dsa_mixer(hd, col, bsz, seq, q_gain, k_gain):
    n = hd.shape[0]
    qw = DSA_HEADS * DSA_HEAD_DIM
    iw = IDX_HEADS * IDX_DIM
    tm = _pick(seq, (256, 128, 64, 32, 16))
    nblk = seq // tm

    def cols(width, block):
        return pl.BlockSpec((tm, width), lambda bi, si: (bi * nblk + si, block))

    per_tok = lambda width: pl.BlockSpec((1, tm, width), lambda bi, si: (bi, si, 0))
    q, k, v, qi, klo, khi, w = pl.pallas_call(
        _dsa_prep_body,
        grid=(bsz, nblk),
        in_specs=[cols(qw, col["q_c"] // qw), cols(LANE, col["k_c"] // LANE), cols(LANE, col["v_c"] // LANE),
                  cols(iw, col["q_i"] // iw), cols(LANE, col["k_i"] // LANE), cols(LANE, col["w_i"] // LANE),
                  pl.BlockSpec((1, DSA_HEAD_DIM), lambda bi, si: (0, 0)),
                  pl.BlockSpec((1, DSA_HEAD_DIM), lambda bi, si: (0, 0))],
        out_specs=[pl.BlockSpec((1, DSA_HEADS, tm, DSA_HEAD_DIM), lambda bi, si: (bi, 0, si, 0)),
                   per_tok(DSA_HEAD_DIM), per_tok(DSA_HEAD_DIM), per_tok(iw), per_tok(LANE), per_tok(LANE),
                   per_tok(LANE)],
        out_shape=[jax.ShapeDtypeStruct((bsz, DSA_HEADS, seq, DSA_HEAD_DIM), BF16),
                   jax.ShapeDtypeStruct((bsz, seq, DSA_HEAD_DIM), BF16),
                   jax.ShapeDtypeStruct((bsz, seq, DSA_HEAD_DIM), BF16),
                   jax.ShapeDtypeStruct((bsz, seq, iw), BF16),
                   jax.ShapeDtypeStruct((bsz, seq, LANE), BF16),
                   jax.ShapeDtypeStruct((bsz, seq, LANE), BF16),
                   jax.ShapeDtypeStruct((bsz, seq, LANE), F32)],
        compiler_params=_params(("parallel", "parallel")),
        name="dsa_prep",
    )(hd, hd, hd, hd, hd, hd, q_gain.reshape(1, -1).astype(F32), k_gain.reshape(1, -1).astype(F32))

    tq = _pick(seq, (512, 256, 128))
    tk = _pick(seq, (512, 256, 128))
    n_sel = min(DSA_TOPK_MAX, seq // 4)
    assert n_sel <= 2 * LANE
    whole = lambda width: pl.BlockSpec((1, seq, width), lambda bi, qb: (bi, 0, 0), pipeline_mode=pl.Buffered(1))
    out = pl.pallas_call(
        functools.partial(_dsa_body, tq=tq, tk=tk, wide=_pick(seq, (4 * tk, 2 * tk, tk)), n_sel=n_sel),
        grid=(bsz, seq // tq),
        in_specs=[pl.BlockSpec((1, DSA_HEADS, tq, DSA_HEAD_DIM), lambda bi, qb: (bi, 0, qb, 0)),
                  whole(DSA_HEAD_DIM), whole(DSA_HEAD_DIM),
                  pl.BlockSpec((1, tq, iw), lambda bi, qb: (bi, qb, 0)),
                  whole(LANE), whole(LANE),
                  pl.BlockSpec((1, tq, LANE), lambda bi, qb: (bi, qb, 0))],
        out_specs=pl.BlockSpec((1, tq, qw), lambda bi, qb: (bi, qb, 0)),
        out_shape=jax.ShapeDtypeStruct((bsz, seq, qw), BF16),
        scratch_shapes=[pltpu.VMEM((tq, seq), I32), pltpu.VMEM((2, tq, LANE), F32),
                        pltpu.VMEM((DSA_HEADS, tq, DSA_HEAD_DIM), F32)],
        compiler_params=_params(("parallel", "arbitrary")),
        name="dsa_attention",
    )(q, k, v, qi, klo, khi, w)
    return out.reshape(n, qw)


def _pad_cols(w, width):
    return jnp.pad(w, ((0, 0), (0, width - w.shape[1])))


def _hybrid_mixer(x2, cos, sin, bsz, seq, norm_gain, w_in, q_lat_gain, kv_lat_gain, w_uq, w_ukv, mla_q_gain,
                  mla_k_gain, dsa_q_gain, dsa_k_gain, lam_re, lam_im, log_dt, b_re, b_im, c_re, c_im, d_skip, w_glu,
                  w_branch, w_out, layer):
    n, d = x2.shape
    ssm_w = d_skip.shape[-1]
    q_lora = q_lat_gain.shape[-1]
    kv_lora = kv_lat_gain.shape[-1]
    half = MLA_ROPE // 2
    dsa_w = DSA_HEADS * DSA_HEAD_DIM
    idx_w = IDX_HEADS * IDX_DIM
    sizes = (ssm_w, q_lora, kv_lora, MLA_ROPE, dsa_w, DSA_HEAD_DIM, DSA_HEAD_DIM, idx_w, IDX_DIM, IDX_HEADS, 3 * d)
    offs = [0]
    for s in sizes:
        offs.append(offs[-1] + s)
    seg = lambda i: w_in[:, offs[i]:offs[i + 1]]
    pe = seg(3)
    tn = _pick(d, (512, 256, 128))
    pe_pad = jnp.concatenate([_pad_cols(pe[:, :half], LANE), _pad_cols(pe[:, half:], LANE)], axis=1)
    parts = [("q_c", seg(4), dsa_w), ("c_kv", seg(2), kv_lora), ("u", seg(0), LANE), ("c_q", seg(1), q_lora),
             ("q_i", seg(7), idx_w), ("pe", pe_pad, 2 * LANE), ("k_c", seg(5), LANE), ("v_c", seg(6), LANE),
             ("k_i", _pad_cols(seg(8), LANE), LANE), ("w_i", _pad_cols(seg(9), LANE), LANE)]
    col = {}
    at = 0
    for name, part, block_width in parts:
        assert at % block_width == 0, (name, at, block_width)
        col[name] = at
        at += part.shape[1]
    gap = -at % tn
    col["gates"] = at + gap
    w_cat = jnp.concatenate([part for _, part, _ in parts] + [jnp.zeros((d, gap), F32), seg(10)],
                            axis=1).astype(BF16)
    h = _matmul([x2], [(0, w_cat, ())], epilogue=_ep_plain, out_dtype=F32, norm_gain=norm_gain,
                tn=_pick(w_cat.shape[1], (768, 512, 256, 128)), name="in_proj")

    u = h[:, col["u"]:col["u"] + ssm_w]
    y_ssm = _s5_mixer(u, bsz, seq, lam_re, lam_im, log_dt, b_re, b_im, c_re, c_im, d_skip, w_glu)
    y_mla = _mla_mixer(h, col, cos, sin, bsz, seq, q_lora, kv_lora, q_lat_gain, kv_lat_gain, w_uq, w_ukv,
                       mla_q_gain, mla_k_gain)
    y_dsa = _dsa_mixer(h, col, bsz, seq, dsa_q_gain, dsa_k_gain)

    br0 = ssm_w
    br1 = ssm_w + MLA_HEADS * MLA_V
    g0 = col["gates"] // tn
    merged = _matmul([y_ssm, y_mla, y_dsa],
                     [(0, w_branch[:br0].astype(BF16), ()), (1, w_branch[br0:br1].astype(BF16), ()),
                      (2, w_branch[br1:].astype(BF16), ())],
                     epilogue=_ep_branches, out_dtype=BF16, tn=tn,
                     extras=[(h, "mn", g0), (h, "mn", g0 + d // tn), (h, "mn", g0 + 2 * (d // tn))],
                     name="branch_merge")
    return _matmul([merged], [(0, w_out, (layer,))], epilogue=_ep_residual, out_dtype=F32, tn=tn,
                   extras=[(x2, "mn", 0)], name="mixer_out")


def _dense_ffn(x2, gain, w_gate, w_up, w_down, idx):
    d = x2.shape[1]
    dff = w_gate.shape[-1]
    xn = _rmsnorm(x2, gain)
    tn = _pick(dff, (256, 128))
    h = _matmul([xn], [(0, w_gate, (idx,)), (0, w_up, (idx,))], epilogue=_ep_swiglu, out_dtype=BF16, tn=tn,
                name="ffn_up")
    tnd = _pick(d, (256, 128))
    n_slabs = 2 if dff % (2 * LANE) == 0 and dff > 8192 else 1
    out = x2
    for s in range(n_slabs):
        out = _matmul([h], [(0, w_down, (idx,))], epilogue=_ep_residual, out_dtype=F32, tn=tnd,
                      k_slice=(dff // n_slabs, s), extras=[(out, "mn", 0)], name="ffn_down")
    return out


def _router_body(x_ref, g_ref, w_ref, xn_ref, meta_ref, cnt_ref, run_ref, *, n_experts):
    i = pl.program_id(0)

    @pl.when(i == 0)
    def _():
        run_ref[...] = jnp.zeros_like(run_ref)

    x = x_ref[...]
    tm = x.shape[0]
    ms = jnp.mean(x * x, axis=-1, keepdims=True)
    xn = x * lax.rsqrt(ms + EPS) * g_ref[...]
    xn_ref[...] = xn.astype(xn_ref.dtype)
    logits = jnp.dot(xn, w_ref[...], preferred_element_type=F32, precision=lax.Precision.HIGHEST)
    lane = lax.broadcasted_iota(I32, logits.shape, 1).astype(F32)
    lg = jnp.where(lane < n_experts, logits, -jnp.inf)
    m1 = jnp.max(lg, axis=1, keepdims=True)
    i1 = jnp.min(jnp.where(lg == m1, lane, float(LANE)), axis=1, keepdims=True)
    lg2 = jnp.where(lane == i1, -jnp.inf, lg)
    m2 = jnp.max(lg2, axis=1, keepdims=True)
    i2 = jnp.min(jnp.where(lg2 == m2, lane, float(LANE)), axis=1, keepdims=True)
    e2 = jnp.exp(m2 - m1)
    den = 1.0 + e2
    oh1 = jnp.where(lane == i1, 1.0, 0.0)
    oh2 = jnp.where(lane == i2, 1.0, 0.0)
    oh = oh1 + oh2
    r = lax.broadcasted_iota(I32, (tm, tm), 0)
    c = lax.broadcasted_iota(I32, (tm, tm), 1)
    tri = jnp.where(c < r, 1.0, 0.0).astype(BF16)
    base = run_ref[0:1, :]
    rank = jnp.dot(tri, oh.astype(BF16), preferred_element_type=F32) + base
    r1 = jnp.sum(oh1 * rank, axis=1, keepdims=True)
    r2 = jnp.sum(oh2 * rank, axis=1, keepdims=True)
    total = base + jnp.sum(oh, axis=0, keepdims=True)
    run_ref[0:1, :] = total
    cnt_ref[...] = jnp.broadcast_to(total, cnt_ref.shape)
    meta = jnp.zeros_like(logits)
    for k, val in enumerate((i1, i2, r1, r2, 1.0 / den, e2 / den)):
        meta = jnp.where(lane == float(k), val, meta)
    meta_ref[...] = meta


def _cast_body(x_ref, o_ref):
    o_ref[...] = x_ref[...].astype(o_ref.dtype)


def _expert_weights_bf16(w, idx):
    _, e, r, c = w.shape
    tr = _pick(r, (1024, 512, 256, 128, 64, 32, 16))
    return pl.pallas_call(
        _cast_body,
        grid=(e, r // tr),
        in_specs=[pl.BlockSpec((None, 1, tr, c), lambda i, j: (idx, i, j, 0))],
        out_specs=pl.BlockSpec((1, tr, c), lambda i, j: (i, j, 0)),
        out_shape=jax.ShapeDtypeStruct((e, r, c), BF16),
        compiler_params=_params(("parallel", "parallel")),
        name="moe_weight_cast",
    )(w)


def _gather_rows_body(src_ref, x_hbm, o_ref, sem, *, rows):
    base = pl.program_id(0) * rows

    def issue(r, carry):
        pltpu.make_async_copy(x_hbm.at[src_ref[base + r]], o_ref.at[r], sem).start()
        return carry

    lax.fori_loop(0, rows, issue, 0)
    pltpu.make_async_copy(x_hbm.at[pl.ds(0, rows)], o_ref, sem).wait()


def _gather_rows(x, src, *, rows):
    n_out = src.shape[0]
    n, d = x.shape
    x3 = x.reshape(n, d // LANE, LANE)
    out = pl.pallas_call(
        functools.partial(_gather_rows_body, rows=rows),
        grid_spec=pltpu.PrefetchScalarGridSpec(
            num_scalar_prefetch=1,
            grid=(n_out // rows,),
            in_specs=[pl.BlockSpec(memory_space=pl.ANY)],
            out_specs=pl.BlockSpec((rows, d // LANE, LANE), lambda i, src_ref: (i, 0, 0)),
            scratch_shapes=[pltpu.SemaphoreType.DMA(())],
        ),
        out_shape=jax.ShapeDtypeStruct((n_out, d // LANE, LANE), x.dtype),
        compiler_params=_params(("arbitrary",)),
        name="moe_gather",
    )(src, x3)
    return out.reshape(n_out, d)


def _experts_body(tile_ref, wexp_ref, lo_ref, hi_ref, first_ref, xs_ref, w1_ref, w3_ref, w2_ref, o_ref, *, tm):
    v = pl.program_id(0)
    f = pl.program_id(1)

    @pl.when((f == 0) & (first_ref[v] == 1))
    def _():
        o_ref[...] = jnp.zeros_like(o_ref)

    lo = lo_ref[v]
    hi = hi_ref[v]

    @pl.when(hi > lo)
    def _():
        xs = xs_ref[...]
        g = jnp.dot(xs, w1_ref[...], preferred_element_type=F32)
        u = jnp.dot(xs, w3_ref[...], preferred_element_type=F32)
        row = tile_ref[v] * tm + lax.broadcasted_iota(I32, (tm, 1), 0)
        h = jnp.where((row >= lo) & (row < hi), g * _sigmoid(g) * u, 0.0)
        o_ref[...] += jnp.dot(h.astype(BF16), w2_ref[...], preferred_element_type=F32)


def _combine_body(d1_ref, d2_ref, y_hbm, x_ref, w1_ref, w2_ref, o_ref, b1_ref, b2_ref, sem, *, rows):
    base = pl.program_id(0) * rows

    def issue(r, carry):
        pltpu.make_async_copy(y_hbm.at[d1_ref[base + r]], b1_ref.at[r], sem.at[0]).start()
        pltpu.make_async_copy(y_hbm.at[d2_ref[base + r]], b2_ref.at[r], sem.at[1]).start()
        return carry

    lax.fori_loop(0, rows, issue, 0)
    pltpu.make_async_copy(y_hbm.at[pl.ds(0, rows)], b1_ref, sem.at[0]).wait()
    pltpu.make_async_copy(y_hbm.at[pl.ds(0, rows)], b2_ref, sem.at[1]).wait()
    o_ref[...] = x_ref[...] + w1_ref[...] * b1_ref[...] + w2_ref[...] * b2_ref[...]


def _moe_ffn(x2, gain, w_router, w1, w3, w2, idx):
    n, d = x2.shape
    n_experts = w_router.shape[-1]
    dff = w1.shape[-1]
    tr = _pick(n, (256, 128, 64, 32, 16, 8))
    wr = _pad_cols(w_router[idx].astype(F32), LANE)
    xn, meta, cnt = pl.pallas_call(
        functools.partial(_router_body, n_experts=n_experts),
        grid=(n // tr,),
        in_specs=[pl.BlockSpec((tr, d), lambda i: (i, 0)), pl.BlockSpec((1, d), lambda i: (0, 0)),
                  pl.BlockSpec((d, LANE), lambda i: (0, 0))],
        out_specs=[pl.BlockSpec((tr, d), lambda i: (i, 0)), pl.BlockSpec((tr, LANE), lambda i: (i, 0)),
                   pl.BlockSpec((SUBLANE, LANE), lambda i: (0, 0))],
        out_shape=[jax.ShapeDtypeStruct((n, d), BF16), jax.ShapeDtypeStruct((n, LANE), F32),
                   jax.ShapeDtypeStruct((SUBLANE, LANE), F32)],
        scratch_shapes=[pltpu.VMEM((SUBLANE, LANE), F32)],
        compiler_params=_params(("arbitrary",)),
        name="moe_router",
    )(x2, gain.reshape(1, d).astype(F32), wr)

    e1, e2 = meta[:, 0].astype(I32), meta[:, 1].astype(I32)
    counts = cnt[0, :n_experts].astype(I32)
    ends = jnp.cumsum(counts)
    off = ends - counts
    dest1 = off[e1] + meta[:, 2].astype(I32)
    dest2 = off[e2] + meta[:, 3].astype(I32)
    na = TOP_K_EXPERTS * n
    tok = jnp.arange(n, dtype=I32)
    src = jnp.zeros((na,), I32).at[dest1].set(tok).at[dest2].set(tok)

    xs = _gather_rows(xn, src, rows=_pick(na, (256, 128, 64, 32, 16, 8)))

    tm = _pick(na, (512, 256, 128, 64, 32, 16, 8))
    n_tiles = na // tm
    nv = n_tiles + n_experts - 1
    lo = jnp.sort(jnp.concatenate([jnp.arange(n_tiles, dtype=I32) * tm, ends[:-1]]))
    hi = jnp.concatenate([lo[1:], jnp.full((1,), na, I32)])
    tile = jnp.minimum(lo // tm, n_tiles - 1)
    expert = jnp.minimum(jnp.sum((lo[:, None] >= ends[None, :]).astype(I32), axis=1), n_experts - 1)
    vid = jnp.arange(nv, dtype=I32)
    last = lax.cummax(jnp.where(hi > lo, vid, -1))
    wexp = expert[jnp.maximum(last, 0)]
    first = jnp.concatenate([jnp.ones((1,), I32), (tile[1:] != tile[:-1]).astype(I32)])

    fc = _pick(dff, (512, 256, 128))
    nf = dff // fc

    def up_map(v, f, tile_ref, wexp_ref, lo_ref, hi_ref, first_ref):
        return wexp_ref[v], 0, jnp.where(hi_ref[v] > lo_ref[v], f, nf - 1)

    def down_map(v, f, tile_ref, wexp_ref, lo_ref, hi_ref, first_ref):
        return wexp_ref[v], jnp.where(hi_ref[v] > lo_ref[v], f, nf - 1), 0

    def row_map(v, f, tile_ref, wexp_ref, lo_ref, hi_ref, first_ref):
        return tile_ref[v], 0

    ys = pl.pallas_call(
        functools.partial(_experts_body, tm=tm),
        grid_spec=pltpu.PrefetchScalarGridSpec(
            num_scalar_prefetch=5,
            grid=(nv, nf),
            in_specs=[pl.BlockSpec((tm, d), row_map), pl.BlockSpec((None, d, fc), up_map),
                      pl.BlockSpec((None, d, fc), up_map), pl.BlockSpec((None, fc, d), down_map)],
            out_specs=pl.BlockSpec((tm, d), row_map, pipeline_mode=pl.Buffered(1)),
        ),
        out_shape=jax.ShapeDtypeStruct((na, d), F32),
        compiler_params=_params(("arbitrary", "arbitrary")),
        name="moe_experts",
    )(tile, wexp, lo, hi, first, xs, _expert_weights_bf16(w1, idx), _expert_weights_bf16(w3, idx),
      _expert_weights_bf16(w2, idx))

    rows = _pick(n, (256, 128, 64, 32, 16, 8))
    slab = (rows, d // LANE, LANE)
    slab_spec = pl.BlockSpec(slab, lambda i, d1, d2: (i, 0, 0))
    weight_spec = pl.BlockSpec((rows, 1, LANE), lambda i, d1, d2: (i, 0, 0))
    wt1 = jnp.broadcast_to(meta[:, 4][:, None, None], (n, 1, LANE))
    wt2 = jnp.broadcast_to(meta[:, 5][:, None, None], (n, 1, LANE))
    out = pl.pallas_call(
        functools.partial(_combine_body, rows=rows),
        grid_spec=pltpu.PrefetchScalarGridSpec(
            num_scalar_prefetch=2,
            grid=(n // rows,),
            in_specs=[pl.BlockSpec(memory_space=pl.ANY), slab_spec, weight_spec, weight_spec],
            out_specs=slab_spec,
            scratch_shapes=[pltpu.VMEM(slab, F32), pltpu.VMEM(slab, F32), pltpu.SemaphoreType.DMA((2,))],
        ),
        out_shape=jax.ShapeDtypeStruct((n, d // LANE, LANE), F32),
        compiler_params=_params(("arbitrary",)),
        name="moe_combine",
    )(dest1, dest2, ys.reshape(na, d // LANE, LANE), x2.reshape(n, d // LANE, LANE), wt1, wt2)
    return out.reshape(n, d)


def kernel(x, positions, mix_norm, w_in, mla_q_lat_norm, mla_kv_lat_norm, mla_w_uq, mla_w_ukv, mla_q_norm, mla_k_norm, dsa_q_norm, dsa_k_norm, ssm_lam_re, ssm_lam_im, ssm_log_dt, ssm_b_re, ssm_b_im, ssm_c_re, ssm_c_im, ssm_d, ssm_w_glu, w_branch, w_out, ffn_norm, dense_w_gate, dense_w_up, dense_w_down, moe_w_router, moe_w1, moe_w3, moe_w2):
    bsz, seq, d = x.shape
    depth = mix_norm.shape[0]
    x2 = x.reshape(bsz * seq, d)
    cos, sin = _rope_tables(positions)
    for layer in range(depth):
        x2 = _hybrid_mixer(x2, cos, sin, bsz, seq, mix_norm[layer], w_in[layer], mla_q_lat_norm[layer],
                           mla_kv_lat_norm[layer], mla_w_uq[layer], mla_w_ukv[layer], mla_q_norm[layer],
                           mla_k_norm[layer], dsa_q_norm[layer], dsa_k_norm[layer], ssm_lam_re[layer],
                           ssm_lam_im[layer], ssm_log_dt[layer], ssm_b_re[layer], ssm_b_im[layer], ssm_c_re[layer],
                           ssm_c_im[layer], ssm_d[layer], ssm_w_glu[layer], w_branch[layer], w_out, layer)
        i = layer // 2
        if layer % 2 == 0:
            x2 = _dense_ffn(x2, ffn_norm[layer], dense_w_gate, dense_w_up, dense_w_down, i)
        else:
            x2 = _moe_ffn(x2, ffn_norm[layer], moe_w_router, moe_w1, moe_w3, moe_w2, i)
    return x2.reshape(bsz, seq, d)
```

```python
import functools
import math

import jax
import jax.numpy as jnp
from jax import lax
from jax.experimental import pallas as pl
from jax.experimental.pallas import tpu as pltpu

F32 = jnp.float32
BF16 = jnp.bfloat16
I32 = jnp.int32

SSM_GROUP = 16
SSM_STATE = 64
MLA_HEADS = 12
MLA_NOPE = 128
MLA_ROPE = 64
MLA_V = 128
DSA_HEADS = 12
DSA_HEAD_DIM = 128
IDX_HEADS = 16
IDX_DIM = 64
DSA_TOPK_MAX = 256
TOP_K_EXPERTS = 2
ROPE_THETA = 10000.0
EPS = 1e-6

LANE = 128
SUBLANE = 8
VMEM_LIMIT_BYTES = 56 * 1024 * 1024
NEG_BIG = -1e30
INT_MIN = -(2 ** 31)
LOG2E = 1.4426950408889634


def _params(semantics):
    return pltpu.CompilerParams(dimension_semantics=semantics, vmem_limit_bytes=VMEM_LIMIT_BYTES)


def _sigmoid(x):
    return 1.0 / (1.0 + jnp.exp(-x))


def _pick(n, candidates):
    for c in candidates:
        if n % c == 0:
            return c
    return n


def _rmsnorm_body(x_ref, g_ref, o_ref):
    x = x_ref[...].astype(F32)
    ms = jnp.mean(x * x, axis=-1, keepdims=True)
    o_ref[...] = (x * lax.rsqrt(ms + EPS) * g_ref[...]).astype(o_ref.dtype)


def _rmsnorm(x, gain, *, width=None, col_block=0, out_dtype=BF16):
    m = x.shape[0]
    width = width or x.shape[1]
    tm = _pick(m, (512, 256, 128, 64, 32, 16, 8))
    return pl.pallas_call(
        _rmsnorm_body,
        grid=(m // tm,),
        in_specs=[pl.BlockSpec((tm, width), lambda i: (i, col_block)),
                  pl.BlockSpec((1, width), lambda i: (0, 0))],
        out_specs=pl.BlockSpec((tm, width), lambda i: (i, 0)),
        out_shape=jax.ShapeDtypeStruct((m, width), out_dtype),
        compiler_params=_params(("parallel",)),
        name="rmsnorm",
    )(x, gain.reshape(1, width).astype(F32))


def _mm_body(*refs, n_a, pairs, n_extra, epilogue, normed):
    a_refs = refs[:n_a]
    w_refs = refs[n_a:n_a + len(pairs)]
    e_refs = refs[n_a + len(pairs):n_a + len(pairs) + n_extra]
    if normed:
        g_ref, o_ref, xn_ref = refs[n_a + len(pairs) + n_extra:]
    else:
        o_ref = refs[n_a + len(pairs) + n_extra]
    if normed:

        @pl.when(pl.program_id(1) == 0)
        def _():
            rows = a_refs[0].shape[0]
            step = math.gcd(rows, 128)

            def chunk(c, carry):
                r0 = pl.multiple_of(c * step, step)
                x = a_refs[0][pl.ds(r0, step), :].astype(F32)
                ms = jnp.mean(x * x, axis=-1, keepdims=True)
                xn_ref[pl.ds(r0, step), :] = (x * lax.rsqrt(ms + EPS) * g_ref[...]).astype(BF16)
                return carry

            lax.fori_loop(0, rows // step, chunk, 0)

        a_vals = [xn_ref[...]]
    else:
        a_vals = [a[...].astype(BF16) for a in a_refs]
    parts = [jnp.dot(a_vals[ai], w[...].astype(BF16), preferred_element_type=F32)
             for ai, w in zip(pairs, w_refs)]
    o_ref[...] = epilogue(parts, *[e[...] for e in e_refs]).astype(o_ref.dtype)


def _matmul(a_list, w_list, *, epilogue, out_dtype, tn, tm=None, k_slice=None, extras=(), norm_gain=None,
            name="matmul"):
    m = a_list[0].shape[0]
    n = w_list[0][1].shape[-1]
    tm = tm or _pick(m, (1024, 512, 256, 128, 64, 32, 16, 8))
    ksize, kidx = k_slice if k_slice else (None, 0)
    in_specs = []
    a_mode = {"pipeline_mode": pl.Buffered(1)} if norm_gain is not None else {}
    for a in a_list:
        in_specs.append(pl.BlockSpec((tm, ksize or a.shape[1]), lambda i, j: (i, kidx), **a_mode))
    for _, w, prefix in w_list:
        in_specs.append(pl.BlockSpec((None,) * len(prefix) + (ksize or w.shape[-2], tn),
                                     lambda i, j, prefix=prefix: tuple(prefix) + (kidx, j)))
    for arr, kind, arg in extras:
        if kind == "mn":
            in_specs.append(pl.BlockSpec((tm, tn), lambda i, j, arg=arg: (i, j + arg)))
        else:
            in_specs.append(pl.BlockSpec((1, tn), lambda i, j: (0, j)))
    pairs = tuple(ai for ai, _, _ in w_list)
    operands = [*a_list, *[w for _, w, _ in w_list], *[arr for arr, _, _ in extras]]
    scratch = []
    if norm_gain is not None:
        kdim = a_list[0].shape[1]
        in_specs.append(pl.BlockSpec((1, kdim), lambda i, j: (0, 0)))
        operands.append(norm_gain.reshape(1, kdim).astype(F32))
        scratch.append(pltpu.VMEM((tm, kdim), BF16))
    body = functools.partial(_mm_body, n_a=len(a_list), pairs=pairs, n_extra=len(extras), epilogue=epilogue,
                             normed=norm_gain is not None)
    return pl.pallas_call(
        body,
        grid=(m // tm, n // tn),
        in_specs=in_specs,
        out_specs=pl.BlockSpec((tm, tn), lambda i, j: (i, j)),
        out_shape=jax.ShapeDtypeStruct((m, n), out_dtype),
        scratch_shapes=scratch,
        compiler_params=_params(("parallel", "arbitrary")),
        name=name,
    )(*operands)


def _ep_plain(ps):
    return ps[0]


def _ep_residual(ps, res):
    return res + ps[0]


def _ep_swiglu(ps):
    g = ps[0]
    return g * _sigmoid(g) * ps[1]


def _ep_glu(ps):
    return ps[0] * _sigmoid(ps[1])


def _ep_branches(ps, g0, g1, g2):
    return _sigmoid(g0) * ps[0] + _sigmoid(g1) * ps[1] + _sigmoid(g2) * ps[2]


def _s5_body(u_ref, wre_ref, wim_ref, a_ref, cre_ref, cim_ref, d_ref, y_ref,
             bre_ref, bim_ref, xr_ref, xi_ref, car_ref, *, steps):
    t = pl.program_id(2)
    lw = bre_ref.shape[-1]

    @pl.when(t == 0)
    def _():
        car_ref[...] = jnp.zeros_like(car_ref)

    u = u_ref[0]
    ub = u.astype(BF16)
    bre_ref[...] = jnp.dot(ub, wre_ref[...], preferred_element_type=F32)
    bim_ref[...] = jnp.dot(ub, wim_ref[...], preferred_element_type=F32)

    ar = jnp.broadcast_to(a_ref[0:1, :], (SUBLANE, lw))
    ai = jnp.broadcast_to(a_ref[1:2, :], (SUBLANE, lw))

    def pass1(j, carry):
        xr, xi = carry
        off = pl.multiple_of(j * SUBLANE, SUBLANE)
        nr = ar * xr - ai * xi + bre_ref[pl.ds(off, SUBLANE), :]
        ni = ar * xi + ai * xr + bim_ref[pl.ds(off, SUBLANE), :]
        xr_ref[pl.ds(off, SUBLANE), :] = nr
        xi_ref[pl.ds(off, SUBLANE), :] = ni
        return nr, ni

    zero = jnp.zeros((SUBLANE, lw), F32)
    er, ei = lax.fori_loop(0, steps, pass1, (zero, zero))

    pr = a_ref[2:3, :]
    pi = a_ref[3:4, :]
    cr = car_ref[0:1, :]
    ci = car_ref[1:2, :]
    rows_r, rows_i = [], []
    for s in range(SUBLANE):
        rows_r.append(cr)
        rows_i.append(ci)
        nr = pr * cr - pi * ci + er[s:s + 1, :]
        ni = pr * ci + pi * cr + ei[s:s + 1, :]
        cr, ci = nr, ni
    car_ref[0:1, :] = cr
    car_ref[1:2, :] = ci
    cin_r = jnp.concatenate(rows_r, axis=0)
    cin_i = jnp.concatenate(rows_i, axis=0)

    def pass2(j, carry):
        cr_, ci_ = carry
        off = pl.multiple_of(j * SUBLANE, SUBLANE)
        nr = ar * cr_ - ai * ci_
        ni = ar * ci_ + ai * cr_
        xr_ref[pl.ds(off, SUBLANE), :] = xr_ref[pl.ds(off, SUBLANE), :] + nr
        xi_ref[pl.ds(off, SUBLANE), :] = xi_ref[pl.ds(off, SUBLANE), :] + ni
        return nr, ni

    lax.fori_loop(0, steps, pass2, (cin_r, cin_i))

    y = (jnp.dot(xr_ref[...].astype(BF16), cre_ref[...], preferred_element_type=F32)
         + jnp.dot(xi_ref[...].astype(BF16), cim_ref[...], preferred_element_type=F32) + d_ref[...] * u)
    y_ref[0] = jax.nn.gelu(y).astype(y_ref.dtype)


def _s5_core(u, bin_re, bin_im, a_tab, bout_re, bout_im, d_skip, *, chunk):
    b, s, w = u.shape
    ns, _, sw = bin_re.shape
    steps = chunk // SUBLANE
    lanes = pl.BlockSpec((1, chunk, LANE), lambda bi, li, ti: (bi, ti, li))
    slab_in = pl.BlockSpec((None, LANE, sw), lambda bi, li, ti: (li, 0, 0))
    slab_out = pl.BlockSpec((None, sw, LANE), lambda bi, li, ti: (li, 0, 0))
    return pl.pallas_call(
        functools.partial(_s5_body, steps=steps),
        grid=(b, ns, s // chunk),
        in_specs=[lanes, slab_in, slab_in, pl.BlockSpec((4, sw), lambda bi, li, ti: (0, li)), slab_out, slab_out,
                  pl.BlockSpec((1, LANE), lambda bi, li, ti: (0, li))],
        out_specs=lanes,
        out_shape=jax.ShapeDtypeStruct((b, s, w), BF16),
        scratch_shapes=[pltpu.VMEM((chunk, sw), F32)] * 4 + [pltpu.VMEM((SUBLANE, sw), F32)],
        compiler_params=_params(("parallel", "parallel", "arbitrary")),
        name="s5_core",
    )(u, bin_re, bin_im, a_tab, bout_re, bout_im, d_skip)


def _s5_chunk(s):
    return _pick(s, (512, 256, 128, 64))


def _to_chunk_order(t, chunk):
    b, s, w = t.shape
    return t.reshape(b, s // chunk, SUBLANE, chunk // SUBLANE, w).transpose(0, 1, 3, 2, 4).reshape(b, s, w)


def _from_chunk_order(t, chunk):
    b, s, w = t.shape
    return t.reshape(b, s // chunk, chunk // SUBLANE, SUBLANE, w).transpose(0, 1, 3, 2, 4).reshape(b, s, w)


def _s5_discretise(lam_re, lam_im, log_dt, b_re, b_im, c_re, c_im, steps):
    g, p = lam_re.shape
    h = b_re.shape[-1]
    lr, li = lam_re.astype(F32), lam_im.astype(F32)
    dt = jnp.exp(log_dt.astype(F32))[:, None]
    mag = jnp.exp(lr * dt)
    ab_re, ab_im = mag * jnp.cos(li * dt), mag * jnp.sin(li * dt)
    den = lr * lr + li * li
    nr = ab_re - 1.0
    zr = (nr * lr + ab_im * li) / den
    zi = (ab_im * lr - nr * li) / den
    bb_re = zr[..., None] * b_re - zi[..., None] * b_im
    bb_im = zr[..., None] * b_im + zi[..., None] * b_re
    gs = LANE // h
    ns = g // gs
    eye = jnp.eye(gs, dtype=F32)
    slab_in = lambda t: jnp.einsum("sgph,gk->sghkp", t.reshape(ns, gs, p, h), eye).reshape(ns, gs * h, gs * p)
    slab_out = lambda t: jnp.einsum("sghp,gk->sgpkh", t.reshape(ns, gs, h, p), eye).reshape(ns, gs * p, gs * h)
    pr, pi = ab_re, ab_im
    for _ in range(int(round(math.log2(steps)))):
        pr, pi = pr * pr - pi * pi, 2.0 * pr * pi
    a_tab = jnp.stack([ab_re.reshape(-1), ab_im.reshape(-1), pr.reshape(-1), pi.reshape(-1)])
    return (slab_in(bb_re).astype(BF16), slab_in(bb_im).astype(BF16), slab_out(c_re.astype(F32)).astype(BF16),
            slab_out(-c_im.astype(F32)).astype(BF16), a_tab)


def _s5_mixer(u, bsz, seq, lam_re, lam_im, log_dt, b_re, b_im, c_re, c_im, d_skip, w_glu):
    n, w = u.shape
    chunk = _s5_chunk(seq)
    steps = chunk // SUBLANE
    bin_re, bin_im, bout_re, bout_im, a_tab = _s5_discretise(lam_re, lam_im, log_dt, b_re, b_im, c_re, c_im, steps)
    u_p = _to_chunk_order(u.reshape(bsz, seq, w), chunk)
    y = _s5_core(u_p, bin_re, bin_im, a_tab, bout_re, bout_im, d_skip.reshape(1, w).astype(F32), chunk=chunk)
    y = y.reshape(n, w)
    tnw = _pick(w, (512, 256, 128))
    w_a = w_glu[:, :w].astype(BF16)
    w_g = w_glu[:, w:].astype(BF16)
    out = _matmul([y], [(0, w_a, ()), (0, w_g, ())], epilogue=_ep_glu, out_dtype=BF16, tn=tnw, name="s5_glu")
    return _from_chunk_order(out.reshape(bsz, seq, w), chunk).reshape(n, w)


def _rope_table_body(pos_ref, inv_ref, cos_ref, sin_ref):
    ang = pos_ref[...].astype(F32) * inv_ref[...]
    cos_ref[...] = jnp.cos(ang)
    sin_ref[...] = jnp.sin(ang)


def _rope_tables(positions):
    n = positions.size
    half = MLA_ROPE // 2
    inv = ROPE_THETA ** (-jnp.arange(half, dtype=F32) / half)
    inv = jnp.zeros((1, LANE), F32).at[0, :half].set(inv)
    tm = _pick(n, (512, 256, 128, 64, 32, 16, 8))
    return pl.pallas_call(
        _rope_table_body,
        grid=(n // tm,),
        in_specs=[pl.BlockSpec((tm, 1), lambda i: (i, 0)), pl.BlockSpec((1, LANE), lambda i: (0, 0))],
        out_specs=[pl.BlockSpec((tm, LANE), lambda i: (i, 0))] * 2,
        out_shape=[jax.ShapeDtypeStruct((n, LANE), F32)] * 2,
        compiler_params=_params(("parallel",)),
        name="rope_table",
    )(positions.reshape(n, 1).astype(I32), inv)


def _head_norm_rope(t0, t1, t2, g0, g1, g2, cos, sin, true_dim):
    ss = (jnp.sum(t0 * t0, axis=-1, keepdims=True) + jnp.sum(t1 * t1, axis=-1, keepdims=True)
          + jnp.sum(t2 * t2, axis=-1, keepdims=True))
    inv = lax.rsqrt(ss * (1.0 / true_dim) + EPS)
    n0 = t0 * inv * g0
    n1 = t1 * inv * g1
    n2 = t2 * inv * g2
    o1 = n1 * cos - n2 * sin
    o2 = n1 * sin + n2 * cos
    half = MLA_ROPE // 2
    lane = lax.broadcasted_iota(I32, o1.shape, 1)
    o1 = jnp.where(lane < half, o1, 0.0)
    o2 = jnp.where(lane < half, o2, 0.0)
    return jnp.concatenate([n0, o1 + pltpu.roll(o2, half, 1)], axis=-1)


def _mla_qkv_body(cq_ref, ckv_ref, pe_ref, gq_lat_ref, gkv_lat_ref, wq_ref, wkv_ref, qg_ref, kg_ref, cos_ref, sin_ref,
                  q_ref, k_ref, v_ref):
    def lat_norm(ref, g_ref):
        t = ref[...]
        ms = jnp.mean(t * t, axis=-1, keepdims=True)
        return (t * lax.rsqrt(ms + EPS) * g_ref[...]).astype(BF16)

    cqn = lat_norm(cq_ref, gq_lat_ref)
    ckvn = lat_norm(ckv_ref, gkv_lat_ref)
    cos, sin = cos_ref[...], sin_ref[...]
    hw = 3 * LANE
    qg = (qg_ref[:, 0:LANE], qg_ref[:, LANE:2 * LANE], qg_ref[:, 2 * LANE:hw])
    kg = (kg_ref[:, 0:LANE], kg_ref[:, LANE:2 * LANE], kg_ref[:, 2 * LANE:hw])
    pe1 = pe_ref[:, 0:LANE]
    pe2 = pe_ref[:, LANE:2 * LANE]
    kvw = MLA_NOPE + MLA_V
    qk = MLA_NOPE + MLA_ROPE
    for h in range(MLA_HEADS):
        qh = jnp.dot(cqn, wq_ref[:, h * hw:(h + 1) * hw], preferred_element_type=F32)
        qh = _head_norm_rope(qh[:, 0:LANE], qh[:, LANE:2 * LANE], qh[:, 2 * LANE:hw], *qg, cos, sin, qk)
        q_ref[0, h] = (qh * (qk ** -0.5 * LOG2E)).astype(q_ref.dtype)
        kvh = jnp.dot(ckvn, wkv_ref[:, h * kvw:(h + 1) * kvw], preferred_element_type=F32)
        k_ref[0, h] = _head_norm_rope(kvh[:, 0:MLA_NOPE], pe1, pe2, *kg, cos, sin, qk).astype(k_ref.dtype)
        v_ref[0, h] = kvh[:, MLA_NOPE:kvw].astype(v_ref.dtype)


def _pad_rope_gain(gain):
    half = MLA_ROPE // 2
    out = jnp.zeros((3 * LANE,), F32)
    out = out.at[:MLA_NOPE].set(gain[:MLA_NOPE].astype(F32))
    out = out.at[LANE:LANE + half].set(gain[MLA_NOPE:MLA_NOPE + half].astype(F32))
    out = out.at[2 * LANE:2 * LANE + half].set(gain[MLA_NOPE + half:].astype(F32))
    return out.reshape(1, 3 * LANE)


def _flash_body(q_ref, k_ref, v_ref, o_ref, *, tq, wide):
    qi = pl.program_id(2)
    dv = v_ref.shape[-1]
    q = q_ref[0, 0]

    def step(off, width, carry, masked):
        m, l, acc = carry
        k = k_ref[0, 0, pl.ds(off, width), :]
        v = v_ref[0, 0, pl.ds(off, width), :]
        s = lax.dot_general(q, k, (((1,), (1,)), ((), ())), preferred_element_type=F32)
        if masked:
            row = lax.broadcasted_iota(I32, (tq, width), 0)
            col = lax.broadcasted_iota(I32, (tq, width), 1)
            s = jnp.where(col <= row, s, NEG_BIG)
        m_new = jnp.maximum(m, jnp.max(s, axis=1, keepdims=True))
        alpha = jnp.exp2(m - m_new)
        p = jnp.exp2(s - m_new)
        l = alpha * l + jnp.sum(p, axis=1, keepdims=True)
        acc = alpha * acc + jnp.dot(p.astype(BF16), v, preferred_element_type=F32)
        return m_new, l, acc

    per = wide // tq
    n_wide = qi // per
    carry = (jnp.full((tq, 1), NEG_BIG, F32), jnp.zeros((tq, 1), F32), jnp.zeros((tq, dv), F32))
    carry = lax.fori_loop(0, n_wide, lambda j, c: step(pl.multiple_of(j * wide, wide), wide, c, False), carry)
    carry = lax.fori_loop(n_wide * per, qi, lambda j, c: step(pl.multiple_of(j * tq, tq), tq, c, False), carry)
    m, l, acc = step(pl.multiple_of(qi * tq, tq), tq, carry, True)
    o_ref[0] = (acc / l).astype(o_ref.dtype)


def _flash_attention(q, k, v):
    b, h, s, dk = q.shape
    dv = v.shape[-1]
    tq = _pick(s, (1024, 512, 256, 128))
    wide = _pick(s, (2 * tq, tq))
    return pl.pallas_call(
        functools.partial(_flash_body, tq=tq, wide=wide),
        grid=(b, h, s // tq),
        in_specs=[pl.BlockSpec((1, 1, tq, dk), lambda bi, hi, qi: (bi, hi, qi, 0)),
                  pl.BlockSpec((1, 1, s, dk), lambda bi, hi, qi: (bi, hi, 0, 0)),
                  pl.BlockSpec((1, 1, s, dv), lambda bi, hi, qi: (bi, hi, 0, 0))],
        out_specs=pl.BlockSpec((1, tq, dv), lambda bi, hi, qi: (bi, qi, hi)),
        out_shape=jax.ShapeDtypeStruct((b, s, h * dv), BF16),
        compiler_params=_params(("parallel", "parallel", "arbitrary")),
        name="mla_flash",
    )(q, k, v)


def _mla_mixer(hm, col, cos, sin, bsz, seq, q_lora, kv_lora, q_lat_gain, kv_lat_gain, w_uq, w_ukv, q_gain, k_gain):
    n = hm.shape[0]
    half = MLA_ROPE // 2
    qk = MLA_NOPE + MLA_ROPE
    wq = w_uq.reshape(q_lora, MLA_HEADS, qk)
    wq_p = jnp.zeros((q_lora, MLA_HEADS, 3 * LANE), BF16)
    wq_p = wq_p.at[:, :, :MLA_NOPE].set(wq[:, :, :MLA_NOPE].astype(BF16))
    wq_p = wq_p.at[:, :, LANE:LANE + half].set(wq[:, :, MLA_NOPE:MLA_NOPE + half].astype(BF16))
    wq_p = wq_p.at[:, :, 2 * LANE:2 * LANE + half].set(wq[:, :, MLA_NOPE + half:].astype(BF16))
    wq_p = wq_p.reshape(q_lora, MLA_HEADS * 3 * LANE)
    wkv = w_ukv.astype(BF16)
    tm = _pick(seq, (256, 128, 64, 32, 16))
    nblk = seq // tm
    cols = lambda width, block: pl.BlockSpec((tm, width), lambda bi, si: (bi * nblk + si, block))
    const = lambda arr: pl.BlockSpec(arr.shape, lambda bi, si: (0, 0), pipeline_mode=pl.Buffered(1))
    heads = lambda width: pl.BlockSpec((1, MLA_HEADS, tm, width), lambda bi, si: (bi, 0, si, 0))
    gq_lat = q_lat_gain.reshape(1, q_lora).astype(F32)
    gkv_lat = kv_lat_gain.reshape(1, kv_lora).astype(F32)
    qg = _pad_rope_gain(q_gain)
    kg = _pad_rope_gain(k_gain)
    q, k, v = pl.pallas_call(
        _mla_qkv_body,
        grid=(bsz, nblk),
        in_specs=[cols(q_lora, col["c_q"] // q_lora), cols(kv_lora, col["c_kv"] // kv_lora),
                  cols(2 * LANE, col["pe"] // (2 * LANE)),
                  const(gq_lat), const(gkv_lat), const(wq_p), const(wkv), const(qg), const(kg),
                  cols(LANE, 0), cols(LANE, 0)],
        out_specs=[heads(2 * LANE), heads(2 * LANE), heads(MLA_V)],
        out_shape=[jax.ShapeDtypeStruct((bsz, MLA_HEADS, seq, 2 * LANE), BF16),
                   jax.ShapeDtypeStruct((bsz, MLA_HEADS, seq, 2 * LANE), BF16),
                   jax.ShapeDtypeStruct((bsz, MLA_HEADS, seq, MLA_V), BF16)],
        compiler_params=_params(("parallel", "parallel")),
        name="mla_qkv",
    )(hm, hm, hm, gq_lat, gkv_lat, wq_p, wkv, qg, kg, cos, sin)
    return _flash_attention(q, k, v).reshape(n, MLA_HEADS * MLA_V)


def _dsa_prep_body(hd_q_ref, hd_k_ref, hd_v_ref, hi_q_ref, hi_k_ref, hi_w_ref, qg_ref, kg_ref,
                   q_ref, k_ref, v_ref, qi_ref, klo_ref, khi_ref, w_ref):
    def norm(t, g):
        ms = jnp.mean(t * t, axis=-1, keepdims=True)
        return t * lax.rsqrt(ms + EPS) * g

    qg = qg_ref[...]
    for h in range(DSA_HEADS):
        qh = norm(hd_q_ref[:, h * DSA_HEAD_DIM:(h + 1) * DSA_HEAD_DIM], qg)
        q_ref[0, h] = (qh * (DSA_HEAD_DIM ** -0.5 * LOG2E)).astype(q_ref.dtype)
    k_ref[0] = norm(hd_k_ref[...], kg_ref[...]).astype(k_ref.dtype)
    v_ref[0] = hd_v_ref[...].astype(v_ref.dtype)
    qi_ref[0] = hi_q_ref[...].astype(qi_ref.dtype)
    ki = hi_k_ref[...]
    lane = lax.broadcasted_iota(I32, ki.shape, 1)
    ki = jnp.where(lane < IDX_DIM, ki, 0.0)
    klo_ref[0] = ki.astype(klo_ref.dtype)
    khi_ref[0] = pltpu.roll(ki, IDX_DIM, 1).astype(khi_ref.dtype)
    w_ref[0] = hi_w_ref[...] * ((IDX_HEADS ** -0.5) * (IDX_DIM ** -0.5))


def _float_key(x):
    bits = pltpu.bitcast(x, I32)
    return jnp.where(bits < 0, bits ^ jnp.int32(0x7FFFFFFF), bits)


def _dsa_body(q_ref, k_ref, v_ref, qi_ref, klo_ref, khi_ref, w_ref, o_ref, keys_ref, gmax_ref, acc_ref,
              *, tq, tk, wide, n_sel, seq):
    qb = pl.program_id(1)
    n_kv = (qb * tq + tq + tk - 1) // tk
    row = lax.broadcasted_iota(I32, (tq, tk), 0) + qb * tq
    col0 = lax.broadcasted_iota(I32, (tq, tk), 1)

    w = w_ref[0]

    def score_tile(j, _):
        off = pl.multiple_of(j * tk, tk)
        klo = klo_ref[0, pl.ds(off, tk), :]
        khi = khi_ref[0, pl.ds(off, tk), :]
        acc = jnp.zeros((tq, tk), F32)
        for hp in range(IDX_HEADS // 2):
            q2 = qi_ref[0, :, hp * LANE:(hp + 1) * LANE]
            for half, kk in ((0, klo), (1, khi)):
                h = 2 * hp + half
                logit = lax.dot_general(q2, kk, (((1,), (1,)), ((), ())), preferred_element_type=F32)
                acc = acc + w[:, h:h + 1] * jnp.maximum(logit, 0.0)
        causal = col0 + off <= row
        keys_ref[:, pl.ds(off, tk)] = jnp.where(causal, _float_key(acc + 0.0), INT_MIN)
        sc = jnp.where(causal, acc, -jnp.inf)
        for c in range(tk // LANE):
            gmax_ref[c % 2] = jnp.maximum(gmax_ref[c % 2], sc[:, c * LANE:(c + 1) * LANE])
        return 0

    gmax_ref[...] = jnp.full(gmax_ref.shape, -jnp.inf, F32)
    lax.fori_loop(0, n_kv, score_tile, 0)

    rc = min(tq, 128)

    def row_group(g, carry):
        r0 = pl.multiple_of(g * rc, rc)
        ge = gmax_ref[0, pl.ds(r0, rc), :]
        go = gmax_ref[1, pl.ds(r0, rc), :]
        key_l = _float_key(jnp.min(jnp.minimum(ge, go), axis=1, keepdims=True) + 0.0)
        key_u = _float_key(jnp.max(jnp.maximum(ge, go), axis=1, keepdims=True) + 0.0)
        span = key_u - key_l
        n_iter = jnp.max(jnp.where(span < 0, 32, 32 - lax.clz(span)))
        lo0 = jnp.broadcast_to(key_l, (rc, LANE))
        hi0 = jnp.broadcast_to(key_u + 1, (rc, LANE))

        lane = lax.broadcasted_iota(I32, (rc, LANE), 1)

        def count(pred):
            def count_tile(j, part):
                off = pl.multiple_of(j * tk, tk)
                for c in range(tk // LANE):
                    kc = keys_ref[pl.ds(r0, rc), pl.ds(off + c * LANE, LANE)]
                    part = part + jnp.where(pred(kc, lane + (off + c * LANE)), 1.0, 0.0)
                return part

            part = lax.fori_loop(0, n_kv, count_tile, jnp.zeros((rc, LANE), F32))
            return jnp.sum(part, axis=1, keepdims=True)

        def set_bias(selected):
            def bias_tile(j, c2):
                off = pl.multiple_of(j * tk, tk)
                for c in range(tk // LANE):
                    kc = keys_ref[pl.ds(r0, rc), pl.ds(off + c * LANE, LANE)]
                    bias = jnp.where(selected(kc, lane + (off + c * LANE)), 0.0, NEG_BIG)
                    keys_ref[pl.ds(r0, rc), pl.ds(off + c * LANE, LANE)] = pltpu.bitcast(bias, I32)
                return c2

            lax.fori_loop(0, n_kv, bias_tile, 0)

        def halve(b, bounds):
            lo, hi = bounds
            cand = (lo >> 1) + (hi >> 1) + (lo & hi & 1)
            enough = count(lambda kc, col: kc >= cand) >= n_sel
            return jnp.where(enough, cand, lo), jnp.where(enough, hi, cand)

        thr, _ = lax.fori_loop(0, n_iter, halve, (lo0, hi0))
        thr = jnp.maximum(thr, INT_MIN + 1)
        tied = jnp.max(count(lambda kc, col: kc >= thr)) > n_sel

        @pl.when(jnp.logical_not(tied))
        def _():
            set_bias(lambda kc, col: kc >= thr)

        @pl.when(tied)
        def _():
            need = n_sel - count(lambda kc, col: kc > thr)

            def narrow(b, bounds):
                below, upto = bounds
                mid = (below + upto) >> 1
                ok = count(lambda kc, col: jnp.where(kc == thr, col, seq) <= mid) >= need
                return jnp.where(ok, below, mid), jnp.where(ok, mid, upto)

            first = (jnp.full((rc, LANE), -1, I32), jnp.full((rc, LANE), seq - 1, I32))
            _, last = lax.fori_loop(0, max(1, (seq - 1).bit_length()) + 1, narrow, first)
            set_bias(lambda kc, col: jnp.where(kc == thr, col, jnp.where(kc > thr, -1, seq)) <= last)

        return carry

    lax.fori_loop(0, tq // rc, row_group, 0)

    per = wide // tk
    n_wide = n_kv // per

    def head(h, carry):
        q = q_ref[0, h]

        def step(off, width, st):
            m, l, acc = st
            k = k_ref[0, pl.ds(off, width), :]
            v = v_ref[0, pl.ds(off, width), :]
            bias = pltpu.bitcast(keys_ref[:, pl.ds(off, width)], F32)
            s = lax.dot_general(q, k, (((1,), (1,)), ((), ())), preferred_element_type=F32) + bias
            m_new = jnp.maximum(m, jnp.max(s, axis=1, keepdims=True))
            alpha = jnp.exp2(m - m_new)
            p = jnp.exp2(s - m_new)
            l = alpha * l + jnp.sum(p, axis=1, keepdims=True)
            acc = alpha * acc + jnp.dot(p.astype(BF16), v, preferred_element_type=F32)
            return m_new, l, acc

        st = (jnp.full((tq, 1), NEG_BIG, F32), jnp.zeros((tq, 1), F32), jnp.zeros((tq, DSA_HEAD_DIM), F32))
        st = lax.fori_loop(0, n_wide, lambda j, c: step(pl.multiple_of(j * wide, wide), wide, c), st)
        st = lax.fori_loop(n_wide * per, n_kv, lambda j, c: step(pl.multiple_of(j * tk, tk), tk, c), st)
        m, l, acc = st
        acc_ref[h] = acc / l
        return carry

    lax.fori_loop(0, DSA_HEADS, head, 0)
    for h in range(DSA_HEADS):
        o_ref[0, :, h * DSA_HEAD_DIM:(h + 1) * DSA_HEAD_DIM] = acc_ref[h].astype(o_ref.dtype)


def _dsa_mixer(hd, col, bsz, seq, q_gain, k_gain):
    n = hd.shape[0]
    qw = DSA_HEADS * DSA_HEAD_DIM
    iw = IDX_HEADS * IDX_DIM
    tm = _pick(seq, (256, 128, 64, 32, 16))
    nblk = seq // tm

    def cols(width, block):
        return pl.BlockSpec((tm, width), lambda bi, si: (bi * nblk + si, block))

    per_tok = lambda width: pl.BlockSpec((1, tm, width), lambda bi, si: (bi, si, 0))
    q, k, v, qi, klo, khi, w = pl.pallas_call(
        _dsa_prep_body,
        grid=(bsz, nblk),
        in_specs=[cols(qw, col["q_c"] // qw), cols(LANE, col["k_c"] // LANE), cols(LANE, col["v_c"] // LANE),
                  cols(iw, col["q_i"] // iw), cols(LANE, col["k_i"] // LANE), cols(LANE, col["w_i"] // LANE),
                  pl.BlockSpec((1, DSA_HEAD_DIM), lambda bi, si: (0, 0)),
                  pl.BlockSpec((1, DSA_HEAD_DIM), lambda bi, si: (0, 0))],
        out_specs=[pl.BlockSpec((1, DSA_HEADS, tm, DSA_HEAD_DIM), lambda bi, si: (bi, 0, si, 0)),
                   per_tok(DSA_HEAD_DIM), per_tok(DSA_HEAD_DIM), per_tok(iw), per_tok(LANE), per_tok(LANE),
                   per_tok(LANE)],
        out_shape=[jax.ShapeDtypeStruct((bsz, DSA_HEADS, seq, DSA_HEAD_DIM), BF16),
                   jax.ShapeDtypeStruct((bsz, seq, DSA_HEAD_DIM), BF16),
                   jax.ShapeDtypeStruct((bsz, seq, DSA_HEAD_DIM), BF16),
                   jax.ShapeDtypeStruct((bsz, seq, iw), BF16),
                   jax.ShapeDtypeStruct((bsz, seq, LANE), BF16),
                   jax.ShapeDtypeStruct((bsz, seq, LANE), BF16),
                   jax.ShapeDtypeStruct((bsz, seq, LANE), F32)],
        compiler_params=_params(("parallel", "parallel")),
        name="dsa_prep",
    )(hd, hd, hd, hd, hd, hd, q_gain.reshape(1, -1).astype(F32), k_gain.reshape(1, -1).astype(F32))

    tq = _pick(seq, (512, 256, 128))
    tk = _pick(seq, (512, 256, 128))
    n_sel = min(DSA_TOPK_MAX, seq // 4)
    assert n_sel <= 2 * LANE
    whole = lambda width: pl.BlockSpec((1, seq, width), lambda bi, qb: (bi, 0, 0), pipeline_mode=pl.Buffered(1))
    out = pl.pallas_call(
        functools.partial(_dsa_body, tq=tq, tk=tk, wide=_pick(seq, (4 * tk, 2 * tk, tk)), n_sel=n_sel, seq=seq),
        grid=(bsz, seq // tq),
        in_specs=[pl.BlockSpec((1, DSA_HEADS, tq, DSA_HEAD_DIM), lambda bi, qb: (bi, 0, qb, 0)),
                  whole(DSA_HEAD_DIM), whole(DSA_HEAD_DIM),
                  pl.BlockSpec((1, tq, iw), lambda bi, qb: (bi, qb, 0)),
                  whole(LANE), whole(LANE),
                  pl.BlockSpec((1, tq, LANE), lambda bi, qb: (bi, qb, 0))],
        out_specs=pl.BlockSpec((1, tq, qw), lambda bi, qb: (bi, qb, 0)),
        out_shape=jax.ShapeDtypeStruct((bsz, seq, qw), BF16),
        scratch_shapes=[pltpu.VMEM((tq, seq), I32), pltpu.VMEM((2, tq, LANE), F32),
                        pltpu.VMEM((DSA_HEADS, tq, DSA_HEAD_DIM), F32)],
        compiler_params=_params(("parallel", "arbitrary")),
        name="dsa_attention",
    )(q, k, v, qi, klo, khi, w)
    return out.reshape(n, qw)


def _pad_cols(w, width):
    return jnp.pad(w, ((0, 0), (0, width - w.shape[1])))


def _hybrid_mixer(x2, cos, sin, bsz, seq, norm_gain, w_in, q_lat_gain, kv_lat_gain, w_uq, w_ukv, mla_q_gain,
                  mla_k_gain, dsa_q_gain, dsa_k_gain, lam_re, lam_im, log_dt, b_re, b_im, c_re, c_im, d_skip, w_glu,
                  w_branch, w_out, layer):
    n, d = x2.shape
    ssm_w = d_skip.shape[-1]
    q_lora = q_lat_gain.shape[-1]
    kv_lora = kv_lat_gain.shape[-1]
    half = MLA_ROPE // 2
    dsa_w = DSA_HEADS * DSA_HEAD_DIM
    idx_w = IDX_HEADS * IDX_DIM
    sizes = (ssm_w, q_lora, kv_lora, MLA_ROPE, dsa_w, DSA_HEAD_DIM, DSA_HEAD_DIM, idx_w, IDX_DIM, IDX_HEADS, 3 * d)
    offs = [0]
    for s in sizes:
        offs.append(offs[-1] + s)
    seg = lambda i: w_in[:, offs[i]:offs[i + 1]]
    pe = seg(3)
    tn = _pick(d, (512, 256, 128))
    pe_pad = jnp.concatenate([_pad_cols(pe[:, :half], LANE), _pad_cols(pe[:, half:], LANE)], axis=1)
    parts = [("q_c", seg(4), dsa_w), ("c_kv", seg(2), kv_lora), ("u", seg(0), LANE), ("c_q", seg(1), q_lora),
             ("q_i", seg(7), idx_w), ("pe", pe_pad, 2 * LANE), ("k_c", seg(5), LANE), ("v_c", seg(6), LANE),
             ("k_i", _pad_cols(seg(8), LANE), LANE), ("w_i", _pad_cols(seg(9), LANE), LANE)]
    col = {}
    at = 0
    for name, part, block_width in parts:
        assert at % block_width == 0, (name, at, block_width)
        col[name] = at
        at += part.shape[1]
    gap = -at % tn
    col["gates"] = at + gap
    w_cat = jnp.concatenate([part for _, part, _ in parts] + [jnp.zeros((d, gap), F32), seg(10)],
                            axis=1).astype(BF16)
    h = _matmul([x2], [(0, w_cat, ())], epilogue=_ep_plain, out_dtype=F32, norm_gain=norm_gain,
                tn=_pick(w_cat.shape[1], (768, 512, 256, 128)), name="in_proj")

    u = h[:, col["u"]:col["u"] + ssm_w]
    y_ssm = _s5_mixer(u, bsz, seq, lam_re, lam_im, log_dt, b_re, b_im, c_re, c_im, d_skip, w_glu)
    y_mla = _mla_mixer(h, col, cos, sin, bsz, seq, q_lora, kv_lora, q_lat_gain, kv_lat_gain, w_uq, w_ukv,
                       mla_q_gain, mla_k_gain)
    y_dsa = _dsa_mixer(h, col, bsz, seq, dsa_q_gain, dsa_k_gain)

    br0 = ssm_w
    br1 = ssm_w + MLA_HEADS * MLA_V
    g0 = col["gates"] // tn
    merged = _matmul([y_ssm, y_mla, y_dsa],
                     [(0, w_branch[:br0].astype(BF16), ()), (1, w_branch[br0:br1].astype(BF16), ()),
                      (2, w_branch[br1:].astype(BF16), ())],
                     epilogue=_ep_branches, out_dtype=BF16, tn=tn,
                     extras=[(h, "mn", g0), (h, "mn", g0 + d // tn), (h, "mn", g0 + 2 * (d // tn))],
                     name="branch_merge")
    return _matmul([merged], [(0, w_out, (layer,))], epilogue=_ep_residual, out_dtype=F32, tn=tn,
                   extras=[(x2, "mn", 0)], name="mixer_out")


def _dense_ffn(x2, gain, w_gate, w_up, w_down, idx):
    d = x2.shape[1]
    dff = w_gate.shape[-1]
    xn = _rmsnorm(x2, gain)
    tn = _pick(dff, (256, 128))
    h = _matmul([xn], [(0, w_gate, (idx,)), (0, w_up, (idx,))], epilogue=_ep_swiglu, out_dtype=BF16, tn=tn,
                name="ffn_up")
    tnd = _pick(d, (256, 128))
    n_slabs = 2 if dff % (2 * LANE) == 0 and dff > 8192 else 1
    out = x2
    for s in range(n_slabs):
        out = _matmul([h], [(0, w_down, (idx,))], epilogue=_ep_residual, out_dtype=F32, tn=tnd,
                      k_slice=(dff // n_slabs, s), extras=[(out, "mn", 0)], name="ffn_down")
    return out


def _router_body(x_ref, g_ref, w_ref, xn_ref, meta_ref, cnt_ref, run_ref, *, n_experts):
    i = pl.program_id(0)

    @pl.when(i == 0)
    def _():
        run_ref[...] = jnp.zeros_like(run_ref)

    x = x_ref[...]
    tm = x.shape[0]
    ms = jnp.mean(x * x, axis=-1, keepdims=True)
    xn = x * lax.rsqrt(ms + EPS) * g_ref[...]
    xn_ref[...] = xn.astype(xn_ref.dtype)
    logits = jnp.dot(xn, w_ref[...], preferred_element_type=F32, precision=lax.Precision.HIGHEST)
    lane = lax.broadcasted_iota(I32, logits.shape, 1).astype(F32)
    lg = jnp.where(lane < n_experts, logits, -jnp.inf)
    m1 = jnp.max(lg, axis=1, keepdims=True)
    i1 = jnp.min(jnp.where(lg == m1, lane, float(LANE)), axis=1, keepdims=True)
    lg2 = jnp.where(lane == i1, -jnp.inf, lg)
    m2 = jnp.max(lg2, axis=1, keepdims=True)
    i2 = jnp.min(jnp.where(lg2 == m2, lane, float(LANE)), axis=1, keepdims=True)
    e2 = jnp.exp(m2 - m1)
    den = 1.0 + e2
    oh1 = jnp.where(lane == i1, 1.0, 0.0)
    oh2 = jnp.where(lane == i2, 1.0, 0.0)
    oh = oh1 + oh2
    r = lax.broadcasted_iota(I32, (tm, tm), 0)
    c = lax.broadcasted_iota(I32, (tm, tm), 1)
    tri = jnp.where(c < r, 1.0, 0.0).astype(BF16)
    base = run_ref[0:1, :]
    rank = jnp.dot(tri, oh.astype(BF16), preferred_element_type=F32) + base
    r1 = jnp.sum(oh1 * rank, axis=1, keepdims=True)
    r2 = jnp.sum(oh2 * rank, axis=1, keepdims=True)
    total = base + jnp.sum(oh, axis=0, keepdims=True)
    run_ref[0:1, :] = total
    cnt_ref[...] = jnp.broadcast_to(total, cnt_ref.shape)
    meta = jnp.zeros_like(logits)
    for k, val in enumerate((i1, i2, r1, r2, 1.0 / den, e2 / den)):
        meta = jnp.where(lane == float(k), val, meta)
    meta_ref[...] = meta


def _cast_body(x_ref, o_ref):
    o_ref[...] = x_ref[...].astype(o_ref.dtype)


def _expert_weights_bf16(w, idx):
    _, e, r, c = w.shape
    tr = _pick(r, (1024, 512, 256, 128, 64, 32, 16))
    return pl.pallas_call(
        _cast_body,
        grid=(e, r // tr),
        in_specs=[pl.BlockSpec((None, 1, tr, c), lambda i, j: (idx, i, j, 0))],
        out_specs=pl.BlockSpec((1, tr, c), lambda i, j: (i, j, 0)),
        out_shape=jax.ShapeDtypeStruct((e, r, c), BF16),
        compiler_params=_params(("parallel", "parallel")),
        name="moe_weight_cast",
    )(w)


def _gather_rows_body(src_ref, x_hbm, o_ref, sem, *, rows):
    base = pl.program_id(0) * rows

    def issue(r, carry):
        pltpu.make_async_copy(x_hbm.at[src_ref[base + r]], o_ref.at[r], sem).start()
        return carry

    lax.fori_loop(0, rows, issue, 0)
    pltpu.make_async_copy(x_hbm.at[pl.ds(0, rows)], o_ref, sem).wait()


def _gather_rows(x, src, *, rows):
    n_out = src.shape[0]
    n, d = x.shape
    x3 = x.reshape(n, d // LANE, LANE)
    out = pl.pallas_call(
        functools.partial(_gather_rows_body, rows=rows),
        grid_spec=pltpu.PrefetchScalarGridSpec(
            num_scalar_prefetch=1,
            grid=(n_out // rows,),
            in_specs=[pl.BlockSpec(memory_space=pl.ANY)],
            out_specs=pl.BlockSpec((rows, d // LANE, LANE), lambda i, src_ref: (i, 0, 0)),
            scratch_shapes=[pltpu.SemaphoreType.DMA(())],
        ),
        out_shape=jax.ShapeDtypeStruct((n_out, d // LANE, LANE), x.dtype),
        compiler_params=_params(("arbitrary",)),
        name="moe_gather",
    )(src, x3)
    return out.reshape(n_out, d)


def _experts_body(tile_ref, wexp_ref, lo_ref, hi_ref, first_ref, xs_ref, w1_ref, w3_ref, w2_ref, o_ref, *, tm):
    v = pl.program_id(0)
    f = pl.program_id(1)

    @pl.when((f == 0) & (first_ref[v] == 1))
    def _():
        o_ref[...] = jnp.zeros_like(o_ref)

    lo = lo_ref[v]
    hi = hi_ref[v]

    @pl.when(hi > lo)
    def _():
        xs = xs_ref[...]
        g = jnp.dot(xs, w1_ref[...], preferred_element_type=F32)
        u = jnp.dot(xs, w3_ref[...], preferred_element_type=F32)
        row = tile_ref[v] * tm + lax.broadcasted_iota(I32, (tm, 1), 0)
        h = jnp.where((row >= lo) & (row < hi), g * _sigmoid(g) * u, 0.0)
        o_ref[...] += jnp.dot(h.astype(BF16), w2_ref[...], preferred_element_type=F32)


def _combine_body(d1_ref, d2_ref, y_hbm, x_ref, w1_ref, w2_ref, o_ref, b1_ref, b2_ref, sem, *, rows):
    base = pl.program_id(0) * rows

    def issue(r, carry):
        pltpu.make_async_copy(y_hbm.at[d1_ref[base + r]], b1_ref.at[r], sem.at[0]).start()
        pltpu.make_async_copy(y_hbm.at[d2_ref[base + r]], b2_ref.at[r], sem.at[1]).start()
        return carry

    lax.fori_loop(0, rows, issue, 0)
    pltpu.make_async_copy(y_hbm.at[pl.ds(0, rows)], b1_ref, sem.at[0]).wait()
    pltpu.make_async_copy(y_hbm.at[pl.ds(0, rows)], b2_ref, sem.at[1]).wait()
    o_ref[...] = x_ref[...] + w1_ref[...] * b1_ref[...] + w2_ref[...] * b2_ref[...]


def _moe_ffn(x2, gain, w_router, w1, w3, w2, idx):
    n, d = x2.shape
    n_experts = w_router.shape[-1]
    dff = w1.shape[-1]
    tr = _pick(n, (256, 128, 64, 32, 16, 8))
    wr = _pad_cols(w_router[idx].astype(F32), LANE)
    xn, meta, cnt = pl.pallas_call(
        functools.partial(_router_body, n_experts=n_experts),
        grid=(n // tr,),
        in_specs=[pl.BlockSpec((tr, d), lambda i: (i, 0)), pl.BlockSpec((1, d), lambda i: (0, 0)),
                  pl.BlockSpec((d, LANE), lambda i: (0, 0))],
        out_specs=[pl.BlockSpec((tr, d), lambda i: (i, 0)), pl.BlockSpec((tr, LANE), lambda i: (i, 0)),
                   pl.BlockSpec((SUBLANE, LANE), lambda i: (0, 0))],
        out_shape=[jax.ShapeDtypeStruct((n, d), BF16), jax.ShapeDtypeStruct((n, LANE), F32),
                   jax.ShapeDtypeStruct((SUBLANE, LANE), F32)],
        scratch_shapes=[pltpu.VMEM((SUBLANE, LANE), F32)],
        compiler_params=_params(("arbitrary",)),
        name="moe_router",
    )(x2, gain.reshape(1, d).astype(F32), wr)

    e1, e2 = meta[:, 0].astype(I32), meta[:, 1].astype(I32)
    counts = cnt[0, :n_experts].astype(I32)
    ends = jnp.cumsum(counts)
    off = ends - counts
    dest1 = off[e1] + meta[:, 2].astype(I32)
    dest2 = off[e2] + meta[:, 3].astype(I32)
    na = TOP_K_EXPERTS * n
    tok = jnp.arange(n, dtype=I32)
    src = jnp.zeros((na,), I32).at[dest1].set(tok).at[dest2].set(tok)

    xs = _gather_rows(xn, src, rows=_pick(na, (256, 128, 64, 32, 16, 8)))

    tm = _pick(na, (512, 256, 128, 64, 32, 16, 8))
    n_tiles = na // tm
    nv = n_tiles + n_experts - 1
    lo = jnp.sort(jnp.concatenate([jnp.arange(n_tiles, dtype=I32) * tm, ends[:-1]]))
    hi = jnp.concatenate([lo[1:], jnp.full((1,), na, I32)])
    tile = jnp.minimum(lo // tm, n_tiles - 1)
    expert = jnp.minimum(jnp.sum((lo[:, None] >= ends[None, :]).astype(I32), axis=1), n_experts - 1)
    vid = jnp.arange(nv, dtype=I32)
    last = lax.cummax(jnp.where(hi > lo, vid, -1))
    wexp = expert[jnp.maximum(last, 0)]
    first = jnp.concatenate([jnp.ones((1,), I32), (tile[1:] != tile[:-1]).astype(I32)])

    fc = _pick(dff, (512, 256, 128))
    nf = dff // fc

    def up_map(v, f, tile_ref, wexp_ref, lo_ref, hi_ref, first_ref):
        return wexp_ref[v], 0, jnp.where(hi_ref[v] > lo_ref[v], f, nf - 1)

    def down_map(v, f, tile_ref, wexp_ref, lo_ref, hi_ref, first_ref):
        return wexp_ref[v], jnp.where(hi_ref[v] > lo_ref[v], f, nf - 1), 0

    def row_map(v, f, tile_ref, wexp_ref, lo_ref, hi_ref, first_ref):
        return tile_ref[v], 0

    ys = pl.pallas_call(
        functools.partial(_experts_body, tm=tm),
        grid_spec=pltpu.PrefetchScalarGridSpec(
            num_scalar_prefetch=5,
            grid=(nv, nf),
            in_specs=[pl.BlockSpec((tm, d), row_map), pl.BlockSpec((None, d, fc), up_map),
                      pl.BlockSpec((None, d, fc), up_map), pl.BlockSpec((None, fc, d), down_map)],
            out_specs=pl.BlockSpec((tm, d), row_map, pipeline_mode=pl.Buffered(1)),
        ),
        out_shape=jax.ShapeDtypeStruct((na, d), F32),
        compiler_params=_params(("arbitrary", "arbitrary")),
        name="moe_experts",
    )(tile, wexp, lo, hi, first, xs, _expert_weights_bf16(w1, idx), _expert_weights_bf16(w3, idx),
      _expert_weights_bf16(w2, idx))

    rows = _pick(n, (256, 128, 64, 32, 16, 8))
    slab = (rows, d // LANE, LANE)
    slab_spec = pl.BlockSpec(slab, lambda i, d1, d2: (i, 0, 0))
    weight_spec = pl.BlockSpec((rows, 1, LANE), lambda i, d1, d2: (i, 0, 0))
    wt1 = jnp.broadcast_to(meta[:, 4][:, None, None], (n, 1, LANE))
    wt2 = jnp.broadcast_to(meta[:, 5][:, None, None], (n, 1, LANE))
    out = pl.pallas_call(
        functools.partial(_combine_body, rows=rows),
        grid_spec=pltpu.PrefetchScalarGridSpec(
            num_scalar_prefetch=2,
            grid=(n // rows,),
            in_specs=[pl.BlockSpec(memory_space=pl.ANY), slab_spec, weight_spec, weight_spec],
            out_specs=slab_spec,
            scratch_shapes=[pltpu.VMEM(slab, F32), pltpu.VMEM(slab, F32), pltpu.SemaphoreType.DMA((2,))],
        ),
        out_shape=jax.ShapeDtypeStruct((n, d // LANE, LANE), F32),
        compiler_params=_params(("arbitrary",)),
        name="moe_combine",
    )(dest1, dest2, ys.reshape(na, d // LANE, LANE), x2.reshape(n, d // LANE, LANE), wt1, wt2)
    return out.reshape(n, d)


def kernel(x, positions, mix_norm, w_in, mla_q_lat_norm, mla_kv_lat_norm, mla_w_uq, mla_w_ukv, mla_q_norm, mla_k_norm, dsa_q_norm, dsa_k_norm, ssm_lam_re, ssm_lam_im, ssm_log_dt, ssm_b_re, ssm_b_im, ssm_c_re, ssm_c_im, ssm_d, ssm_w_glu, w_branch, w_out, ffn_norm, dense_w_gate, dense_w_up, dense_w_down, moe_w_router, moe_w1, moe_w3, moe_w2):
    bsz, seq, d = x.shape
    depth = mix_norm.shape[0]
    x2 = x.reshape(bsz * seq, d)
    cos, sin = _rope_tables(positions)
    for layer in range(depth):
        x2 = _hybrid_mixer(x2, cos, sin, bsz, seq, mix_norm[layer], w_in[layer], mla_q_lat_norm[layer],
                           mla_kv_lat_norm[layer], mla_w_uq[layer], mla_w_ukv[layer], mla_q_norm[layer],
                           mla_k_norm[layer], dsa_q_norm[layer], dsa_k_norm[layer], ssm_lam_re[layer],
                           ssm_lam_im[layer], ssm_log_dt[layer], ssm_b_re[layer], ssm_b_im[layer], ssm_c_re[layer],
                           ssm_c_im[layer], ssm_d[layer], ssm_w_glu[layer], w_branch[layer], w_out, layer)
        i = layer // 2
        if layer % 2 == 0:
            x2 = _dense_ffn(x2, ffn_norm[layer], dense_w_gate, dense_w_up, dense_w_down, i)
        else:
            x2 = _moe_ffn(x2, ffn_norm[layer], moe_w_router, moe_w1, moe_w3, moe_w2, i)
    return x2.reshape(bsz, seq, d)
```

```python
import functools
import math

import jax
import jax.numpy as jnp
from jax import lax
from jax.experimental import pallas as pl
from jax.experimental.pallas import tpu as pltpu

F32 = jnp.float32
BF16 = jnp.bfloat16
I32 = jnp.int32

SSM_GROUP = 16
SSM_STATE = 64
MLA_HEADS = 12
MLA_NOPE = 128
MLA_ROPE = 64
MLA_V = 128
DSA_HEADS = 12
DSA_HEAD_DIM = 128
IDX_HEADS = 16
IDX_DIM = 64
DSA_TOPK_MAX = 256
TOP_K_EXPERTS = 2
ROPE_THETA = 10000.0
EPS = 1e-6

LANE = 128
SUBLANE = 8
VMEM_LIMIT_BYTES = 56 * 1024 * 1024
NEG_BIG = -1e30
INT_MIN = -(2 ** 31)
LOG2E = 1.4426950408889634


def _params(semantics):
    return pltpu.CompilerParams(dimension_semantics=semantics, vmem_limit_bytes=VMEM_LIMIT_BYTES)


def _sigmoid(x):
    return 1.0 / (1.0 + jnp.exp(-x))


def _pick(n, candidates):
    for c in candidates:
        if n % c == 0:
            return c
    return n


def _rmsnorm_body(x_ref, g_ref, o_ref):
    x = x_ref[...].astype(F32)
    ms = jnp.mean(x * x, axis=-1, keepdims=True)
    o_ref[...] = (x * lax.rsqrt(ms + EPS) * g_ref[...]).astype(o_ref.dtype)


def _rmsnorm(x, gain):
    m, width = x.shape
    tm = _pick(m, (512, 256, 128, 64, 32, 16, 8))
    return pl.pallas_call(
        _rmsnorm_body,
        grid=(m // tm,),
        in_specs=[pl.BlockSpec((tm, width), lambda i: (i, 0)), pl.BlockSpec((1, width), lambda i: (0, 0))],
        out_specs=pl.BlockSpec((tm, width), lambda i: (i, 0)),
        out_shape=jax.ShapeDtypeStruct((m, width), BF16),
        compiler_params=_params(("parallel",)),
        name="rmsnorm",
    )(x, gain.reshape(1, width).astype(F32))


def _mm_body(*refs, n_a, pairs, n_extra, epilogue, normed):
    a_refs = refs[:n_a]
    w_refs = refs[n_a:n_a + len(pairs)]
    e_refs = refs[n_a + len(pairs):n_a + len(pairs) + n_extra]
    if normed:
        g_ref, o_ref, xn_ref = refs[n_a + len(pairs) + n_extra:]
    else:
        o_ref = refs[n_a + len(pairs) + n_extra]
    if normed:

        @pl.when(pl.program_id(1) == 0)
        def _():
            rows = a_refs[0].shape[0]
            step = math.gcd(rows, 128)

            def chunk(c, carry):
                r0 = pl.multiple_of(c * step, step)
                x = a_refs[0][pl.ds(r0, step), :].astype(F32)
                ms = jnp.mean(x * x, axis=-1, keepdims=True)
                xn_ref[pl.ds(r0, step), :] = (x * lax.rsqrt(ms + EPS) * g_ref[...]).astype(BF16)
                return carry

            lax.fori_loop(0, rows // step, chunk, 0)

        a_vals = [xn_ref[...]]
    else:
        a_vals = [a[...].astype(BF16) for a in a_refs]
    parts = [jnp.dot(a_vals[ai], w[...].astype(BF16), preferred_element_type=F32)
             for ai, w in zip(pairs, w_refs)]
    o_ref[...] = epilogue(parts, *[e[...] for e in e_refs]).astype(o_ref.dtype)


def _matmul(a_list, w_list, *, epilogue, out_dtype, tn, tm=None, k_slice=None, extras=(), norm_gain=None,
            name="matmul"):
    m = a_list[0].shape[0]
    n = w_list[0][1].shape[-1]
    tm = tm or _pick(m, (1024, 512, 256, 128, 64, 32, 16, 8))
    ksize, kidx = k_slice if k_slice else (None, 0)
    in_specs = []
    a_mode = {"pipeline_mode": pl.Buffered(1)} if norm_gain is not None else {}
    for a in a_list:
        in_specs.append(pl.BlockSpec((tm, ksize or a.shape[1]), lambda i, j: (i, kidx), **a_mode))
    for _, w, prefix in w_list:
        in_specs.append(pl.BlockSpec((None,) * len(prefix) + (ksize or w.shape[-2], tn),
                                     lambda i, j, prefix=prefix: tuple(prefix) + (kidx, j)))
    for arr, kind, arg in extras:
        if kind == "mn":
            in_specs.append(pl.BlockSpec((tm, tn), lambda i, j, arg=arg: (i, j + arg)))
        else:
            in_specs.append(pl.BlockSpec((1, tn), lambda i, j: (0, j)))
    pairs = tuple(ai for ai, _, _ in w_list)
    operands = [*a_list, *[w for _, w, _ in w_list], *[arr for arr, _, _ in extras]]
    scratch = []
    if norm_gain is not None:
        kdim = a_list[0].shape[1]
        in_specs.append(pl.BlockSpec((1, kdim), lambda i, j: (0, 0)))
        operands.append(norm_gain.reshape(1, kdim).astype(F32))
        scratch.append(pltpu.VMEM((tm, kdim), BF16))
    body = functools.partial(_mm_body, n_a=len(a_list), pairs=pairs, n_extra=len(extras), epilogue=epilogue,
                             normed=norm_gain is not None)
    return pl.pallas_call(
        body,
        grid=(m // tm, n // tn),
        in_specs=in_specs,
        out_specs=pl.BlockSpec((tm, tn), lambda i, j: (i, j)),
        out_shape=jax.ShapeDtypeStruct((m, n), out_dtype),
        scratch_shapes=scratch,
        compiler_params=_params(("parallel", "arbitrary")),
        name=name,
    )(*operands)


def _ep_plain(ps):
    return ps[0]


def _ep_residual(ps, res):
    return res + ps[0]


def _ep_swiglu(ps):
    g = ps[0]
    return g * _sigmoid(g) * ps[1]


def _ep_glu(ps):
    return ps[0] * _sigmoid(ps[1])


def _ep_branches(ps, g0, g1, g2):
    return _sigmoid(g0) * ps[0] + _sigmoid(g1) * ps[1] + _sigmoid(g2) * ps[2]


def _s5_body(u_ref, wre_ref, wim_ref, a_ref, cre_ref, cim_ref, d_ref, y_ref,
             bre_ref, bim_ref, xr_ref, xi_ref, car_ref, *, steps):
    t = pl.program_id(2)
    lw = bre_ref.shape[-1]

    @pl.when(t == 0)
    def _():
        car_ref[...] = jnp.zeros_like(car_ref)

    u = u_ref[0]
    ub = u.astype(BF16)
    bre_ref[...] = jnp.dot(ub, wre_ref[...], preferred_element_type=F32)
    bim_ref[...] = jnp.dot(ub, wim_ref[...], preferred_element_type=F32)

    ar = jnp.broadcast_to(a_ref[0:1, :], (SUBLANE, lw))
    ai = jnp.broadcast_to(a_ref[1:2, :], (SUBLANE, lw))

    def pass1(j, carry):
        xr, xi = carry
        off = pl.multiple_of(j * SUBLANE, SUBLANE)
        nr = ar * xr - ai * xi + bre_ref[pl.ds(off, SUBLANE), :]
        ni = ar * xi + ai * xr + bim_ref[pl.ds(off, SUBLANE), :]
        xr_ref[pl.ds(off, SUBLANE), :] = nr
        xi_ref[pl.ds(off, SUBLANE), :] = ni
        return nr, ni

    zero = jnp.zeros((SUBLANE, lw), F32)
    er, ei = lax.fori_loop(0, steps, pass1, (zero, zero))

    pr = a_ref[2:3, :]
    pi = a_ref[3:4, :]
    cr = car_ref[0:1, :]
    ci = car_ref[1:2, :]
    rows_r, rows_i = [], []
    for s in range(SUBLANE):
        rows_r.append(cr)
        rows_i.append(ci)
        nr = pr * cr - pi * ci + er[s:s + 1, :]
        ni = pr * ci + pi * cr + ei[s:s + 1, :]
        cr, ci = nr, ni
    car_ref[0:1, :] = cr
    car_ref[1:2, :] = ci
    cin_r = jnp.concatenate(rows_r, axis=0)
    cin_i = jnp.concatenate(rows_i, axis=0)

    def pass2(j, carry):
        cr_, ci_ = carry
        off = pl.multiple_of(j * SUBLANE, SUBLANE)
        nr = ar * cr_ - ai * ci_
        ni = ar * ci_ + ai * cr_
        xr_ref[pl.ds(off, SUBLANE), :] = xr_ref[pl.ds(off, SUBLANE), :] + nr
        xi_ref[pl.ds(off, SUBLANE), :] = xi_ref[pl.ds(off, SUBLANE), :] + ni
        return nr, ni

    lax.fori_loop(0, steps, pass2, (cin_r, cin_i))

    y = (jnp.dot(xr_ref[...].astype(BF16), cre_ref[...], preferred_element_type=F32)
         + jnp.dot(xi_ref[...].astype(BF16), cim_ref[...], preferred_element_type=F32) + d_ref[...] * u)
    y_ref[0] = jax.nn.gelu(y).astype(y_ref.dtype)


def _s5_core(u, bin_re, bin_im, a_tab, bout_re, bout_im, d_skip, *, chunk):
    b, s, w = u.shape
    ns, _, sw = bin_re.shape
    steps = chunk // SUBLANE
    lanes = pl.BlockSpec((1, chunk, LANE), lambda bi, li, ti: (bi, ti, li))
    slab_in = pl.BlockSpec((None, LANE, sw), lambda bi, li, ti: (li, 0, 0))
    slab_out = pl.BlockSpec((None, sw, LANE), lambda bi, li, ti: (li, 0, 0))
    return pl.pallas_call(
        functools.partial(_s5_body, steps=steps),
        grid=(b, ns, s // chunk),
        in_specs=[lanes, slab_in, slab_in, pl.BlockSpec((4, sw), lambda bi, li, ti: (0, li)), slab_out, slab_out,
                  pl.BlockSpec((1, LANE), lambda bi, li, ti: (0, li))],
        out_specs=lanes,
        out_shape=jax.ShapeDtypeStruct((b, s, w), BF16),
        scratch_shapes=[pltpu.VMEM((chunk, sw), F32)] * 4 + [pltpu.VMEM((SUBLANE, sw), F32)],
        compiler_params=_params(("parallel", "parallel", "arbitrary")),
        name="s5_core",
    )(u, bin_re, bin_im, a_tab, bout_re, bout_im, d_skip)


def _s5_chunk(s):
    return _pick(s, (512, 256, 128, 64))


def _to_chunk_order(t, chunk):
    b, s, w = t.shape
    return t.reshape(b, s // chunk, SUBLANE, chunk // SUBLANE, w).transpose(0, 1, 3, 2, 4).reshape(b, s, w)


def _from_chunk_order(t, chunk):
    b, s, w = t.shape
    return t.reshape(b, s // chunk, chunk // SUBLANE, SUBLANE, w).transpose(0, 1, 3, 2, 4).reshape(b, s, w)


def _s5_discretise(lam_re, lam_im, log_dt, b_re, b_im, c_re, c_im, steps):
    g, p = lam_re.shape
    h = b_re.shape[-1]
    lr, li = lam_re.astype(F32), lam_im.astype(F32)
    dt = jnp.exp(log_dt.astype(F32))[:, None]
    mag = jnp.exp(lr * dt)
    ab_re, ab_im = mag * jnp.cos(li * dt), mag * jnp.sin(li * dt)
    den = lr * lr + li * li
    nr = ab_re - 1.0
    zr = (nr * lr + ab_im * li) / den
    zi = (ab_im * lr - nr * li) / den
    bb_re = zr[..., None] * b_re - zi[..., None] * b_im
    bb_im = zr[..., None] * b_im + zi[..., None] * b_re
    gs = LANE // h
    ns = g // gs
    eye = jnp.eye(gs, dtype=F32)
    slab_in = lambda t: jnp.einsum("sgph,gk->sghkp", t.reshape(ns, gs, p, h), eye).reshape(ns, gs * h, gs * p)
    slab_out = lambda t: jnp.einsum("sghp,gk->sgpkh", t.reshape(ns, gs, h, p), eye).reshape(ns, gs * p, gs * h)
    pr, pi = ab_re, ab_im
    for _ in range(int(round(math.log2(steps)))):
        pr, pi = pr * pr - pi * pi, 2.0 * pr * pi
    a_tab = jnp.stack([ab_re.reshape(-1), ab_im.reshape(-1), pr.reshape(-1), pi.reshape(-1)])
    return (slab_in(bb_re).astype(BF16), slab_in(bb_im).astype(BF16), slab_out(c_re.astype(F32)).astype(BF16),
            slab_out(-c_im.astype(F32)).astype(BF16), a_tab)


def _s5_mixer(u, bsz, seq, lam_re, lam_im, log_dt, b_re, b_im, c_re, c_im, d_skip, w_glu):
    n, w = u.shape
    chunk = _s5_chunk(seq)
    steps = chunk // SUBLANE
    bin_re, bin_im, bout_re, bout_im, a_tab = _s5_discretise(lam_re, lam_im, log_dt, b_re, b_im, c_re, c_im, steps)
    u_p = _to_chunk_order(u.reshape(bsz, seq, w), chunk)
    y = _s5_core(u_p, bin_re, bin_im, a_tab, bout_re, bout_im, d_skip.reshape(1, w).astype(F32), chunk=chunk)
    y = y.reshape(n, w)
    tnw = _pick(w, (512, 256, 128))
    w_a = w_glu[:, :w].astype(BF16)
    w_g = w_glu[:, w:].astype(BF16)
    out = _matmul([y], [(0, w_a, ()), (0, w_g, ())], epilogue=_ep_glu, out_dtype=BF16, tn=tnw, name="s5_glu")
    return _from_chunk_order(out.reshape(bsz, seq, w), chunk).reshape(n, w)


def _rope_table_body(pos_ref, inv_ref, cos_ref, sin_ref):
    ang = pos_ref[...].astype(F32) * inv_ref[...]
    cos_ref[...] = jnp.cos(ang)
    sin_ref[...] = jnp.sin(ang)


def _rope_tables(positions):
    n = positions.size
    half = MLA_ROPE // 2
    inv = ROPE_THETA ** (-jnp.arange(half, dtype=F32) / half)
    inv = jnp.zeros((1, LANE), F32).at[0, :half].set(inv)
    tm = _pick(n, (512, 256, 128, 64, 32, 16, 8))
    return pl.pallas_call(
        _rope_table_body,
        grid=(n // tm,),
        in_specs=[pl.BlockSpec((tm, 1), lambda i: (i, 0)), pl.BlockSpec((1, LANE), lambda i: (0, 0))],
        out_specs=[pl.BlockSpec((tm, LANE), lambda i: (i, 0))] * 2,
        out_shape=[jax.ShapeDtypeStruct((n, LANE), F32)] * 2,
        compiler_params=_params(("parallel",)),
        name="rope_table",
    )(positions.reshape(n, 1).astype(I32), inv)


def _head_norm_rope(t0, t1, t2, g0, g1, g2, cos, sin, true_dim):
    ss = (jnp.sum(t0 * t0, axis=-1, keepdims=True) + jnp.sum(t1 * t1, axis=-1, keepdims=True)
          + jnp.sum(t2 * t2, axis=-1, keepdims=True))
    inv = lax.rsqrt(ss * (1.0 / true_dim) + EPS)
    n0 = t0 * inv * g0
    n1 = t1 * inv * g1
    n2 = t2 * inv * g2
    o1 = n1 * cos - n2 * sin
    o2 = n1 * sin + n2 * cos
    half = MLA_ROPE // 2
    lane = lax.broadcasted_iota(I32, o1.shape, 1)
    o1 = jnp.where(lane < half, o1, 0.0)
    o2 = jnp.where(lane < half, o2, 0.0)
    return jnp.concatenate([n0, o1 + pltpu.roll(o2, half, 1)], axis=-1)


def _mla_qkv_body(cq_ref, ckv_ref, pe_ref, gq_lat_ref, gkv_lat_ref, wq_ref, wkv_ref, qg_ref, kg_ref, cos_ref, sin_ref,
                  q_ref, k_ref, v_ref):
    def lat_norm(ref, g_ref):
        t = ref[...]
        ms = jnp.mean(t * t, axis=-1, keepdims=True)
        return (t * lax.rsqrt(ms + EPS) * g_ref[...]).astype(BF16)

    cqn = lat_norm(cq_ref, gq_lat_ref)
    ckvn = lat_norm(ckv_ref, gkv_lat_ref)
    cos, sin = cos_ref[...], sin_ref[...]
    hw = 3 * LANE
    qg = (qg_ref[:, 0:LANE], qg_ref[:, LANE:2 * LANE], qg_ref[:, 2 * LANE:hw])
    kg = (kg_ref[:, 0:LANE], kg_ref[:, LANE:2 * LANE], kg_ref[:, 2 * LANE:hw])
    pe1 = pe_ref[:, 0:LANE]
    pe2 = pe_ref[:, LANE:2 * LANE]
    kvw = MLA_NOPE + MLA_V
    qk = MLA_NOPE + MLA_ROPE
    for h in range(MLA_HEADS):
        qh = jnp.dot(cqn, wq_ref[:, h * hw:(h + 1) * hw], preferred_element_type=F32)
        qh = _head_norm_rope(qh[:, 0:LANE], qh[:, LANE:2 * LANE], qh[:, 2 * LANE:hw], *qg, cos, sin, qk)
        q_ref[0, h] = (qh * (qk ** -0.5 * LOG2E)).astype(q_ref.dtype)
        kvh = jnp.dot(ckvn, wkv_ref[:, h * kvw:(h + 1) * kvw], preferred_element_type=F32)
        k_ref[0, h] = _head_norm_rope(kvh[:, 0:MLA_NOPE], pe1, pe2, *kg, cos, sin, qk).astype(k_ref.dtype)
        v_ref[0, h] = kvh[:, MLA_NOPE:kvw].astype(v_ref.dtype)


def _pad_rope_gain(gain):
    half = MLA_ROPE // 2
    out = jnp.zeros((3 * LANE,), F32)
    out = out.at[:MLA_NOPE].set(gain[:MLA_NOPE].astype(F32))
    out = out.at[LANE:LANE + half].set(gain[MLA_NOPE:MLA_NOPE + half].astype(F32))
    out = out.at[2 * LANE:2 * LANE + half].set(gain[MLA_NOPE + half:].astype(F32))
    return out.reshape(1, 3 * LANE)


def _flash_body(q_ref, k_ref, v_ref, o_ref, *, tq, wide):
    qi = pl.program_id(2)
    dv = v_ref.shape[-1]
    q = q_ref[0, 0]

    def step(off, width, carry, masked):
        m, l, acc = carry
        k = k_ref[0, 0, pl.ds(off, width), :]
        v = v_ref[0, 0, pl.ds(off, width), :]
        s = lax.dot_general(q, k, (((1,), (1,)), ((), ())), preferred_element_type=F32)
        if masked:
            row = lax.broadcasted_iota(I32, (tq, width), 0)
            col = lax.broadcasted_iota(I32, (tq, width), 1)
            s = jnp.where(col <= row, s, NEG_BIG)
        m_new = jnp.maximum(m, jnp.max(s, axis=1, keepdims=True))
        alpha = jnp.exp2(m - m_new)
        p = jnp.exp2(s - m_new)
        l = alpha * l + jnp.sum(p, axis=1, keepdims=True)
        acc = alpha * acc + jnp.dot(p.astype(BF16), v, preferred_element_type=F32)
        return m_new, l, acc

    per = wide // tq
    n_wide = qi // per
    carry = (jnp.full((tq, 1), NEG_BIG, F32), jnp.zeros((tq, 1), F32), jnp.zeros((tq, dv), F32))
    carry = lax.fori_loop(0, n_wide, lambda j, c: step(pl.multiple_of(j * wide, wide), wide, c, False), carry)
    carry = lax.fori_loop(n_wide * per, qi, lambda j, c: step(pl.multiple_of(j * tq, tq), tq, c, False), carry)
    m, l, acc = step(pl.multiple_of(qi * tq, tq), tq, carry, True)
    o_ref[0] = (acc / l).astype(o_ref.dtype)


def _flash_attention(q, k, v):
    b, h, s, dk = q.shape
    dv = v.shape[-1]
    tq = _pick(s, (1024, 512, 256, 128))
    wide = _pick(s, (2 * tq, tq))
    return pl.pallas_call(
        functools.partial(_flash_body, tq=tq, wide=wide),
        grid=(b, h, s // tq),
        in_specs=[pl.BlockSpec((1, 1, tq, dk), lambda bi, hi, qi: (bi, hi, qi, 0)),
                  pl.BlockSpec((1, 1, s, dk), lambda bi, hi, qi: (bi, hi, 0, 0)),
                  pl.BlockSpec((1, 1, s, dv), lambda bi, hi, qi: (bi, hi, 0, 0))],
        out_specs=pl.BlockSpec((1, tq, dv), lambda bi, hi, qi: (bi, qi, hi)),
        out_shape=jax.ShapeDtypeStruct((b, s, h * dv), BF16),
        compiler_params=_params(("parallel", "parallel", "arbitrary")),
        name="mla_flash",
    )(q, k, v)


def _mla_mixer(hm, col, cos, sin, bsz, seq, q_lora, kv_lora, q_lat_gain, kv_lat_gain, w_uq, w_ukv, q_gain, k_gain):
    n = hm.shape[0]
    half = MLA_ROPE // 2
    qk = MLA_NOPE + MLA_ROPE
    wq = w_uq.reshape(q_lora, MLA_HEADS, qk)
    wq_p = jnp.zeros((q_lora, MLA_HEADS, 3 * LANE), BF16)
    wq_p = wq_p.at[:, :, :MLA_NOPE].set(wq[:, :, :MLA_NOPE].astype(BF16))
    wq_p = wq_p.at[:, :, LANE:LANE + half].set(wq[:, :, MLA_NOPE:MLA_NOPE + half].astype(BF16))
    wq_p = wq_p.at[:, :, 2 * LANE:2 * LANE + half].set(wq[:, :, MLA_NOPE + half:].astype(BF16))
    wq_p = wq_p.reshape(q_lora, MLA_HEADS * 3 * LANE)
    wkv = w_ukv.astype(BF16)
    tm = _pick(seq, (256, 128, 64, 32, 16))
    nblk = seq // tm
    cols = lambda width, block: pl.BlockSpec((tm, width), lambda bi, si: (bi * nblk + si, block))
    const = lambda arr: pl.BlockSpec(arr.shape, lambda bi, si: (0, 0), pipeline_mode=pl.Buffered(1))
    heads = lambda width: pl.BlockSpec((1, MLA_HEADS, tm, width), lambda bi, si: (bi, 0, si, 0))
    gq_lat = q_lat_gain.reshape(1, q_lora).astype(F32)
    gkv_lat = kv_lat_gain.reshape(1, kv_lora).astype(F32)
    qg = _pad_rope_gain(q_gain)
    kg = _pad_rope_gain(k_gain)
    q, k, v = pl.pallas_call(
        _mla_qkv_body,
        grid=(bsz, nblk),
        in_specs=[cols(q_lora, col["c_q"] // q_lora), cols(kv_lora, col["c_kv"] // kv_lora),
                  cols(2 * LANE, col["pe"] // (2 * LANE)),
                  const(gq_lat), const(gkv_lat), const(wq_p), const(wkv), const(qg), const(kg),
                  cols(LANE, 0), cols(LANE, 0)],
        out_specs=[heads(2 * LANE), heads(2 * LANE), heads(MLA_V)],
        out_shape=[jax.ShapeDtypeStruct((bsz, MLA_HEADS, seq, 2 * LANE), BF16),
                   jax.ShapeDtypeStruct((bsz, MLA_HEADS, seq, 2 * LANE), BF16),
                   jax.ShapeDtypeStruct((bsz, MLA_HEADS, seq, MLA_V), BF16)],
        compiler_params=_params(("parallel", "parallel")),
        name="mla_qkv",
    )(hm, hm, hm, gq_lat, gkv_lat, wq_p, wkv, qg, kg, cos, sin)
    return _flash_attention(q, k, v).reshape(n, MLA_HEADS * MLA_V)


def _dsa_prep_body(hd_q_ref, hd_k_ref, hd_v_ref, hi_q_ref, hi_k_ref, hi_w_ref, qg_ref, kg_ref,
                   q_ref, k_ref, v_ref, qi_ref, klo_ref, khi_ref, w_ref):
    def norm(t, g):
        ms = jnp.mean(t * t, axis=-1, keepdims=True)
        return t * lax.rsqrt(ms + EPS) * g

    qg = qg_ref[...]
    for h in range(DSA_HEADS):
        qh = norm(hd_q_ref[:, h * DSA_HEAD_DIM:(h + 1) * DSA_HEAD_DIM], qg)
        q_ref[0, h] = (qh * (DSA_HEAD_DIM ** -0.5 * LOG2E)).astype(q_ref.dtype)
    k_ref[0] = norm(hd_k_ref[...], kg_ref[...]).astype(k_ref.dtype)
    v_ref[0] = hd_v_ref[...].astype(v_ref.dtype)
    qi_ref[0] = hi_q_ref[...].astype(qi_ref.dtype)
    ki = hi_k_ref[...]
    lane = lax.broadcasted_iota(I32, ki.shape, 1)
    ki = jnp.where(lane < IDX_DIM, ki, 0.0)
    klo_ref[0] = ki.astype(klo_ref.dtype)
    khi_ref[0] = pltpu.roll(ki, IDX_DIM, 1).astype(khi_ref.dtype)
    w_ref[0] = hi_w_ref[...] * ((IDX_HEADS ** -0.5) * (IDX_DIM ** -0.5))


def _float_key(x):
    bits = pltpu.bitcast(x, I32)
    return jnp.where(bits < 0, bits ^ jnp.int32(0x7FFFFFFF), bits)


def _dsa_body(q_ref, k_ref, v_ref, qi_ref, klo_ref, khi_ref, w_ref, o_ref, keys_ref, gmax_ref, acc_ref,
              *, tq, tk, wide, n_sel, seq):
    qb = pl.program_id(1)
    n_kv = (qb * tq + tq + tk - 1) // tk
    row = lax.broadcasted_iota(I32, (tq, tk), 0) + qb * tq
    col0 = lax.broadcasted_iota(I32, (tq, tk), 1)

    w = w_ref[0]

    def score_tile(j, _):
        off = pl.multiple_of(j * tk, tk)
        klo = klo_ref[0, pl.ds(off, tk), :]
        khi = khi_ref[0, pl.ds(off, tk), :]
        acc = jnp.zeros((tq, tk), F32)
        for hp in range(IDX_HEADS // 2):
            q2 = qi_ref[0, :, hp * LANE:(hp + 1) * LANE]
            for half, kk in ((0, klo), (1, khi)):
                h = 2 * hp + half
                logit = lax.dot_general(q2, kk, (((1,), (1,)), ((), ())), preferred_element_type=F32)
                acc = acc + w[:, h:h + 1] * jnp.maximum(logit, 0.0)
        causal = col0 + off <= row
        keys_ref[:, pl.ds(off, tk)] = jnp.where(causal, _float_key(acc + 0.0), INT_MIN)
        sc = jnp.where(causal, acc, -jnp.inf)
        for c in range(tk // LANE):
            gmax_ref[c % 2] = jnp.maximum(gmax_ref[c % 2], sc[:, c * LANE:(c + 1) * LANE])
        return 0

    gmax_ref[...] = jnp.full(gmax_ref.shape, -jnp.inf, F32)
    lax.fori_loop(0, n_kv, score_tile, 0)

    rc = min(tq, 128)

    def row_group(g, carry):
        r0 = pl.multiple_of(g * rc, rc)
        ge = gmax_ref[0, pl.ds(r0, rc), :]
        go = gmax_ref[1, pl.ds(r0, rc), :]
        key_l = _float_key(jnp.min(jnp.minimum(ge, go), axis=1, keepdims=True) + 0.0)
        key_u = _float_key(jnp.max(jnp.maximum(ge, go), axis=1, keepdims=True) + 0.0)
        span = key_u - key_l
        n_iter = jnp.max(jnp.where(span < 0, 32, 32 - lax.clz(span)))
        lo0 = jnp.broadcast_to(key_l, (rc, LANE))
        hi0 = jnp.broadcast_to(key_u + 1, (rc, LANE))

        lane = lax.broadcasted_iota(I32, (rc, LANE), 1)

        def count(pred):
            def count_tile(j, part):
                off = pl.multiple_of(j * tk, tk)
                for c in range(tk // LANE):
                    kc = keys_ref[pl.ds(r0, rc), pl.ds(off + c * LANE, LANE)]
                    part = part + jnp.where(pred(kc, lane + (off + c * LANE)), 1.0, 0.0)
                return part

            part = lax.fori_loop(0, n_kv, count_tile, jnp.zeros((rc, LANE), F32))
            return jnp.sum(part, axis=1, keepdims=True)

        def set_bias(selected):
            def bias_tile(j, c2):
                off = pl.multiple_of(j * tk, tk)
                for c in range(tk // LANE):
                    kc = keys_ref[pl.ds(r0, rc), pl.ds(off + c * LANE, LANE)]
                    bias = jnp.where(selected(kc, lane + (off + c * LANE)), 0.0, NEG_BIG)
                    keys_ref[pl.ds(r0, rc), pl.ds(off + c * LANE, LANE)] = pltpu.bitcast(bias, I32)
                return c2

            lax.fori_loop(0, n_kv, bias_tile, 0)

        def halve(b, state):
            lo, hi, n_lo = state
            cand = (lo >> 1) + (hi >> 1) + (lo & hi & 1)
            n_cand = count(lambda kc, col: kc >= cand)
            enough = n_cand >= n_sel
            return jnp.where(enough, cand, lo), jnp.where(enough, hi, cand), jnp.where(enough, n_cand, n_lo)

        unknown = jnp.full((rc, 1), float(seq + 1), F32)
        thr, _, n_thr = lax.fori_loop(0, n_iter, halve, (lo0, hi0, unknown))
        thr = jnp.maximum(thr, INT_MIN + 1)
        tied = jnp.max(n_thr) > n_sel

        @pl.when(jnp.logical_not(tied))
        def _():
            set_bias(lambda kc, col: kc >= thr)

        @pl.when(tied)
        def _():
            need = n_sel - count(lambda kc, col: kc > thr)

            def narrow(b, bounds):
                below, upto = bounds
                mid = (below + upto) >> 1
                ok = count(lambda kc, col: jnp.where(kc == thr, col, seq) <= mid) >= need
                return jnp.where(ok, below, mid), jnp.where(ok, mid, upto)

            first = (jnp.full((rc, LANE), -1, I32), jnp.full((rc, LANE), seq - 1, I32))
            _, last = lax.fori_loop(0, max(1, (seq - 1).bit_length()) + 1, narrow, first)
            set_bias(lambda kc, col: jnp.where(kc == thr, col, jnp.where(kc > thr, -1, seq)) <= last)

        return carry

    lax.fori_loop(0, tq // rc, row_group, 0)

    per = wide // tk
    n_wide = n_kv // per

    def head(h, carry):
        q = q_ref[0, h]

        def step(off, width, st):
            m, l, acc = st
            k = k_ref[0, pl.ds(off, width), :]
            v = v_ref[0, pl.ds(off, width), :]
            bias = pltpu.bitcast(keys_ref[:, pl.ds(off, width)], F32)
            s = lax.dot_general(q, k, (((1,), (1,)), ((), ())), preferred_element_type=F32) + bias
            m_new = jnp.maximum(m, jnp.max(s, axis=1, keepdims=True))
            alpha = jnp.exp2(m - m_new)
            p = jnp.exp2(s - m_new)
            l = alpha * l + jnp.sum(p, axis=1, keepdims=True)
            acc = alpha * acc + jnp.dot(p.astype(BF16), v, preferred_element_type=F32)
            return m_new, l, acc

        st = (jnp.full((tq, 1), NEG_BIG, F32), jnp.zeros((tq, 1), F32), jnp.zeros((tq, DSA_HEAD_DIM), F32))
        st = lax.fori_loop(0, n_wide, lambda j, c: step(pl.multiple_of(j * wide, wide), wide, c), st)
        st = lax.fori_loop(n_wide * per, n_kv, lambda j, c: step(pl.multiple_of(j * tk, tk), tk, c), st)
        m, l, acc = st
        acc_ref[h] = acc / l
        return carry

    lax.fori_loop(0, DSA_HEADS, head, 0)
    for h in range(DSA_HEADS):
        o_ref[0, :, h * DSA_HEAD_DIM:(h + 1) * DSA_HEAD_DIM] = acc_ref[h].astype(o_ref.dtype)


def _dsa_mixer(hd, col, bsz, seq, q_gain, k_gain):
    n = hd.shape[0]
    qw = DSA_HEADS * DSA_HEAD_DIM
    iw = IDX_HEADS * IDX_DIM
    tm = _pick(seq, (256, 128, 64, 32, 16))
    nblk = seq // tm

    def cols(width, block):
        return pl.BlockSpec((tm, width), lambda bi, si: (bi * nblk + si, block))

    per_tok = lambda width: pl.BlockSpec((1, tm, width), lambda bi, si: (bi, si, 0))
    q, k, v, qi, klo, khi, w = pl.pallas_call(
        _dsa_prep_body,
        grid=(bsz, nblk),
        in_specs=[cols(qw, col["q_c"] // qw), cols(LANE, col["k_c"] // LANE), cols(LANE, col["v_c"] // LANE),
                  cols(iw, col["q_i"] // iw), cols(LANE, col["k_i"] // LANE), cols(LANE, col["w_i"] // LANE),
                  pl.BlockSpec((1, DSA_HEAD_DIM), lambda bi, si: (0, 0)),
                  pl.BlockSpec((1, DSA_HEAD_DIM), lambda bi, si: (0, 0))],
        out_specs=[pl.BlockSpec((1, DSA_HEADS, tm, DSA_HEAD_DIM), lambda bi, si: (bi, 0, si, 0)),
                   per_tok(DSA_HEAD_DIM), per_tok(DSA_HEAD_DIM), per_tok(iw), per_tok(LANE), per_tok(LANE),
                   per_tok(LANE)],
        out_shape=[jax.ShapeDtypeStruct((bsz, DSA_HEADS, seq, DSA_HEAD_DIM), BF16),
                   jax.ShapeDtypeStruct((bsz, seq, DSA_HEAD_DIM), BF16),
                   jax.ShapeDtypeStruct((bsz, seq, DSA_HEAD_DIM), BF16),
                   jax.ShapeDtypeStruct((bsz, seq, iw), BF16),
                   jax.ShapeDtypeStruct((bsz, seq, LANE), BF16),
                   jax.ShapeDtypeStruct((bsz, seq, LANE), BF16),
                   jax.ShapeDtypeStruct((bsz, seq, LANE), F32)],
        compiler_params=_params(("parallel", "parallel")),
        name="dsa_prep",
    )(hd, hd, hd, hd, hd, hd, q_gain.reshape(1, -1).astype(F32), k_gain.reshape(1, -1).astype(F32))

    tq = _pick(seq, (512, 256, 128))
    tk = _pick(seq, (512, 256, 128))
    n_sel = min(DSA_TOPK_MAX, seq // 4)
    assert n_sel <= 2 * LANE
    whole = lambda width: pl.BlockSpec((1, seq, width), lambda bi, qb: (bi, 0, 0), pipeline_mode=pl.Buffered(1))
    out = pl.pallas_call(
        functools.partial(_dsa_body, tq=tq, tk=tk, wide=_pick(seq, (4 * tk, 2 * tk, tk)), n_sel=n_sel, seq=seq),
        grid=(bsz, seq // tq),
        in_specs=[pl.BlockSpec((1, DSA_HEADS, tq, DSA_HEAD_DIM), lambda bi, qb: (bi, 0, qb, 0)),
                  whole(DSA_HEAD_DIM), whole(DSA_HEAD_DIM),
                  pl.BlockSpec((1, tq, iw), lambda bi, qb: (bi, qb, 0)),
                  whole(LANE), whole(LANE),
                  pl.BlockSpec((1, tq, LANE), lambda bi, qb: (bi, qb, 0))],
        out_specs=pl.BlockSpec((1, tq, qw), lambda bi, qb: (bi, qb, 0)),
        out_shape=jax.ShapeDtypeStruct((bsz, seq, qw), BF16),
        scratch_shapes=[pltpu.VMEM((tq, seq), I32), pltpu.VMEM((2, tq, LANE), F32),
                        pltpu.VMEM((DSA_HEADS, tq, DSA_HEAD_DIM), F32)],
        compiler_params=_params(("parallel", "arbitrary")),
        name="dsa_attention",
    )(q, k, v, qi, klo, khi, w)
    return out.reshape(n, qw)


def _pad_cols(w, width):
    return jnp.pad(w, ((0, 0), (0, width - w.shape[1])))


def _hybrid_mixer(x2, cos, sin, bsz, seq, norm_gain, w_in, q_lat_gain, kv_lat_gain, w_uq, w_ukv, mla_q_gain,
                  mla_k_gain, dsa_q_gain, dsa_k_gain, lam_re, lam_im, log_dt, b_re, b_im, c_re, c_im, d_skip, w_glu,
                  w_branch, w_out, layer):
    n, d = x2.shape
    ssm_w = d_skip.shape[-1]
    q_lora = q_lat_gain.shape[-1]
    kv_lora = kv_lat_gain.shape[-1]
    half = MLA_ROPE // 2
    dsa_w = DSA_HEADS * DSA_HEAD_DIM
    idx_w = IDX_HEADS * IDX_DIM
    sizes = (ssm_w, q_lora, kv_lora, MLA_ROPE, dsa_w, DSA_HEAD_DIM, DSA_HEAD_DIM, idx_w, IDX_DIM, IDX_HEADS, 3 * d)
    offs = [0]
    for s in sizes:
        offs.append(offs[-1] + s)
    seg = lambda i: w_in[:, offs[i]:offs[i + 1]]
    pe = seg(3)
    tn = _pick(d, (512, 256, 128))
    pe_pad = jnp.concatenate([_pad_cols(pe[:, :half], LANE), _pad_cols(pe[:, half:], LANE)], axis=1)
    parts = [("q_c", seg(4), dsa_w), ("c_kv", seg(2), kv_lora), ("u", seg(0), LANE), ("c_q", seg(1), q_lora),
             ("q_i", seg(7), idx_w), ("pe", pe_pad, 2 * LANE), ("k_c", seg(5), LANE), ("v_c", seg(6), LANE),
             ("k_i", _pad_cols(seg(8), LANE), LANE), ("w_i", _pad_cols(seg(9), LANE), LANE)]
    col = {}
    at = 0
    for name, part, block_width in parts:
        assert at % block_width == 0, (name, at, block_width)
        col[name] = at
        at += part.shape[1]
    gap = -at % tn
    col["gates"] = at + gap
    w_cat = jnp.concatenate([part for _, part, _ in parts] + [jnp.zeros((d, gap), F32), seg(10)],
                            axis=1).astype(BF16)
    h = _matmul([x2], [(0, w_cat, ())], epilogue=_ep_plain, out_dtype=F32, norm_gain=norm_gain,
                tn=_pick(w_cat.shape[1], (768, 512, 256, 128)), name="in_proj")

    u = h[:, col["u"]:col["u"] + ssm_w]
    y_ssm = _s5_mixer(u, bsz, seq, lam_re, lam_im, log_dt, b_re, b_im, c_re, c_im, d_skip, w_glu)
    y_mla = _mla_mixer(h, col, cos, sin, bsz, seq, q_lora, kv_lora, q_lat_gain, kv_lat_gain, w_uq, w_ukv,
                       mla_q_gain, mla_k_gain)
    y_dsa = _dsa_mixer(h, col, bsz, seq, dsa_q_gain, dsa_k_gain)

    br0 = ssm_w
    br1 = ssm_w + MLA_HEADS * MLA_V
    g0 = col["gates"] // tn
    merged = _matmul([y_ssm, y_mla, y_dsa],
                     [(0, w_branch[:br0].astype(BF16), ()), (1, w_branch[br0:br1].astype(BF16), ()),
                      (2, w_branch[br1:].astype(BF16), ())],
                     epilogue=_ep_branches, out_dtype=BF16, tn=tn,
                     extras=[(h, "mn", g0), (h, "mn", g0 + d // tn), (h, "mn", g0 + 2 * (d // tn))],
                     name="branch_merge")
    return _matmul([merged], [(0, w_out, (layer,))], epilogue=_ep_residual, out_dtype=F32, tn=tn,
                   extras=[(x2, "mn", 0)], name="mixer_out")


def _dense_ffn(x2, gain, w_gate, w_up, w_down, idx):
    d = x2.shape[1]
    dff = w_gate.shape[-1]
    xn = _rmsnorm(x2, gain)
    tn = _pick(dff, (256, 128))
    h = _matmul([xn], [(0, w_gate, (idx,)), (0, w_up, (idx,))], epilogue=_ep_swiglu, out_dtype=BF16, tn=tn,
                name="ffn_up")
    tnd = _pick(d, (256, 128))
    n_slabs = 2 if dff % (2 * LANE) == 0 and dff > 8192 else 1
    out = x2
    for s in range(n_slabs):
        out = _matmul([h], [(0, w_down, (idx,))], epilogue=_ep_residual, out_dtype=F32, tn=tnd,
                      k_slice=(dff // n_slabs, s), extras=[(out, "mn", 0)], name="ffn_down")
    return out


def _router_body(x_ref, g_ref, w_ref, xn_ref, meta_ref, cnt_ref, run_ref, *, n_experts):
    i = pl.program_id(0)

    @pl.when(i == 0)
    def _():
        run_ref[...] = jnp.zeros_like(run_ref)

    x = x_ref[...]
    tm = x.shape[0]
    ms = jnp.mean(x * x, axis=-1, keepdims=True)
    xn = x * lax.rsqrt(ms + EPS) * g_ref[...]
    xn_ref[...] = xn.astype(xn_ref.dtype)
    logits = jnp.dot(xn, w_ref[...], preferred_element_type=F32, precision=lax.Precision.HIGHEST)
    lane = lax.broadcasted_iota(I32, logits.shape, 1).astype(F32)
    lg = jnp.where(lane < n_experts, logits, -jnp.inf)
    m1 = jnp.max(lg, axis=1, keepdims=True)
    i1 = jnp.min(jnp.where(lg == m1, lane, float(LANE)), axis=1, keepdims=True)
    lg2 = jnp.where(lane == i1, -jnp.inf, lg)
    m2 = jnp.max(lg2, axis=1, keepdims=True)
    i2 = jnp.min(jnp.where(lg2 == m2, lane, float(LANE)), axis=1, keepdims=True)
    e2 = jnp.exp(m2 - m1)
    den = 1.0 + e2
    oh1 = jnp.where(lane == i1, 1.0, 0.0)
    oh2 = jnp.where(lane == i2, 1.0, 0.0)
    oh = oh1 + oh2
    r = lax.broadcasted_iota(I32, (tm, tm), 0)
    c = lax.broadcasted_iota(I32, (tm, tm), 1)
    tri = jnp.where(c < r, 1.0, 0.0).astype(BF16)
    base = run_ref[0:1, :]
    rank = jnp.dot(tri, oh.astype(BF16), preferred_element_type=F32) + base
    r1 = jnp.sum(oh1 * rank, axis=1, keepdims=True)
    r2 = jnp.sum(oh2 * rank, axis=1, keepdims=True)
    total = base + jnp.sum(oh, axis=0, keepdims=True)
    run_ref[0:1, :] = total
    cnt_ref[...] = jnp.broadcast_to(total, cnt_ref.shape)
    meta = jnp.zeros_like(logits)
    for k, val in enumerate((i1, i2, r1, r2, 1.0 / den, e2 / den)):
        meta = jnp.where(lane == float(k), val, meta)
    meta_ref[...] = meta


def _cast_body(x_ref, o_ref):
    o_ref[...] = x_ref[...].astype(o_ref.dtype)


def _expert_weights_bf16(w, idx):
    _, e, r, c = w.shape
    tr = _pick(r, (1024, 512, 256, 128, 64, 32, 16))
    return pl.pallas_call(
        _cast_body,
        grid=(e, r // tr),
        in_specs=[pl.BlockSpec((None, 1, tr, c), lambda i, j: (idx, i, j, 0))],
        out_specs=pl.BlockSpec((1, tr, c), lambda i, j: (i, j, 0)),
        out_shape=jax.ShapeDtypeStruct((e, r, c), BF16),
        compiler_params=_params(("parallel", "parallel")),
        name="moe_weight_cast",
    )(w)


def _gather_rows_body(src_ref, x_hbm, o_ref, sem, *, rows):
    base = pl.program_id(0) * rows

    def issue(r, carry):
        pltpu.make_async_copy(x_hbm.at[src_ref[base + r]], o_ref.at[r], sem).start()
        return carry

    lax.fori_loop(0, rows, issue, 0)
    pltpu.make_async_copy(x_hbm.at[pl.ds(0, rows)], o_ref, sem).wait()


def _gather_rows(x, src, *, rows):
    n_out = src.shape[0]
    n, d = x.shape
    x3 = x.reshape(n, d // LANE, LANE)
    out = pl.pallas_call(
        functools.partial(_gather_rows_body, rows=rows),
        grid_spec=pltpu.PrefetchScalarGridSpec(
            num_scalar_prefetch=1,
            grid=(n_out // rows,),
            in_specs=[pl.BlockSpec(memory_space=pl.ANY)],
            out_specs=pl.BlockSpec((rows, d // LANE, LANE), lambda i, src_ref: (i, 0, 0)),
            scratch_shapes=[pltpu.SemaphoreType.DMA(())],
        ),
        out_shape=jax.ShapeDtypeStruct((n_out, d // LANE, LANE), x.dtype),
        compiler_params=_params(("arbitrary",)),
        name="moe_gather",
    )(src, x3)
    return out.reshape(n_out, d)


def _experts_body(tile_ref, wexp_ref, lo_ref, hi_ref, first_ref, xs_ref, w1_ref, w3_ref, w2_ref, o_ref, *, tm):
    v = pl.program_id(0)
    f = pl.program_id(1)

    @pl.when((f == 0) & (first_ref[v] == 1))
    def _():
        o_ref[...] = jnp.zeros_like(o_ref)

    lo = lo_ref[v]
    hi = hi_ref[v]

    @pl.when(hi > lo)
    def _():
        xs = xs_ref[...]
        g = jnp.dot(xs, w1_ref[...], preferred_element_type=F32)
        u = jnp.dot(xs, w3_ref[...], preferred_element_type=F32)
        row = tile_ref[v] * tm + lax.broadcasted_iota(I32, (tm, 1), 0)
        h = jnp.where((row >= lo) & (row < hi), g * _sigmoid(g) * u, 0.0)
        o_ref[...] += jnp.dot(h.astype(BF16), w2_ref[...], preferred_element_type=F32)


def _combine_body(d1_ref, d2_ref, y_hbm, x_ref, w1_ref, w2_ref, o_ref, b1_ref, b2_ref, sem, *, rows):
    base = pl.program_id(0) * rows

    def issue(r, carry):
        pltpu.make_async_copy(y_hbm.at[d1_ref[base + r]], b1_ref.at[r], sem.at[0]).start()
        pltpu.make_async_copy(y_hbm.at[d2_ref[base + r]], b2_ref.at[r], sem.at[1]).start()
        return carry

    lax.fori_loop(0, rows, issue, 0)
    pltpu.make_async_copy(y_hbm.at[pl.ds(0, rows)], b1_ref, sem.at[0]).wait()
    pltpu.make_async_copy(y_hbm.at[pl.ds(0, rows)], b2_ref, sem.at[1]).wait()
    o_ref[...] = x_ref[...] + w1_ref[...] * b1_ref[...] + w2_ref[...] * b2_ref[...]


def _moe_ffn(x2, gain, w_router, w1, w3, w2, idx):
    n, d = x2.shape
    n_experts = w_router.shape[-1]
    dff = w1.shape[-1]
    tr = _pick(n, (256, 128, 64, 32, 16, 8))
    wr = _pad_cols(w_router[idx].astype(F32), LANE)
    xn, meta, cnt = pl.pallas_call(
        functools.partial(_router_body, n_experts=n_experts),
        grid=(n // tr,),
        in_specs=[pl.BlockSpec((tr, d), lambda i: (i, 0)), pl.BlockSpec((1, d), lambda i: (0, 0)),
                  pl.BlockSpec((d, LANE), lambda i: (0, 0))],
        out_specs=[pl.BlockSpec((tr, d), lambda i: (i, 0)), pl.BlockSpec((tr, LANE), lambda i: (i, 0)),
                   pl.BlockSpec((SUBLANE, LANE), lambda i: (0, 0))],
        out_shape=[jax.ShapeDtypeStruct((n, d), BF16), jax.ShapeDtypeStruct((n, LANE), F32),
                   jax.ShapeDtypeStruct((SUBLANE, LANE), F32)],
        scratch_shapes=[pltpu.VMEM((SUBLANE, LANE), F32)],
        compiler_params=_params(("arbitrary",)),
        name="moe_router",
    )(x2, gain.reshape(1, d).astype(F32), wr)

    e1, e2 = meta[:, 0].astype(I32), meta[:, 1].astype(I32)
    counts = cnt[0, :n_experts].astype(I32)
    ends = jnp.cumsum(counts)
    off = ends - counts
    dest1 = off[e1] + meta[:, 2].astype(I32)
    dest2 = off[e2] + meta[:, 3].astype(I32)
    na = TOP_K_EXPERTS * n
    tok = jnp.arange(n, dtype=I32)
    src = jnp.zeros((na,), I32).at[dest1].set(tok).at[dest2].set(tok)

    xs = _gather_rows(xn, src, rows=_pick(na, (256, 128, 64, 32, 16, 8)))

    tm = _pick(na, (512, 256, 128, 64, 32, 16, 8))
    n_tiles = na // tm
    nv = n_tiles + n_experts - 1
    lo = jnp.sort(jnp.concatenate([jnp.arange(n_tiles, dtype=I32) * tm, ends[:-1]]))
    hi = jnp.concatenate([lo[1:], jnp.full((1,), na, I32)])
    tile = jnp.minimum(lo // tm, n_tiles - 1)
    expert = jnp.minimum(jnp.sum((lo[:, None] >= ends[None, :]).astype(I32), axis=1), n_experts - 1)
    vid = jnp.arange(nv, dtype=I32)
    last = lax.cummax(jnp.where(hi > lo, vid, -1))
    wexp = expert[jnp.maximum(last, 0)]
    first = jnp.concatenate([jnp.ones((1,), I32), (tile[1:] != tile[:-1]).astype(I32)])

    fc = _pick(dff, (512, 256, 128))
    nf = dff // fc

    def up_map(v, f, tile_ref, wexp_ref, lo_ref, hi_ref, first_ref):
        return wexp_ref[v], 0, jnp.where(hi_ref[v] > lo_ref[v], f, nf - 1)

    def down_map(v, f, tile_ref, wexp_ref, lo_ref, hi_ref, first_ref):
        return wexp_ref[v], jnp.where(hi_ref[v] > lo_ref[v], f, nf - 1), 0

    def row_map(v, f, tile_ref, wexp_ref, lo_ref, hi_ref, first_ref):
        return tile_ref[v], 0

    ys = pl.pallas_call(
        functools.partial(_experts_body, tm=tm),
        grid_spec=pltpu.PrefetchScalarGridSpec(
            num_scalar_prefetch=5,
            grid=(nv, nf),
            in_specs=[pl.BlockSpec((tm, d), row_map), pl.BlockSpec((None, d, fc), up_map),
                      pl.BlockSpec((None, d, fc), up_map), pl.BlockSpec((None, fc, d), down_map)],
            out_specs=pl.BlockSpec((tm, d), row_map, pipeline_mode=pl.Buffered(1)),
        ),
        out_shape=jax.ShapeDtypeStruct((na, d), F32),
        compiler_params=_params(("arbitrary", "arbitrary")),
        name="moe_experts",
    )(tile, wexp, lo, hi, first, xs, _expert_weights_bf16(w1, idx), _expert_weights_bf16(w3, idx),
      _expert_weights_bf16(w2, idx))

    rows = _pick(n, (256, 128, 64, 32, 16, 8))
    slab = (rows, d // LANE, LANE)
    slab_spec = pl.BlockSpec(slab, lambda i, d1, d2: (i, 0, 0))
    weight_spec = pl.BlockSpec((rows, 1, LANE), lambda i, d1, d2: (i, 0, 0))
    wt1 = jnp.broadcast_to(meta[:, 4][:, None, None], (n, 1, LANE))
    wt2 = jnp.broadcast_to(meta[:, 5][:, None, None], (n, 1, LANE))
    out = pl.pallas_call(
        functools.partial(_combine_body, rows=rows),
        grid_spec=pltpu.PrefetchScalarGridSpec(
            num_scalar_prefetch=2,
            grid=(n // rows,),
            in_specs=[pl.BlockSpec(memory_space=pl.ANY), slab_spec, weight_spec, weight_spec],
            out_specs=slab_spec,
            scratch_shapes=[pltpu.VMEM(slab, F32), pltpu.VMEM(slab, F32), pltpu.SemaphoreType.DMA((2,))],
        ),
        out_shape=jax.ShapeDtypeStruct((n, d // LANE, LANE), F32),
        compiler_params=_params(("arbitrary",)),
        name="moe_combine",
    )(dest1, dest2, ys.reshape(na, d // LANE, LANE), x2.reshape(n, d // LANE, LANE), wt1, wt2)
    return out.reshape(n, d)


def kernel(x, positions, mix_norm, w_in, mla_q_lat_norm, mla_kv_lat_norm, mla_w_uq, mla_w_ukv, mla_q_norm, mla_k_norm, dsa_q_norm, dsa_k_norm, ssm_lam_re, ssm_lam_im, ssm_log_dt, ssm_b_re, ssm_b_im, ssm_c_re, ssm_c_im, ssm_d, ssm_w_glu, w_branch, w_out, ffn_norm, dense_w_gate, dense_w_up, dense_w_down, moe_w_router, moe_w1, moe_w3, moe_w2):
    bsz, seq, d = x.shape
    depth = mix_norm.shape[0]
    x2 = x.reshape(bsz * seq, d)
    cos, sin = _rope_tables(positions)
    for layer in range(depth):
        x2 = _hybrid_mixer(x2, cos, sin, bsz, seq, mix_norm[layer], w_in[layer], mla_q_lat_norm[layer],
                           mla_kv_lat_norm[layer], mla_w_uq[layer], mla_w_ukv[layer], mla_q_norm[layer],
                           mla_k_norm[layer], dsa_q_norm[layer], dsa_k_norm[layer], ssm_lam_re[layer],
                           ssm_lam_im[layer], ssm_log_dt[layer], ssm_b_re[layer], ssm_b_im[layer], ssm_c_re[layer],
                           ssm_c_im[layer], ssm_d[layer], ssm_w_glu[layer], w_branch[layer], w_out, layer)
        i = layer // 2
        if layer % 2 == 0:
            x2 = _dense_ffn(x2, ffn_norm[layer], dense_w_gate, dense_w_up, dense_w_down, i)
        else:
            x2 = _moe_ffn(x2, ffn_norm[layer], moe_w_router, moe_w1, moe_w3, moe_w2, i)
    return x2.reshape(bsz, seq, d)
```
